```python
import functools
import jax, jax.numpy as jnp
from jax import lax
import numpy as np

D_MODEL = 1024
BATCH = 4
SEQ = 4096
DEPTH = 1
DEC_BATCH = 32
DEC_SEQ = 8
PAST_LEN = 16384
PAGE_SIZE = 128

N_HEADS = 8
HEAD_DIM = 64
N_KV_HEADS = 2
GROUP = N_HEADS // N_KV_HEADS
ATTN_DIM = N_HEADS * HEAD_DIM
KV_DIM = N_KV_HEADS * HEAD_DIM
N_IDX_HEADS = 4
IDX_DIM = 64
TOPK_MAX = 256
CONV_DIM = D_MODEL - ATTN_DIM
CONV_WIDTH = 3
MIX_DIM = ATTN_DIM + CONV_DIM
D_FF = 2816
PLE_DIM = 256
ROPE_THETA = 10000.0
RMS_EPS = 1e-6
Q_BLOCK = 128
_SPLITS = (ATTN_DIM, KV_DIM, KV_DIM, N_IDX_HEADS * IDX_DIM, IDX_DIM, N_IDX_HEADS, CONV_DIM, CONV_DIM, CONV_DIM)
PROJ_DIM = ATTN_DIM + 2 * KV_DIM + N_IDX_HEADS * IDX_DIM + IDX_DIM + N_IDX_HEADS + 3 * CONV_DIM

kernel_name = 'hybrid_dsa_shortconv_macaron_step'


def rms_norm(x, g):
    xf = x.astype(jnp.float32)
    y = xf * lax.rsqrt(jnp.mean(xf * xf, axis=-1, keepdims=True) + RMS_EPS)
    return (y * g.astype(jnp.float32)).astype(x.dtype)


def swiglu(h, w_gate, w_up, w_down):
    return (jax.nn.silu(h @ w_gate) * (h @ w_up)) @ w_down


def rope(x, pos):
    half = x.shape[-1] // 2
    inv = ROPE_THETA ** (-jnp.arange(half, dtype=jnp.float32) / half)
    ang = pos.astype(jnp.float32)[:, None] * inv[None, :]
    cos = jnp.cos(ang)[:, None, :]
    sin = jnp.sin(ang)[:, None, :]
    xf = x.astype(jnp.float32)
    x1, x2 = xf[..., :half], xf[..., half:]
    return jnp.concatenate([x1 * cos - x2 * sin, x2 * cos + x1 * sin], axis=-1).astype(x.dtype)


def split_proj(z):
    parts, o = [], 0
    for w in _SPLITS:
        parts.append(z[..., o:o + w])
        o += w
    return parts


def project_heads(z, pos):
    q, k, v, iq, ik, iw, cb, cc, ch = split_proj(z)
    b, t = z.shape[0], z.shape[1]
    q = rope(q.reshape(b, t, N_HEADS, HEAD_DIM), pos)
    k = rope(k.reshape(b, t, N_KV_HEADS, HEAD_DIM), pos)
    v = v.reshape(b, t, N_KV_HEADS, HEAD_DIM)
    iq = rope(iq.reshape(b, t, N_IDX_HEADS, IDX_DIM), pos)
    ik = rope(ik.reshape(b, t, 1, IDX_DIM), pos)[:, :, 0]
    return q, k, v, iq, ik, iw, cb, cc, ch


def indexer_select(iq, iw, ik, qpos, topk):
    logits = jnp.einsum('bthd,bld->bthl', iq.astype(jnp.float32), ik.astype(jnp.float32)) * (IDX_DIM ** -0.5)
    w = iw.astype(jnp.float32) * (N_IDX_HEADS ** -0.5)
    score = jnp.einsum('bth,bthl->btl', w, jax.nn.relu(logits))
    adm = jnp.arange(ik.shape[1])[None, :] <= qpos[:, None]
    score = jnp.where(adm[None], score, -jnp.inf)
    _, idx = lax.top_k(score, topk)
    valid = idx <= qpos[None, :, None]
    return idx, valid


def gqa_attend(q, k_sel, v_sel, valid):
    b, t = q.shape[0], q.shape[1]
    qg = q.reshape(b, t, N_KV_HEADS, GROUP, HEAD_DIM).astype(jnp.float32)
    s = jnp.einsum('btngd,btknd->btngk', qg, k_sel.astype(jnp.float32)) * (HEAD_DIM ** -0.5)
    s = jnp.where(valid[:, :, None, None, :], s, -jnp.inf)
    p = jax.nn.softmax(s, axis=-1)
    o = jnp.einsum('btngk,btknd->btngd', p, v_sel.astype(jnp.float32))
    return o.reshape(b, t, ATTN_DIM).astype(q.dtype)


def short_conv(cb, cc, ch, buf, conv_w):
    u = cc * ch
    t = u.shape[1]
    up = jnp.concatenate([buf.astype(u.dtype), u], axis=1)
    y = conv_w[0] * up[:, 0:t]
    for j in range(1, CONV_WIDTH):
        y = y + conv_w[j] * up[:, j:j + t]
    return cb * y, up[:, t:]


def prompt_mixer(z, conv_w):
    b, s = z.shape[0], z.shape[1]
    pos = jnp.arange(s, dtype=jnp.int32)
    q, k, v, iq, ik, iw, cb, cc, ch = project_heads(z, pos)
    topk = min(TOPK_MAX, s // 4)
    nb = s // Q_BLOCK
    bidx = jnp.arange(b)[:, None, None]

    def to_blocks(a):
        return jnp.moveaxis(a.reshape((b, nb, Q_BLOCK) + a.shape[2:]), 1, 0)

    def block(args):
        qb, iqb, iwb, qpos = args
        idx, valid = indexer_select(iqb, iwb, ik, qpos, topk)
        return gqa_attend(qb, k[bidx, idx], v[bidx, idx], valid)

    attn = lax.map(block, (to_blocks(q), to_blocks(iq), to_blocks(iw), pos.reshape(nb, Q_BLOCK)))
    attn = jnp.moveaxis(attn, 0, 1).reshape(b, s, ATTN_DIM)
    buf = jnp.zeros((b, CONV_WIDTH - 1, CONV_DIM), z.dtype)
    conv_out, conv_state = short_conv(cb, cc, ch, buf, conv_w)
    return jnp.concatenate([attn, conv_out], axis=-1), (k, v, ik, conv_state)


def sample_mixer(z, conv_w, cache_k, cache_v, cache_idx_k, conv_buf, page_table):
    db, t = z.shape[0], z.shape[1]
    past = page_table.shape[1] * PAGE_SIZE
    pos = past + jnp.arange(t, dtype=jnp.int32)
    q, k, v, iq, ik, iw, cb, cc, ch = project_heads(z, pos)
    topk = min(TOPK_MAX, (past + t) // 4)
    ik_past = cache_idx_k[page_table].reshape(db, past, IDX_DIM)
    ik_all = jnp.concatenate([ik_past.astype(ik.dtype), ik], axis=1)
    idx, valid = indexer_select(iq, iw, ik_all, pos, topk)
    bidx = jnp.arange(db)[:, None, None]
    is_past = (idx < past)[..., None, None]
    pidx = jnp.minimum(idx, past - 1)
    phys = page_table[bidx, pidx // PAGE_SIZE]
    off = pidx % PAGE_SIZE
    nidx = jnp.clip(idx - past, 0, t - 1)

    def select_rows(cache, new):
        return jnp.where(is_past, cache[phys, off].astype(new.dtype), new[bidx, nidx])

    attn = gqa_attend(q, select_rows(cache_k, k), select_rows(cache_v, v), valid)
    conv_out, conv_state = short_conv(cb, cc, ch, conv_buf, conv_w)
    return jnp.concatenate([attn, conv_out], axis=-1), (k, v, ik, conv_state)


def decoder_layer(x, pe, mixer, g_ffn1, w1_gate, w1_up, w1_down, g_mix, w_in, w_out,
                  g_ffn2, w2_gate, w2_up, w2_down, g_ple, w_ple_gate, w_ple_proj):
    x = x + 0.5 * swiglu(rms_norm(x, g_ffn1), w1_gate, w1_up, w1_down)
    mixed, state = mixer(rms_norm(x, g_mix) @ w_in)
    x = x + mixed @ w_out
    x = x + 0.5 * swiglu(rms_norm(x, g_ffn2), w2_gate, w2_up, w2_down)
    gate = jax.nn.sigmoid(rms_norm(x, g_ple) @ w_ple_gate)
    x = x + gate * (pe @ w_ple_proj)
    return x, state


def setup_inputs(seed: int = 0) -> dict:
    key = jax.random.key(seed)
    ks = jax.random.split(key, 32)
    f32 = jnp.float32
    n_pages = PAST_LEN // PAGE_SIZE
    n_pool = (DEC_BATCH * n_pages * 5) // 4

    def nrm(k, shape, scale):
        return jax.random.normal(k, shape, f32) * scale

    def gain(k, shape):
        return 1.0 + 0.05 * jax.random.normal(k, shape, f32)

    page_table = jax.random.permutation(ks[0], n_pool)[:DEC_BATCH * n_pages].reshape(DEC_BATCH, n_pages).astype(jnp.int32)
    return {
        'x_prompt': nrm(ks[1], (BATCH, SEQ, D_MODEL), 1.0),
        'x_sample': nrm(ks[2], (DEC_BATCH, DEC_SEQ, D_MODEL), 1.0),
        'cache_k': nrm(ks[3], (DEPTH, n_pool, PAGE_SIZE, N_KV_HEADS, HEAD_DIM), 1.0),
        'cache_v': nrm(ks[4], (DEPTH, n_pool, PAGE_SIZE, N_KV_HEADS, HEAD_DIM), 1.0),
        'cache_idx_k': nrm(ks[5], (DEPTH, n_pool, PAGE_SIZE, IDX_DIM), 1.0),
        'state_conv': nrm(ks[6], (DEPTH, DEC_BATCH, CONV_WIDTH - 1, CONV_DIM), 1.0),
        'page_table': page_table,
        'p_prompt': nrm(ks[7], (DEPTH, BATCH, SEQ, PLE_DIM), 1.0),
        'p_sample': nrm(ks[8], (DEPTH, DEC_BATCH, DEC_SEQ, PLE_DIM), 1.0),
        'g_ffn1': gain(ks[9], (DEPTH, D_MODEL)),
        'w1_gate': nrm(ks[10], (DEPTH, D_MODEL, D_FF), D_MODEL ** -0.5),
        'w1_up': nrm(ks[11], (DEPTH, D_MODEL, D_FF), D_MODEL ** -0.5),
        'w1_down': nrm(ks[12], (DEPTH, D_FF, D_MODEL), D_FF ** -0.5),
        'g_mix': gain(ks[13], (DEPTH, D_MODEL)),
        'w_in': nrm(ks[14], (DEPTH, D_MODEL, PROJ_DIM), D_MODEL ** -0.5),
        'conv_w': nrm(ks[15], (DEPTH, CONV_WIDTH, CONV_DIM), CONV_WIDTH ** -0.5),
        'w_out': nrm(ks[16], (DEPTH, MIX_DIM, D_MODEL), MIX_DIM ** -0.5),
        'g_ffn2': gain(ks[17], (DEPTH, D_MODEL)),
        'w2_gate': nrm(ks[18], (DEPTH, D_MODEL, D_FF), D_MODEL ** -0.5),
        'w2_up': nrm(ks[19], (DEPTH, D_MODEL, D_FF), D_MODEL ** -0.5),
        'w2_down': nrm(ks[20], (DEPTH, D_FF, D_MODEL), D_FF ** -0.5),
        'g_ple': gain(ks[21], (DEPTH, D_MODEL)),
        'w_ple_gate': nrm(ks[22], (DEPTH, D_MODEL, D_MODEL), D_MODEL ** -0.5),
        'w_ple_proj': nrm(ks[23], (DEPTH, PLE_DIM, D_MODEL), PLE_DIM ** -0.5),
        'g_final': gain(ks[24], (D_MODEL,)),
    }


def reference(x_prompt, x_sample, cache_k, cache_v, cache_idx_k, state_conv, page_table, p_prompt, p_sample,
              g_ffn1, w1_gate, w1_up, w1_down, g_mix, w_in, conv_w, w_out,
              g_ffn2, w2_gate, w2_up, w2_down, g_ple, w_ple_gate, w_ple_proj, g_final):
    xp, xs = x_prompt, x_sample
    p_states, s_states = [], []
    for i in range(DEPTH):
        shared = (g_ffn1[i], w1_gate[i], w1_up[i], w1_down[i], g_mix[i], w_in[i], w_out[i],
                  g_ffn2[i], w2_gate[i], w2_up[i], w2_down[i], g_ple[i], w_ple_gate[i], w_ple_proj[i])
        xp, st_p = decoder_layer(xp, p_prompt[i], functools.partial(prompt_mixer, conv_w=conv_w[i]), *shared)
        xs, st_s = decoder_layer(
            xs, p_sample[i],
            functools.partial(sample_mixer, conv_w=conv_w[i], cache_k=cache_k[i], cache_v=cache_v[i],
                              cache_idx_k=cache_idx_k[i], conv_buf=state_conv[i], page_table=page_table),
            *shared)
        p_states.append(st_p)
        s_states.append(st_s)
    y_prompt = rms_norm(xp, g_final)
    y_sample = rms_norm(xs, g_final)
    k_prompt, v_prompt, idx_k_prompt, conv_prompt = (jnp.stack(a) for a in zip(*p_states))
    k_sample, v_sample, idx_k_sample, conv_sample = (jnp.stack(a) for a in zip(*s_states))
    return (y_prompt, y_sample, k_prompt, v_prompt, idx_k_prompt, conv_prompt,
            k_sample, v_sample, idx_k_sample, conv_sample)
```

```python
import functools

import jax
import jax.numpy as jnp
from jax import lax
from jax.experimental import pallas as pl
from jax.experimental.pallas import tpu as pltpu

F32 = jnp.float32
BF16 = jnp.bfloat16
I32 = jnp.int32

D_MODEL = 1024
N_HEADS = 8
HEAD_DIM = 64
N_KV_HEADS = 2
ATTN_DIM = N_HEADS * HEAD_DIM
KV_DIM = N_KV_HEADS * HEAD_DIM
N_IDX_HEADS = 4
IDX_DIM = 64
TOPK_MAX = 256
CONV_DIM = D_MODEL - ATTN_DIM
CONV_WIDTH = 3
D_FF = 2816
PLE_DIM = 256
PAGE_SIZE = 128
ROPE_THETA = 10000.0
RMS_EPS = 1e-6

LANES = 128
SUBLANES = 8
VMEM_LIMIT = 56 * 1024 * 1024

INT_MIN = -(2 ** 31)
NEG_BIG = -1e30

_Q0, _K0, _V0, _IQ0, _IK0, _CB0, _CC0, _CH0, _PROJ_W = 0, 512, 640, 768, 1024, 1152, 1664, 2176, 2688
_HEAD_PERM = (0, 4, 1, 5, 2, 6, 3, 7)


def _cparams(sem):
    return pltpu.CompilerParams(dimension_semantics=sem, vmem_limit_bytes=VMEM_LIMIT)


def _dot(a, b):
    return jnp.dot(a, b, preferred_element_type=F32)


def _dot_t(a, b):
    return lax.dot_general(a, b, (((1,), (1,)), ((), ())), preferred_element_type=F32)


def _rms(x, g):
    ms = jnp.mean(x * x, axis=-1, keepdims=True)
    return x * lax.rsqrt(ms + RMS_EPS) * g


def _ffn_body(x_ref, g_ref, wg_ref, wu_ref, wd_ref, o_ref, h_ref, acc_ref):
    j = pl.program_id(1)

    @pl.when(j == 0)
    def _():
        h_ref[...] = _rms(x_ref[...], g_ref[...]).astype(BF16)

    h = h_ref[...]
    gate = _dot(h, wg_ref[...])
    up = _dot(h, wu_ref[...])
    act = (gate * jax.nn.sigmoid(gate) * up).astype(BF16)
    part = _dot(act, wd_ref[...])

    @pl.when(j == 0)
    def _():
        acc_ref[...] = part

    @pl.when(j > 0)
    def _():
        acc_ref[...] += part

    @pl.when(j == pl.num_programs(1) - 1)
    def _():
        o_ref[...] = x_ref[...] + 0.5 * acc_ref[...]


def _ffn(x, g, wg, wu, wd, tm):
    n = x.shape[0]
    ff_chunk = D_FF // 2
    return pl.pallas_call(
        _ffn_body,
        grid=(n // tm, D_FF // ff_chunk),
        in_specs=[
            pl.BlockSpec((tm, D_MODEL), lambda i, j: (i, 0)),
            pl.BlockSpec((1, D_MODEL), lambda i, j: (0, 0)),
            pl.BlockSpec((D_MODEL, ff_chunk), lambda i, j: (0, j)),
            pl.BlockSpec((D_MODEL, ff_chunk), lambda i, j: (0, j)),
            pl.BlockSpec((ff_chunk, D_MODEL), lambda i, j: (j, 0)),
        ],
        out_specs=pl.BlockSpec((tm, D_MODEL), lambda i, j: (i, 0)),
        out_shape=jax.ShapeDtypeStruct((n, D_MODEL), F32),
        scratch_shapes=[pltpu.VMEM((tm, D_MODEL), BF16), pltpu.VMEM((tm, D_MODEL), F32)],
        compiler_params=_cparams(("arbitrary", "arbitrary")),
        name="ffn",
    )(x, g, wg, wu, wd)


def _swap32(x):
    w = x.shape[-1]
    lane = lax.broadcasted_iota(I32, x.shape, 1)
    return jnp.where((lane & 63) < 32, pltpu.roll(x, w - 32, 1), pltpu.roll(x, 32, 1))


def _rope_cols(z, cos, sin):
    cols = []
    for c in range(z.shape[-1] // LANES):
        zc = z[:, c * LANES:(c + 1) * LANES]
        cols.append(zc * cos + _swap32(zc) * sin)
    return cols[0] if len(cols) == 1 else jnp.concatenate(cols, axis=1)


def _proj_body(*refs, tm, tiles_per_seq, halo):
    if halo:
        (x_ref, g_ref, w_ref, cos_ref, sin_ref, cosk_ref, sink_ref, cw_ref, h1_ref, h2_ref,
         q_ref, iq_ref, k_ref, v_ref, ikw_ref, kb_ref, vb_ref, ik2_ref, cy_ref, u_ref) = refs
    else:
        (x_ref, g_ref, w_ref, cos_ref, sin_ref, cosk_ref, sink_ref, cw_ref,
         q_ref, iq_ref, k_ref, v_ref, ikw_ref, kb_ref, vb_ref, ik2_ref, cy_ref, u_ref, carry_ref) = refs

    h = _rms(x_ref[...], g_ref[...]).astype(BF16)
    cos, sin = cos_ref[...], sin_ref[...]

    zq = _dot(h, w_ref[:, _Q0:_K0])
    q_ref[...] = (_rope_cols(zq, cos, sin) * (HEAD_DIM ** -0.5)).astype(BF16)
    kr = _rope_cols(_dot(h, w_ref[:, _K0:_V0]), cos, sin)
    k_ref[...] = kr
    kb_ref[...] = kr.astype(BF16)
    zv = _dot(h, w_ref[:, _V0:_IQ0])
    v_ref[...] = zv
    vb_ref[...] = zv.astype(BF16)
    ziq = _dot(h, w_ref[:, _IQ0:_IK0])
    iq_ref[...] = (_rope_cols(ziq, cos, sin) * (IDX_DIM ** -0.5)).astype(BF16)
    ikr = _rope_cols(_dot(h, w_ref[:, _IK0:_CB0]), cosk_ref[...], sink_ref[...])
    ikw_ref[...] = ikr
    lane = lax.broadcasted_iota(I32, ikr.shape, 1)
    ik2_ref[...] = jnp.where(lane < IDX_DIM, ikr, pltpu.roll(ikr, IDX_DIM, 1)).astype(BF16)

    cb = _dot(h, w_ref[:, _CB0:_CC0])
    u = _dot(h, w_ref[:, _CC0:_CH0]) * _dot(h, w_ref[:, _CH0:_PROJ_W])
    row = lax.broadcasted_iota(I32, u.shape, 0)
    r1 = pltpu.roll(u, 1, 0)
    r2 = pltpu.roll(u, 2, 0)
    if halo:
        t = row & (SUBLANES - 1)
        us1 = jnp.where(t == 0, h1_ref[...], r1)
        us2 = jnp.where(t < 2, h2_ref[...], r2)
        u_ref[...] = u
    else:
        @pl.when(pl.program_id(0) % tiles_per_seq == 0)
        def _():
            carry_ref[...] = jnp.zeros_like(carry_ref)

        c0 = carry_ref[SUBLANES - 2:SUBLANES - 1, :]
        c1 = carry_ref[SUBLANES - 1:SUBLANES, :]
        us1 = jnp.where(row == 0, c1, r1)
        us2 = jnp.where(row == 0, c0, jnp.where(row == 1, c1, r2))
        tail = u[tm - SUBLANES:tm, :]
        carry_ref[...] = tail
        u_ref[0] = tail
    cw = cw_ref[...]
    y = cb * (cw[0:1, :] * us2 + cw[1:2, :] * us1 + cw[2:3, :] * u)
    cy_ref[...] = y.astype(BF16)


def _proj(x, g, w, cos, sin, cosk, sink, cw, tm, tiles_per_seq, halos=None):
    n = x.shape[0]
    n_tiles = n // tm
    halo = halos is not None
    tok = lambda width: pl.BlockSpec((tm, width), lambda i: (i, 0))
    tab = pl.BlockSpec((tm, LANES), lambda i: (i % tiles_per_seq, 0))
    full = lambda a: pl.BlockSpec(a.shape, lambda i: (0,) * a.ndim)
    in_specs = [tok(D_MODEL), full(g), full(w), tab, tab, tab, tab, full(cw)]
    args = [x, g, w, cos, sin, cosk, sink, cw]
    out_shape = [
        jax.ShapeDtypeStruct((n, ATTN_DIM), BF16),
        jax.ShapeDtypeStruct((n, N_IDX_HEADS * IDX_DIM), BF16),
        jax.ShapeDtypeStruct((n, KV_DIM), F32),
        jax.ShapeDtypeStruct((n, KV_DIM), F32),
        jax.ShapeDtypeStruct((n, LANES), F32),
        jax.ShapeDtypeStruct((n, KV_DIM), BF16),
        jax.ShapeDtypeStruct((n, KV_DIM), BF16),
        jax.ShapeDtypeStruct((n, LANES), BF16),
        jax.ShapeDtypeStruct((n, CONV_DIM), BF16),
    ]
    out_specs = [tok(ATTN_DIM), tok(N_IDX_HEADS * IDX_DIM), tok(KV_DIM), tok(KV_DIM), tok(LANES),
                 tok(KV_DIM), tok(KV_DIM), tok(LANES), tok(CONV_DIM)]
    scratch = []
    if halo:
        in_specs += [tok(CONV_DIM), tok(CONV_DIM)]
        args += list(halos)
        out_shape.append(jax.ShapeDtypeStruct((n, CONV_DIM), F32))
        out_specs.append(tok(CONV_DIM))
    else:
        n_seq = n_tiles // tiles_per_seq
        out_shape.append(jax.ShapeDtypeStruct((n_seq, SUBLANES, CONV_DIM), F32))
        out_specs.append(pl.BlockSpec((1, SUBLANES, CONV_DIM), lambda i: (i // tiles_per_seq, 0, 0)))
        scratch.append(pltpu.VMEM((SUBLANES, CONV_DIM), F32))
    return pl.pallas_call(
        functools.partial(_proj_body, tm=tm, tiles_per_seq=tiles_per_seq, halo=halo),
        grid=(n_tiles,),
        in_specs=in_specs,
        out_specs=out_specs,
        out_shape=out_shape,
        scratch_shapes=scratch,
        compiler_params=_cparams(("arbitrary",)),
        name="proj_sample" if halo else "proj_prompt",
    )(*args)


def _sort_key(s):
    bits = lax.bitcast_convert_type(s + 0.0, I32)
    return bits ^ ((bits >> 31) & 0x7FFFFFFF)


def _lane_fold(ind):
    acc = ind[:, 0:LANES]
    for j in range(1, ind.shape[-1] // LANES):
        acc = acc + ind[:, j * LANES:(j + 1) * LANES]
    return acc


def _select_threshold(count_fn, rows, topk):
    def step(b, t):
        cand = t + lax.shift_left(jnp.int32(1), 31 - b)
        cnt = count_fn(lambda kk: kk >= cand)
        return jnp.where(cnt >= topk, cand, t)

    return lax.fori_loop(0, 32, step, jnp.full((rows, 1), INT_MIN, I32))


def _tie_cutoff(count_fn, t, need, idx_bits):
    def step(b, c):
        cand = c + lax.shift_left(jnp.int32(1), idx_bits - 1 - b)
        cnt = count_fn(lambda kk, col: (kk == t) & (col < cand))
        return jnp.where(cnt <= need, cand, c)

    return lax.fori_loop(0, idx_bits, step, jnp.zeros(t.shape, I32))


def _attn_prompt_body(q_ref, iq_ref, w_ref, ik2_ref, kb_ref, vb_ref, o_ref,
                      keys_ref, qs_ref, iqs_ref, m_ref, l_ref, acc_ref, *, qb, kc, topk, idx_bits):
    i = pl.program_id(1)
    n_chunks = lax.shift_right_logical((i + 1) * qb + (kc - 1), jnp.int32(kc.bit_length() - 1))
    lo = lax.broadcasted_iota(I32, (qb, LANES), 1) < HEAD_DIM

    for c in range(N_HEADS // 2):
        qc = q_ref[:, c * LANES:(c + 1) * LANES]
        qs_ref[c * qb:(c + 1) * qb, :] = jnp.where(lo, qc, jnp.zeros_like(qc))
        qs_ref[(c + 4) * qb:(c + 5) * qb, :] = jnp.where(lo, jnp.zeros_like(qc), qc)
    for c in range(N_IDX_HEADS // 2):
        ic = iq_ref[:, c * LANES:(c + 1) * LANES]
        iqs_ref[(2 * c) * qb:(2 * c + 1) * qb, :] = jnp.where(lo, ic, jnp.zeros_like(ic))
        iqs_ref[(2 * c + 1) * qb:(2 * c + 2) * qb, :] = jnp.where(lo, jnp.zeros_like(ic), ic)

    wq = w_ref[...]
    wcol = [wq[:, IDX_DIM + h:IDX_DIM + h + 1] * (N_IDX_HEADS ** -0.5) for h in range(N_IDX_HEADS)]
    row_g = i * qb + lax.broadcasted_iota(I32, (qb, kc), 0)
    col_l = lax.broadcasted_iota(I32, (qb, kc), 1)

    def score_chunk(c, carry):
        off = pl.multiple_of(c * kc, kc)
        logits = _dot_t(iqs_ref[...], ik2_ref[pl.ds(off, kc), :])
        sc = wcol[0] * jnp.maximum(logits[0:qb], 0.0)
        for h in range(1, N_IDX_HEADS):
            sc = sc + wcol[h] * jnp.maximum(logits[h * qb:(h + 1) * qb], 0.0)
        keys_ref[c] = jnp.where(off + col_l <= row_g, _sort_key(sc), INT_MIN)
        return carry

    lax.fori_loop(0, n_chunks, score_chunk, 0)

    def count_keys(pred):
        def body(c, acc):
            return acc + _lane_fold(jnp.where(pred(keys_ref[c]), 1.0, 0.0))
        acc = lax.fori_loop(0, n_chunks, body, jnp.zeros((qb, LANES), F32))
        return jnp.sum(acc, axis=1, keepdims=True)

    def count_keys_cols(pred):
        def body(c, acc):
            return acc + _lane_fold(jnp.where(pred(keys_ref[c], c * kc + col_l), 1.0, 0.0))
        acc = lax.fori_loop(0, n_chunks, body, jnp.zeros((qb, LANES), F32))
        return jnp.sum(acc, axis=1, keepdims=True)

    thr = _select_threshold(count_keys, qb, float(topk))
    thr = jnp.maximum(thr, INT_MIN + 1)
    over = count_keys(lambda kk: kk >= thr) > float(topk)

    @pl.when(jnp.max(jnp.where(over, 1.0, 0.0)) > 0.0)
    def _():
        need = float(topk) - count_keys(lambda kk: kk > thr)
        cut = _tie_cutoff(count_keys_cols, thr, need, idx_bits)
        cut = jnp.where(over, cut, jnp.int32(1 << idx_bits))

        def drop(c, carry):
            kk = keys_ref[c]
            keys_ref[c] = jnp.where((kk == thr) & (c * kc + col_l >= cut), INT_MIN, kk)
            return carry

        lax.fori_loop(0, n_chunks, drop, 0)

    m_ref[...] = jnp.full(m_ref.shape, NEG_BIG, F32)
    l_ref[...] = jnp.zeros(l_ref.shape, F32)
    acc_ref[...] = jnp.zeros(acc_ref.shape, F32)

    def attn_chunk(c, carry):
        off = pl.multiple_of(c * kc, kc)
        s = _dot_t(qs_ref[...], kb_ref[pl.ds(off, kc), :])
        sel = keys_ref[c] >= thr
        s = jnp.where(sel[None], s.reshape(N_HEADS, qb, kc), NEG_BIG).reshape(N_HEADS * qb, kc)
        m_old = m_ref[...]
        m_new = jnp.maximum(m_old, jnp.max(s, axis=1, keepdims=True))
        alpha = jnp.exp(m_old - m_new)
        p = jnp.exp(s - m_new)
        l_ref[...] = alpha * l_ref[...] + jnp.sum(p, axis=1, keepdims=True)
        acc_ref[...] = alpha * acc_ref[...] + _dot(p.astype(BF16), vb_ref[pl.ds(off, kc), :])
        m_ref[...] = m_new
        return carry

    lax.fori_loop(0, n_chunks, attn_chunk, 0)

    out = acc_ref[...] / l_ref[...]
    for c in range(N_HEADS // 2):
        oc = jnp.where(lo, out[c * qb:(c + 1) * qb], out[(c + 4) * qb:(c + 5) * qb])
        o_ref[:, c * LANES:(c + 1) * LANES] = oc.astype(BF16)


def _attn_prompt(q, iq, ikw, ik2, kb, vb, batch, seq, qb, kc):
    nqb = seq // qb
    topk = min(TOPK_MAX, seq // 4)
    tokq = lambda width: pl.BlockSpec((qb, width), lambda b, i: (b * nqb + i, 0))
    seqk = pl.BlockSpec((None, seq, LANES), lambda b, i: (b, 0, 0))
    as_seq = lambda a: a.reshape(batch, seq, LANES)
    return pl.pallas_call(
        functools.partial(_attn_prompt_body, qb=qb, kc=kc, topk=topk, idx_bits=(seq - 1).bit_length()),
        grid=(batch, nqb),
        in_specs=[tokq(ATTN_DIM), tokq(N_IDX_HEADS * IDX_DIM), tokq(LANES), seqk, seqk, seqk],
        out_specs=tokq(ATTN_DIM),
        out_shape=jax.ShapeDtypeStruct((batch * seq, ATTN_DIM), BF16),
        scratch_shapes=[
            pltpu.VMEM((seq // kc, qb, kc), I32),
            pltpu.VMEM((N_HEADS * qb, LANES), BF16),
            pltpu.VMEM((N_IDX_HEADS * qb, LANES), BF16),
            pltpu.VMEM((N_HEADS * qb, 1), F32),
            pltpu.VMEM((N_HEADS * qb, 1), F32),
            pltpu.VMEM((N_HEADS * qb, LANES), F32),
        ],
        compiler_params=_cparams(("arbitrary", "arbitrary")),
        name="attn_prompt",
    )(q, iq, ikw, as_seq(ik2), as_seq(kb), as_seq(vb))


def _stack_heads(x_bf, n_heads):
    x = x_bf.astype(F32)
    return jnp.concatenate([x[:, h * HEAD_DIM:(h + 1) * HEAD_DIM] for h in range(n_heads)], axis=0).astype(BF16)


def _index_scores(iqs, wcol, keys_bf, t):
    logits = _dot_t(iqs, keys_bf)
    sc = wcol[0] * jnp.maximum(logits[0:t], 0.0)
    for h in range(1, N_IDX_HEADS):
        sc = sc + wcol[h] * jnp.maximum(logits[h * t:(h + 1) * t], 0.0)
    return sc


def _head_weights(w_ref):
    wq = w_ref[...]
    return [wq[:, IDX_DIM + h:IDX_DIM + h + 1] * (N_IDX_HEADS ** -0.5) for h in range(N_IDX_HEADS)]


def _sample_scores_body(pt_ref, iq_ref, w_ref, *refs, pg, t):
    del pt_ref
    pages, keys_ref = refs[:pg], refs[pg]
    iqs = _stack_heads(iq_ref[...], N_IDX_HEADS)
    wcol = _head_weights(w_ref)
    for j in range(pg):
        keys_ref[j] = _sort_key(_index_scores(iqs, wcol, pages[j][...].astype(BF16), t))


def _sample_scores(page_table, iq, ikw, cache_idx, db, t, pg):
    n_pages = page_table.shape[1]
    page_spec = lambda j: pl.BlockSpec((None, PAGE_SIZE, IDX_DIM),
                                       lambda b, g, pt: (pt[b, g * pg + j], 0, 0))
    grid_spec = pltpu.PrefetchScalarGridSpec(
        num_scalar_prefetch=1,
        grid=(db, n_pages // pg),
        in_specs=[pl.BlockSpec((t, N_IDX_HEADS * IDX_DIM), lambda b, g, pt: (b, 0)),
                  pl.BlockSpec((t, LANES), lambda b, g, pt: (b, 0))] + [page_spec(j) for j in range(pg)],
        out_specs=pl.BlockSpec((None, pg, t, PAGE_SIZE), lambda b, g, pt: (b, g, 0, 0)),
    )
    return pl.pallas_call(
        functools.partial(_sample_scores_body, pg=pg, t=t),
        grid_spec=grid_spec,
        out_shape=jax.ShapeDtypeStruct((db, n_pages, t, PAGE_SIZE), I32),
        compiler_params=_cparams(("arbitrary", "arbitrary")),
        name="sample_scores",
    )(page_table, iq, ikw, *([cache_idx] * pg))


def _sample_attn_body(pt_ref, keys_ref, q_ref, iq_ref, w_ref, ikn_ref, kn_ref, vn_ref, *refs,
                      pg, t, topk, idx_bits, kc):
    del pt_ref
    kpages, vpages = refs[:pg], refs[pg:2 * pg]
    o_ref = refs[2 * pg]
    nkeys_ref, qs_ref, thr_ref, m_ref, l_ref, acc_ref = refs[2 * pg + 1:]
    g = pl.program_id(1)
    n_pages = keys_ref.shape[0]
    past = n_pages * PAGE_SIZE
    n_kchunks = n_pages // kc
    rows = N_HEADS * t

    def flash_update(s, sel, v):
        n = s.shape[-1]
        s = jnp.where(sel[None], s.reshape(N_HEADS, t, n), NEG_BIG).reshape(rows, n)
        m_old = m_ref[...]
        m_new = jnp.maximum(m_old, jnp.max(s, axis=1, keepdims=True))
        alpha = jnp.exp(m_old - m_new)
        p = jnp.exp(s - m_new)
        l_ref[...] = alpha * l_ref[...] + jnp.sum(p, axis=1, keepdims=True)
        acc_ref[...] = alpha * acc_ref[...] + _dot(p.astype(BF16), v)
        m_ref[...] = m_new

    @pl.when(g == 0)
    def _():
        lo = lax.broadcasted_iota(I32, (t, LANES), 1) < HEAD_DIM
        for c in range(N_HEADS // 2):
            qc = q_ref[:, c * LANES:(c + 1) * LANES]
            qs_ref[c * t:(c + 1) * t, :] = jnp.where(lo, qc, jnp.zeros_like(qc))
            qs_ref[(c + 4) * t:(c + 5) * t, :] = jnp.where(lo, jnp.zeros_like(qc), qc)

        iqs = _stack_heads(iq_ref[...], N_IDX_HEADS)
        ikn = jnp.concatenate([ikn_ref[...][:, 0:IDX_DIM], jnp.zeros((LANES - t, IDX_DIM), F32)], axis=0)
        sc = _index_scores(iqs, _head_weights(w_ref), ikn.astype(BF16), t)
        col_n = lax.broadcasted_iota(I32, (t, LANES), 1)
        row_n = lax.broadcasted_iota(I32, (t, LANES), 0)
        nkeys_ref[...] = jnp.where(col_n <= row_n, _sort_key(sc), INT_MIN)

        col_c = (lax.broadcasted_iota(I32, (kc, t, PAGE_SIZE), 0) * PAGE_SIZE
                 + lax.broadcasted_iota(I32, (kc, t, PAGE_SIZE), 2))

        def count_keys(pred):
            def body(c, acc):
                kk = keys_ref[pl.ds(c * kc, kc)]
                return acc + jnp.sum(jnp.where(pred(kk), 1.0, 0.0), axis=0)
            acc = lax.fori_loop(0, n_kchunks, body, jnp.zeros((t, LANES), F32))
            acc = acc + jnp.where(pred(nkeys_ref[...]), 1.0, 0.0)
            return jnp.sum(acc, axis=1, keepdims=True)

        def count_keys_cols(pred):
            def body(c, acc):
                kk = keys_ref[pl.ds(c * kc, kc)]
                return acc + jnp.sum(jnp.where(pred(kk, c * (kc * PAGE_SIZE) + col_c), 1.0, 0.0), axis=0)
            acc = lax.fori_loop(0, n_kchunks, body, jnp.zeros((t, LANES), F32))
            acc = acc + jnp.where(pred(nkeys_ref[...], past + col_n), 1.0, 0.0)
            return jnp.sum(acc, axis=1, keepdims=True)

        thr = _select_threshold(count_keys, t, float(topk))
        thr = jnp.maximum(thr, INT_MIN + 1)
        over = count_keys(lambda kk: kk >= thr) > float(topk)
        need = float(topk) - count_keys(lambda kk: kk > thr)
        cut = _tie_cutoff(count_keys_cols, thr, need, idx_bits)
        cut = jnp.where(over, cut, jnp.int32(1 << idx_bits))
        thr_ref[:, 0:1] = thr
        thr_ref[:, 1:2] = cut
        m_ref[...] = jnp.full(m_ref.shape, NEG_BIG, F32)
        l_ref[...] = jnp.zeros(l_ref.shape, F32)
        acc_ref[...] = jnp.zeros(acc_ref.shape, F32)

    thr = thr_ref[:, 0:1]
    cut = thr_ref[:, 1:2]

    def selected(kk, col):
        return (kk > thr) | ((kk == thr) & (col < cut))

    qs = qs_ref[...]
    col_p = lax.broadcasted_iota(I32, (t, PAGE_SIZE), 1)
    for j in range(pg):
        page = g * pg + j
        s = _dot_t(qs, kpages[j][...].astype(BF16))
        flash_update(s, selected(keys_ref[page], page * PAGE_SIZE + col_p), vpages[j][...].astype(BF16))

    @pl.when(g == pl.num_programs(1) - 1)
    def _():
        pad = jnp.zeros((LANES - t, LANES), F32)
        kn = jnp.concatenate([kn_ref[...], pad], axis=0).astype(BF16)
        vn = jnp.concatenate([vn_ref[...], pad], axis=0).astype(BF16)
        s = _dot_t(qs, kn)
        flash_update(s, selected(nkeys_ref[...], past + col_p), vn)
        lo = lax.broadcasted_iota(I32, (t, LANES), 1) < HEAD_DIM
        out = acc_ref[...] / l_ref[...]
        for c in range(N_HEADS // 2):
            oc = jnp.where(lo, out[c * t:(c + 1) * t], out[(c + 4) * t:(c + 5) * t])
            o_ref[:, c * LANES:(c + 1) * LANES] = oc.astype(BF16)


def _sample_attn(page_table, keys, q, iq, ikw, k_new, v_new, cache_k, cache_v, db, t, pg):
    n_pages = page_table.shape[1]
    past = n_pages * PAGE_SIZE
    topk = min(TOPK_MAX, (past + t) // 4)
    page_spec = lambda j: pl.BlockSpec((None, PAGE_SIZE, KV_DIM),
                                       lambda b, g, pt: (pt[b, g * pg + j], 0, 0))
    tok = lambda width: pl.BlockSpec((t, width), lambda b, g, pt: (b, 0))
    grid_spec = pltpu.PrefetchScalarGridSpec(
        num_scalar_prefetch=1,
        grid=(db, n_pages // pg),
        in_specs=[pl.BlockSpec((None, n_pages, t, PAGE_SIZE), lambda b, g, pt: (b, 0, 0, 0)),
                  tok(ATTN_DIM), tok(N_IDX_HEADS * IDX_DIM), tok(LANES), tok(LANES), tok(KV_DIM), tok(KV_DIM)]
                 + [page_spec(j) for j in range(pg)] + [page_spec(j) for j in range(pg)],
        out_specs=tok(ATTN_DIM),
        scratch_shapes=[
            pltpu.VMEM((t, LANES), I32),
            pltpu.VMEM((N_HEADS * t, LANES), BF16),
            pltpu.VMEM((t, LANES), I32),
            pltpu.VMEM((N_HEADS * t, 1), F32),
            pltpu.VMEM((N_HEADS * t, 1), F32),
            pltpu.VMEM((N_HEADS * t, LANES), F32),
        ],
    )
    return pl.pallas_call(
        functools.partial(_sample_attn_body, pg=pg, t=t, topk=topk,
                          idx_bits=(past + LANES - 1).bit_length(), kc=16),
        grid_spec=grid_spec,
        out_shape=jax.ShapeDtypeStruct((db * t, ATTN_DIM), BF16),
        compiler_params=_cparams(("arbitrary", "arbitrary")),
        name="sample_attn",
    )(page_table, keys, q, iq, ikw, ikw, k_new, v_new, *([cache_k] * pg), *([cache_v] * pg))


def _out_body(x_ref, a_ref, c_ref, wa_ref, wc_ref, o_ref):
    o_ref[...] = x_ref[...] + _dot(a_ref[...], wa_ref[...]) + _dot(c_ref[...], wc_ref[...])


def _out_proj(x, attn, conv, wa, wc, tm):
    n = x.shape[0]
    tok = lambda width: pl.BlockSpec((tm, width), lambda i: (i, 0))
    full = lambda a: pl.BlockSpec(a.shape, lambda i: (0, 0))
    return pl.pallas_call(
        _out_body,
        grid=(n // tm,),
        in_specs=[tok(D_MODEL), tok(ATTN_DIM), tok(CONV_DIM), full(wa), full(wc)],
        out_specs=tok(D_MODEL),
        out_shape=jax.ShapeDtypeStruct((n, D_MODEL), F32),
        compiler_params=_cparams(("arbitrary",)),
        name="out_proj",
    )(x, attn, conv, wa, wc)


def _ple_body(x_ref, pe_ref, g_ref, wg_ref, wp_ref, gf_ref, o_ref):
    x = x_ref[...]
    gate = jax.nn.sigmoid(_dot(_rms(x, g_ref[...]).astype(BF16), wg_ref[...]))
    x = x + gate * _dot(pe_ref[...].astype(BF16), wp_ref[...])
    o_ref[...] = _rms(x, gf_ref[...])


def _ple_final(x, pe, g, wg, wp, gf, tm):
    n = x.shape[0]
    tok = lambda width: pl.BlockSpec((tm, width), lambda i: (i, 0))
    full = lambda a: pl.BlockSpec(a.shape, lambda i: (0, 0))
    return pl.pallas_call(
        _ple_body,
        grid=(n // tm,),
        in_specs=[tok(D_MODEL), tok(PLE_DIM), full(g), full(wg), full(wp), full(gf)],
        out_specs=tok(D_MODEL),
        out_shape=jax.ShapeDtypeStruct((n, D_MODEL), F32),
        compiler_params=_cparams(("arbitrary",)),
        name="ple_final",
    )(x, pe, g, wg, wp, gf)


def _rope_tables(pos):
    half = HEAD_DIM // 2
    inv = ROPE_THETA ** (-jnp.arange(half, dtype=F32) / half)
    ang = pos.astype(F32)[:, None] * inv[None, :]
    cos, sin = jnp.cos(ang), jnp.sin(ang)
    cos2 = jnp.concatenate([cos, cos], axis=1)
    sin2 = jnp.concatenate([-sin, sin], axis=1)
    cos128 = jnp.concatenate([cos2, cos2], axis=1)
    sin128 = jnp.concatenate([sin2, sin2], axis=1)
    cosk = jnp.concatenate([cos2, jnp.ones_like(cos2)], axis=1)
    sink = jnp.concatenate([sin2, jnp.zeros_like(sin2)], axis=1)
    return cos128, sin128, cosk, sink


def _prep_w_in(w_in):
    d = w_in.shape[0]
    q = w_in[:, :ATTN_DIM].reshape(d, N_HEADS, HEAD_DIM)[:, jnp.array(_HEAD_PERM)].reshape(d, ATTN_DIM)
    o = ATTN_DIM
    kv = w_in[:, o:o + 2 * KV_DIM]
    o += 2 * KV_DIM
    iq = w_in[:, o:o + N_IDX_HEADS * IDX_DIM]
    o += N_IDX_HEADS * IDX_DIM
    ikw = w_in[:, o:o + IDX_DIM + N_IDX_HEADS]
    o += IDX_DIM + N_IDX_HEADS
    ikw = jnp.pad(ikw, ((0, 0), (0, LANES - IDX_DIM - N_IDX_HEADS)))
    conv = w_in[:, o:]
    return jnp.concatenate([q, kv, iq, ikw, conv], axis=1).astype(BF16)


def _layer(x, pe, mixer, tm, g_ffn1, w1g, w1u, w1d, g_mix, w_in, conv_w, wo_a, wo_c,
           g_ffn2, w2g, w2u, w2d, g_ple, w_pg, w_pp, g_final):
    x1 = _ffn(x, g_ffn1, w1g, w1u, w1d, tm)
    attn, conv, state = mixer(x1, g_mix, w_in, conv_w)
    x2 = _out_proj(x1, attn, conv, wo_a, wo_c, tm)
    x3 = _ffn(x2, g_ffn2, w2g, w2u, w2d, tm)
    return _ple_final(x3, pe, g_ple, w_pg, w_pp, g_final, tm), state


def kernel(x_prompt, x_sample, cache_k, cache_v, cache_idx_k, state_conv, page_table, p_prompt, p_sample,
           g_ffn1, w1_gate, w1_up, w1_down, g_mix, w_in, conv_w, w_out,
           g_ffn2, w2_gate, w2_up, w2_down, g_ple, w_ple_gate, w_ple_proj, g_final):
    depth = w_in.shape[0]
    assert depth == 1, "single-layer step"
    batch, seq, _ = x_prompt.shape
    db, t, _ = x_sample.shape
    n_pages = page_table.shape[1]
    past = n_pages * PAGE_SIZE
    assert t == SUBLANES and seq % 512 == 0

    row = lambda gvec: gvec.reshape(1, -1)
    perm = jnp.array(_HEAD_PERM)
    wo_a = w_out[0, :ATTN_DIM].reshape(N_HEADS, HEAD_DIM, D_MODEL)[perm].reshape(ATTN_DIM, D_MODEL).astype(BF16)
    wo_c = w_out[0, ATTN_DIM:].astype(BF16)
    shared = (row(g_ffn1[0]), w1_gate[0].astype(BF16), w1_up[0].astype(BF16), w1_down[0].astype(BF16),
              row(g_mix[0]), _prep_w_in(w_in[0]), conv_w[0], wo_a, wo_c,
              row(g_ffn2[0]), w2_gate[0].astype(BF16), w2_up[0].astype(BF16), w2_down[0].astype(BF16),
              row(g_ple[0]), w_ple_gate[0].astype(BF16), w_ple_proj[0].astype(BF16), row(g_final))

    tm_p = 512
    tabs_p = _rope_tables(jnp.arange(seq, dtype=I32))

    def prompt_mixer(x1, g, w, cw):
        q, iq, k, v, ikw, kb, vb, ik2, cy, tail = _proj(x1, g, w, *tabs_p, cw, tm_p, seq // tm_p)
        attn = _attn_prompt(q, iq, ikw, ik2, kb, vb, batch, seq, qb=128, kc=512)
        return attn, cy, (k, v, ikw, tail)

    yp, (kp, vp, ikwp, tailp) = _layer(x_prompt.reshape(batch * seq, D_MODEL),
                                       p_prompt[0].reshape(batch * seq, PLE_DIM), prompt_mixer, tm_p, *shared)

    n_s = db * t
    tabs_s = tuple(jnp.tile(a, (db, 1)) for a in _rope_tables(past + jnp.arange(t, dtype=I32)))
    buf = state_conv[0]
    zero = jnp.zeros((db, t - 2, CONV_DIM), F32)
    halo1 = jnp.concatenate([buf[:, 1:2], jnp.zeros((db, t - 1, CONV_DIM), F32)], axis=1).reshape(n_s, CONV_DIM)
    halo2 = jnp.concatenate([buf, zero], axis=1).reshape(n_s, CONV_DIM)

    def sample_mixer(x1, g, w, cw):
        q, iq, k, v, ikw, kb, vb, ik2, cy, u = _proj(x1, g, w, *tabs_s, cw, n_s, 1, halos=(halo1, halo2))
        keys = _sample_scores(page_table, iq, ikw, cache_idx_k[0], db, t, pg=16)
        attn = _sample_attn(page_table, keys, q, iq, ikw, k, v, cache_k[0].reshape(-1, PAGE_SIZE, KV_DIM),
                            cache_v[0].reshape(-1, PAGE_SIZE, KV_DIM), db, t, pg=16)
        return attn, cy, (k, v, ikw, u)

    ys, (ks, vs, ikws, us) = _layer(x_sample.reshape(n_s, D_MODEL), p_sample[0].reshape(n_s, PLE_DIM),
                                    sample_mixer, n_s, *shared)

    return (yp.reshape(batch, seq, D_MODEL),
            ys.reshape(db, t, D_MODEL),
            kp.reshape(1, batch, seq, N_KV_HEADS, HEAD_DIM),
            vp.reshape(1, batch, seq, N_KV_HEADS, HEAD_DIM),
            ikwp[:, :IDX_DIM].reshape(1, batch, seq, IDX_DIM),
            tailp[:, SUBLANES - (CONV_WIDTH - 1):][None],
            ks.reshape(1, db, t, N_KV_HEADS, HEAD_DIM),
            vs.reshape(1, db, t, N_KV_HEADS, HEAD_DIM),
            ikws[:, :IDX_DIM].reshape(1, db, t, IDX_DIM),
            us.reshape(db, t, CONV_DIM)[:, t - (CONV_WIDTH - 1):][None])
```

```python
import functools

import jax
import jax.numpy as jnp
from jax import lax
from jax.experimental import pallas as pl
from jax.experimental.pallas import tpu as pltpu

F32 = jnp.float32
BF16 = jnp.bfloat16
I32 = jnp.int32

D_MODEL = 1024
N_HEADS = 8
HEAD_DIM = 64
N_KV_HEADS = 2
ATTN_DIM = N_HEADS * HEAD_DIM
KV_DIM = N_KV_HEADS * HEAD_DIM
N_IDX_HEADS = 4
IDX_DIM = 64
TOPK_MAX = 256
CONV_DIM = D_MODEL - ATTN_DIM
CONV_WIDTH = 3
D_FF = 2816
PLE_DIM = 256
PAGE_SIZE = 128
ROPE_THETA = 10000.0
RMS_EPS = 1e-6

LANES = 128
SUBLANES = 8
VMEM_LIMIT = 56 * 1024 * 1024

INT_MIN = -(2 ** 31)
NEG_BIG = -1e30

_Q0, _K0, _V0, _IQ0, _IK0, _CB0, _CC0, _CH0, _PROJ_W = 0, 512, 640, 768, 1024, 1152, 1664, 2176, 2688
_HEAD_PERM = (0, 4, 1, 5, 2, 6, 3, 7)


def _cparams(sem):
    return pltpu.CompilerParams(dimension_semantics=sem, vmem_limit_bytes=VMEM_LIMIT)


def _dot(a, b):
    return jnp.dot(a, b, preferred_element_type=F32)


def _dot_t(a, b):
    return lax.dot_general(a, b, (((1,), (1,)), ((), ())), preferred_element_type=F32)


def _rms(x, g):
    ms = jnp.mean(x * x, axis=-1, keepdims=True)
    return x * lax.rsqrt(ms + RMS_EPS) * g


def _ffn_body(x_ref, g_ref, wg_ref, wu_ref, wd_ref, o_ref, h_ref, acc_ref):
    j = pl.program_id(1)

    @pl.when(j == 0)
    def _():
        h_ref[...] = _rms(x_ref[...], g_ref[...]).astype(BF16)

    h = h_ref[...]
    gate = _dot(h, wg_ref[...])
    up = _dot(h, wu_ref[...])
    act = (gate * jax.nn.sigmoid(gate) * up).astype(BF16)
    part = _dot(act, wd_ref[...])

    @pl.when(j == 0)
    def _():
        acc_ref[...] = part

    @pl.when(j > 0)
    def _():
        acc_ref[...] += part

    @pl.when(j == pl.num_programs(1) - 1)
    def _():
        o_ref[...] = x_ref[...] + 0.5 * acc_ref[...]


def _ffn(x, g, wg, wu, wd, tm):
    n = x.shape[0]
    ff_chunk = D_FF // 2
    return pl.pallas_call(
        _ffn_body,
        grid=(n // tm, D_FF // ff_chunk),
        in_specs=[
            pl.BlockSpec((tm, D_MODEL), lambda i, j: (i, 0)),
            pl.BlockSpec((1, D_MODEL), lambda i, j: (0, 0)),
            pl.BlockSpec((D_MODEL, ff_chunk), lambda i, j: (0, j)),
            pl.BlockSpec((D_MODEL, ff_chunk), lambda i, j: (0, j)),
            pl.BlockSpec((ff_chunk, D_MODEL), lambda i, j: (j, 0)),
        ],
        out_specs=pl.BlockSpec((tm, D_MODEL), lambda i, j: (i, 0)),
        out_shape=jax.ShapeDtypeStruct((n, D_MODEL), F32),
        scratch_shapes=[pltpu.VMEM((tm, D_MODEL), BF16), pltpu.VMEM((tm, D_MODEL), F32)],
        compiler_params=_cparams(("arbitrary", "arbitrary")),
        name="ffn",
    )(x, g, wg, wu, wd)


def _swap32(x):
    w = x.shape[-1]
    lane = lax.broadcasted_iota(I32, x.shape, 1)
    return jnp.where((lane & 63) < 32, pltpu.roll(x, w - 32, 1), pltpu.roll(x, 32, 1))


def _rope_cols(z, cos, sin):
    cols = []
    for c in range(z.shape[-1] // LANES):
        zc = z[:, c * LANES:(c + 1) * LANES]
        cols.append(zc * cos + _swap32(zc) * sin)
    return cols[0] if len(cols) == 1 else jnp.concatenate(cols, axis=1)


def _proj_body(*refs, tm, tiles_per_seq, halo):
    if halo:
        (x_ref, g_ref, w_ref, cos_ref, sin_ref, cosk_ref, sink_ref, cw_ref, h1_ref, h2_ref,
         q_ref, iq_ref, k_ref, v_ref, ikw_ref, kb_ref, vb_ref, ik2_ref, cy_ref, u_ref) = refs
    else:
        (x_ref, g_ref, w_ref, cos_ref, sin_ref, cosk_ref, sink_ref, cw_ref,
         q_ref, iq_ref, k_ref, v_ref, ikw_ref, kb_ref, vb_ref, ik2_ref, cy_ref, u_ref, carry_ref) = refs

    h = _rms(x_ref[...], g_ref[...]).astype(BF16)
    cos, sin = cos_ref[...], sin_ref[...]

    zq = _dot(h, w_ref[:, _Q0:_K0])
    q_ref[...] = (_rope_cols(zq, cos, sin) * (HEAD_DIM ** -0.5)).astype(BF16)
    kr = _rope_cols(_dot(h, w_ref[:, _K0:_V0]), cos, sin)
    k_ref[...] = kr
    kb_ref[...] = kr.astype(BF16)
    zv = _dot(h, w_ref[:, _V0:_IQ0])
    v_ref[...] = zv
    vb_ref[...] = zv.astype(BF16)
    ziq = _dot(h, w_ref[:, _IQ0:_IK0])
    iq_ref[...] = (_rope_cols(ziq, cos, sin) * (IDX_DIM ** -0.5)).astype(BF16)
    ikr = _rope_cols(_dot(h, w_ref[:, _IK0:_CB0]), cosk_ref[...], sink_ref[...])
    ikw_ref[...] = ikr
    lane = lax.broadcasted_iota(I32, ikr.shape, 1)
    ik2_ref[...] = jnp.where(lane < IDX_DIM, ikr, pltpu.roll(ikr, IDX_DIM, 1)).astype(BF16)

    cb = _dot(h, w_ref[:, _CB0:_CC0])
    u = _dot(h, w_ref[:, _CC0:_CH0]) * _dot(h, w_ref[:, _CH0:_PROJ_W])
    row = lax.broadcasted_iota(I32, u.shape, 0)
    r1 = pltpu.roll(u, 1, 0)
    r2 = pltpu.roll(u, 2, 0)
    if halo:
        t = row & (SUBLANES - 1)
        us1 = jnp.where(t == 0, h1_ref[...], r1)
        us2 = jnp.where(t < 2, h2_ref[...], r2)
        u_ref[...] = u
    else:
        @pl.when(pl.program_id(0) % tiles_per_seq == 0)
        def _():
            carry_ref[...] = jnp.zeros_like(carry_ref)

        c0 = carry_ref[SUBLANES - 2:SUBLANES - 1, :]
        c1 = carry_ref[SUBLANES - 1:SUBLANES, :]
        us1 = jnp.where(row == 0, c1, r1)
        us2 = jnp.where(row == 0, c0, jnp.where(row == 1, c1, r2))
        tail = u[tm - SUBLANES:tm, :]
        carry_ref[...] = tail
        u_ref[0] = tail
    cw = cw_ref[...]
    y = cb * (cw[0:1, :] * us2 + cw[1:2, :] * us1 + cw[2:3, :] * u)
    cy_ref[...] = y.astype(BF16)


def _proj(x, g, w, cos, sin, cosk, sink, cw, tm, tiles_per_seq, halos=None):
    n = x.shape[0]
    n_tiles = n // tm
    halo = halos is not None
    tok = lambda width: pl.BlockSpec((tm, width), lambda i: (i, 0))
    tab = pl.BlockSpec((tm, LANES), lambda i: (i % tiles_per_seq, 0))
    full = lambda a: pl.BlockSpec(a.shape, lambda i: (0,) * a.ndim)
    in_specs = [tok(D_MODEL), full(g), full(w), tab, tab, tab, tab, full(cw)]
    args = [x, g, w, cos, sin, cosk, sink, cw]
    out_shape = [
        jax.ShapeDtypeStruct((n, ATTN_DIM), BF16),
        jax.ShapeDtypeStruct((n, N_IDX_HEADS * IDX_DIM), BF16),
        jax.ShapeDtypeStruct((n, KV_DIM), F32),
        jax.ShapeDtypeStruct((n, KV_DIM), F32),
        jax.ShapeDtypeStruct((n, LANES), F32),
        jax.ShapeDtypeStruct((n, KV_DIM), BF16),
        jax.ShapeDtypeStruct((n, KV_DIM), BF16),
        jax.ShapeDtypeStruct((n, LANES), BF16),
        jax.ShapeDtypeStruct((n, CONV_DIM), BF16),
    ]
    out_specs = [tok(ATTN_DIM), tok(N_IDX_HEADS * IDX_DIM), tok(KV_DIM), tok(KV_DIM), tok(LANES),
                 tok(KV_DIM), tok(KV_DIM), tok(LANES), tok(CONV_DIM)]
    scratch = []
    if halo:
        in_specs += [tok(CONV_DIM), tok(CONV_DIM)]
        args += list(halos)
        out_shape.append(jax.ShapeDtypeStruct((n, CONV_DIM), F32))
        out_specs.append(tok(CONV_DIM))
    else:
        n_seq = n_tiles // tiles_per_seq
        out_shape.append(jax.ShapeDtypeStruct((n_seq, SUBLANES, CONV_DIM), F32))
        out_specs.append(pl.BlockSpec((1, SUBLANES, CONV_DIM), lambda i: (i // tiles_per_seq, 0, 0)))
        scratch.append(pltpu.VMEM((SUBLANES, CONV_DIM), F32))
    return pl.pallas_call(
        functools.partial(_proj_body, tm=tm, tiles_per_seq=tiles_per_seq, halo=halo),
        grid=(n_tiles,),
        in_specs=in_specs,
        out_specs=out_specs,
        out_shape=out_shape,
        scratch_shapes=scratch,
        compiler_params=_cparams(("arbitrary",)),
        name="proj_sample" if halo else "proj_prompt",
    )(*args)


def _sort_key(s):
    bits = lax.bitcast_convert_type(s + 0.0, I32)
    return bits ^ ((bits >> 31) & 0x7FFFFFFF)


def _lane_fold(ind):
    acc = ind[:, 0:LANES]
    for j in range(1, ind.shape[-1] // LANES):
        acc = acc + ind[:, j * LANES:(j + 1) * LANES]
    return acc


def _select_threshold(count_fn, rows, topk):
    def step(b, t):
        cand = t + lax.shift_left(jnp.int32(1), 31 - b)
        cnt = count_fn(lambda kk: kk >= cand)
        return jnp.where(cnt >= topk, cand, t)

    return lax.fori_loop(0, 32, step, jnp.full((rows, 1), INT_MIN, I32))


def _tie_cutoff(count_fn, t, need, idx_bits):
    def step(b, c):
        cand = c + lax.shift_left(jnp.int32(1), idx_bits - 1 - b)
        cnt = count_fn(lambda kk, col: (kk == t) & (col < cand))
        return jnp.where(cnt <= need, cand, c)

    return lax.fori_loop(0, idx_bits, step, jnp.zeros(t.shape, I32))


def _attn_prompt_body(q_ref, iq_ref, w_ref, ik2_ref, kb_ref, vb_ref, o_ref,
                      keys_ref, qs_ref, iqs_ref, m_ref, l_ref, acc_ref, *, qb, kc, topk, idx_bits):
    i = pl.program_id(1)
    n_chunks = lax.shift_right_logical((i + 1) * qb + (kc - 1), jnp.int32(kc.bit_length() - 1))
    lo = lax.broadcasted_iota(I32, (qb, LANES), 1) < HEAD_DIM

    for c in range(N_HEADS // 2):
        qc = q_ref[:, c * LANES:(c + 1) * LANES]
        qs_ref[c * qb:(c + 1) * qb, :] = jnp.where(lo, qc, jnp.zeros_like(qc))
        qs_ref[(c + 4) * qb:(c + 5) * qb, :] = jnp.where(lo, jnp.zeros_like(qc), qc)
    for c in range(N_IDX_HEADS // 2):
        ic = iq_ref[:, c * LANES:(c + 1) * LANES]
        iqs_ref[(2 * c) * qb:(2 * c + 1) * qb, :] = jnp.where(lo, ic, jnp.zeros_like(ic))
        iqs_ref[(2 * c + 1) * qb:(2 * c + 2) * qb, :] = jnp.where(lo, jnp.zeros_like(ic), ic)

    wq = w_ref[...]
    wcol = [wq[:, IDX_DIM + h:IDX_DIM + h + 1] * (N_IDX_HEADS ** -0.5) for h in range(N_IDX_HEADS)]
    row_g = i * qb + lax.broadcasted_iota(I32, (qb, kc), 0)
    col_l = lax.broadcasted_iota(I32, (qb, kc), 1)

    def score_chunk(c, carry):
        off = pl.multiple_of(c * kc, kc)
        logits = _dot_t(iqs_ref[...], ik2_ref[pl.ds(off, kc), :])
        sc = wcol[0] * jnp.maximum(logits[0:qb], 0.0)
        for h in range(1, N_IDX_HEADS):
            sc = sc + wcol[h] * jnp.maximum(logits[h * qb:(h + 1) * qb], 0.0)
        keys_ref[c] = jnp.where(off + col_l <= row_g, _sort_key(sc), INT_MIN)
        return carry

    lax.fori_loop(0, n_chunks, score_chunk, 0)

    def count_keys(pred):
        def body(c, acc):
            return acc + _lane_fold(jnp.where(pred(keys_ref[c]), 1.0, 0.0))
        acc = lax.fori_loop(0, n_chunks, body, jnp.zeros((qb, LANES), F32))
        return jnp.sum(acc, axis=1, keepdims=True)

    def count_keys_cols(pred):
        def body(c, acc):
            return acc + _lane_fold(jnp.where(pred(keys_ref[c], c * kc + col_l), 1.0, 0.0))
        acc = lax.fori_loop(0, n_chunks, body, jnp.zeros((qb, LANES), F32))
        return jnp.sum(acc, axis=1, keepdims=True)

    thr = _select_threshold(count_keys, qb, float(topk))
    thr = jnp.maximum(thr, INT_MIN + 1)
    over = count_keys(lambda kk: kk >= thr) > float(topk)

    @pl.when(jnp.max(jnp.where(over, 1.0, 0.0)) > 0.0)
    def _():
        need = float(topk) - count_keys(lambda kk: kk > thr)
        cut = _tie_cutoff(count_keys_cols, thr, need, idx_bits)
        cut = jnp.where(over, cut, jnp.int32(1 << idx_bits))

        def drop(c, carry):
            kk = keys_ref[c]
            keys_ref[c] = jnp.where((kk == thr) & (c * kc + col_l >= cut), INT_MIN, kk)
            return carry

        lax.fori_loop(0, n_chunks, drop, 0)

    m_ref[...] = jnp.full(m_ref.shape, NEG_BIG, F32)
    l_ref[...] = jnp.zeros(l_ref.shape, F32)
    acc_ref[...] = jnp.zeros(acc_ref.shape, F32)

    def attn_chunk(c, carry):
        off = pl.multiple_of(c * kc, kc)
        s = _dot_t(qs_ref[...], kb_ref[pl.ds(off, kc), :])
        sel = keys_ref[c] >= thr
        s = jnp.where(sel[None], s.reshape(N_HEADS, qb, kc), NEG_BIG).reshape(N_HEADS * qb, kc)
        m_old = m_ref[...]
        m_new = jnp.maximum(m_old, jnp.max(s, axis=1, keepdims=True))
        alpha = jnp.exp(m_old - m_new)
        p = jnp.exp(s - m_new)
        l_ref[...] = alpha * l_ref[...] + jnp.sum(p, axis=1, keepdims=True)
        acc_ref[...] = alpha * acc_ref[...] + _dot(p.astype(BF16), vb_ref[pl.ds(off, kc), :])
        m_ref[...] = m_new
        return carry

    lax.fori_loop(0, n_chunks, attn_chunk, 0)

    out = acc_ref[...] / l_ref[...]
    for c in range(N_HEADS // 2):
        oc = jnp.where(lo, out[c * qb:(c + 1) * qb], out[(c + 4) * qb:(c + 5) * qb])
        o_ref[:, c * LANES:(c + 1) * LANES] = oc.astype(BF16)


def _attn_prompt(q, iq, ikw, ik2, kb, vb, batch, seq, qb, kc):
    nqb = seq // qb
    topk = min(TOPK_MAX, seq // 4)
    tokq = lambda width: pl.BlockSpec((qb, width), lambda b, i: (b * nqb + i, 0))
    seqk = pl.BlockSpec((None, seq, LANES), lambda b, i: (b, 0, 0))
    as_seq = lambda a: a.reshape(batch, seq, LANES)
    return pl.pallas_call(
        functools.partial(_attn_prompt_body, qb=qb, kc=kc, topk=topk, idx_bits=(seq - 1).bit_length()),
        grid=(batch, nqb),
        in_specs=[tokq(ATTN_DIM), tokq(N_IDX_HEADS * IDX_DIM), tokq(LANES), seqk, seqk, seqk],
        out_specs=tokq(ATTN_DIM),
        out_shape=jax.ShapeDtypeStruct((batch * seq, ATTN_DIM), BF16),
        scratch_shapes=[
            pltpu.VMEM((seq // kc, qb, kc), I32),
            pltpu.VMEM((N_HEADS * qb, LANES), BF16),
            pltpu.VMEM((N_IDX_HEADS * qb, LANES), BF16),
            pltpu.VMEM((N_HEADS * qb, 1), F32),
            pltpu.VMEM((N_HEADS * qb, 1), F32),
            pltpu.VMEM((N_HEADS * qb, LANES), F32),
        ],
        compiler_params=_cparams(("arbitrary", "arbitrary")),
        name="attn_prompt",
    )(q, iq, ikw, as_seq(ik2), as_seq(kb), as_seq(vb))


def _stack_heads(x_bf, n_heads):
    x = x_bf.astype(F32)
    return jnp.concatenate([x[:, h * HEAD_DIM:(h + 1) * HEAD_DIM] for h in range(n_heads)], axis=0).astype(BF16)


def _weighted_relu_sum(logits, wcol, t):
    sc = wcol[0] * jnp.maximum(logits[0:t], 0.0)
    for h in range(1, N_IDX_HEADS):
        sc = sc + wcol[h] * jnp.maximum(logits[h * t:(h + 1) * t], 0.0)
    return sc


def _head_weights(w_ref):
    wq = w_ref[...]
    return [wq[:, IDX_DIM + h:IDX_DIM + h + 1] * (N_IDX_HEADS ** -0.5) for h in range(N_IDX_HEADS)]


def _sample_scores_body(pt_ref, iq_ref, w_ref, *refs, pg, t):
    del pt_ref
    pages = refs[:pg]
    keys_ref, nkeys_ref, ikt_ref = refs[pg:]
    iqs = _stack_heads(iq_ref[...], N_IDX_HEADS)
    wcol = _head_weights(w_ref)
    for j in range(pg):
        ikt_ref[:, j * PAGE_SIZE:(j + 1) * PAGE_SIZE] = pages[j][...].astype(BF16)
    logits = _dot(iqs, ikt_ref[...])
    keys_ref[...] = _sort_key(_weighted_relu_sum(logits, wcol, t))

    @pl.when(pl.program_id(1) == 0)
    def _():
        ikn = jnp.concatenate([w_ref[...][:, 0:IDX_DIM], jnp.zeros((LANES - t, IDX_DIM), F32)], axis=0)
        sc = _weighted_relu_sum(_dot_t(iqs, ikn.astype(BF16)), wcol, t)
        col_n = lax.broadcasted_iota(I32, (t, LANES), 1)
        row_n = lax.broadcasted_iota(I32, (t, LANES), 0)
        nkeys_ref[...] = jnp.where(col_n <= row_n, _sort_key(sc), INT_MIN)


def _sample_scores(page_table, iq, ikw, cache_idx_t, db, t, pg):
    n_pages = page_table.shape[1]
    page_spec = lambda j: pl.BlockSpec((None, IDX_DIM, PAGE_SIZE),
                                       lambda b, g, pt: (pt[b, g * pg + j], 0, 0))
    grid_spec = pltpu.PrefetchScalarGridSpec(
        num_scalar_prefetch=1,
        grid=(db, n_pages // pg),
        in_specs=[pl.BlockSpec((t, N_IDX_HEADS * IDX_DIM), lambda b, g, pt: (b, 0)),
                  pl.BlockSpec((t, LANES), lambda b, g, pt: (b, 0))] + [page_spec(j) for j in range(pg)],
        out_specs=[pl.BlockSpec((t, pg * PAGE_SIZE), lambda b, g, pt: (b, g)),
                   pl.BlockSpec((t, LANES), lambda b, g, pt: (b, 0))],
        scratch_shapes=[pltpu.VMEM((IDX_DIM, pg * PAGE_SIZE), BF16)],
    )
    return pl.pallas_call(
        functools.partial(_sample_scores_body, pg=pg, t=t),
        grid_spec=grid_spec,
        out_shape=[jax.ShapeDtypeStruct((db * t, n_pages * PAGE_SIZE), I32),
                   jax.ShapeDtypeStruct((db * t, LANES), I32)],
        compiler_params=_cparams(("arbitrary", "arbitrary")),
        name="sample_scores",
    )(page_table, iq, ikw, *([cache_idx_t] * pg))


def _sample_thr_body(keys_ref, nkeys_ref, thr_ref, cut_ref, *, rows, topk, idx_bits, cw):
    past = keys_ref.shape[-1]
    col_c = lax.broadcasted_iota(I32, (rows, cw), 1)
    col_n = lax.broadcasted_iota(I32, (rows, LANES), 1)

    def count_keys(pred):
        acc = jnp.where(pred(nkeys_ref[...]), 1.0, 0.0)
        for c in range(past // cw):
            acc = acc + _lane_fold(jnp.where(pred(keys_ref[:, c * cw:(c + 1) * cw]), 1.0, 0.0))
        return jnp.sum(acc, axis=1, keepdims=True)

    def count_keys_cols(pred):
        acc = jnp.where(pred(nkeys_ref[...], past + col_n), 1.0, 0.0)
        for c in range(past // cw):
            acc = acc + _lane_fold(jnp.where(pred(keys_ref[:, c * cw:(c + 1) * cw], c * cw + col_c), 1.0, 0.0))
        return jnp.sum(acc, axis=1, keepdims=True)

    thr = _select_threshold(count_keys, rows, float(topk))
    thr = jnp.maximum(thr, INT_MIN + 1)
    over = count_keys(lambda kk: kk >= thr) > float(topk)
    thr_ref[...] = jnp.broadcast_to(thr, (rows, LANES))
    cut_ref[...] = jnp.full((rows, LANES), 1 << idx_bits, I32)

    @pl.when(jnp.max(jnp.where(over, 1.0, 0.0)) > 0.0)
    def _():
        need = float(topk) - count_keys(lambda kk: kk > thr)
        cut = _tie_cutoff(count_keys_cols, thr, need, idx_bits)
        cut_ref[...] = jnp.broadcast_to(jnp.where(over, cut, jnp.int32(1 << idx_bits)), (rows, LANES))


def _sample_thr(keys, nkeys, t, rows):
    n, past = keys.shape
    topk = min(TOPK_MAX, (past + t) // 4)
    blk = lambda width: pl.BlockSpec((rows, width), lambda i: (i, 0))
    return pl.pallas_call(
        functools.partial(_sample_thr_body, rows=rows, topk=topk,
                          idx_bits=(past + LANES - 1).bit_length(), cw=2048),
        grid=(n // rows,),
        in_specs=[blk(past), blk(LANES)],
        out_specs=[blk(LANES), blk(LANES)],
        out_shape=[jax.ShapeDtypeStruct((n, LANES), I32), jax.ShapeDtypeStruct((n, LANES), I32)],
        compiler_params=_cparams(("arbitrary",)),
        name="sample_thr",
    )(keys, nkeys)


def _sample_attn_body(pt_ref, keys_ref, nkeys_ref, thr_ref, cut_ref, q_ref, kn_ref, vn_ref, *refs,
                      pg, t, past):
    del pt_ref
    kpages, vpages = refs[:pg], refs[pg:2 * pg]
    o_ref = refs[2 * pg]
    qs_ref, kt_ref, vt_ref, m_ref, l_ref, acc_ref = refs[2 * pg + 1:]
    g = pl.program_id(1)
    rows = N_HEADS * t

    def flash_update(s, sel, pv):
        n = s.shape[-1]
        s = jnp.where(sel[None], s.reshape(N_HEADS, t, n), NEG_BIG).reshape(rows, n)
        m_old = m_ref[...]
        m_new = jnp.maximum(m_old, jnp.max(s, axis=1, keepdims=True))
        alpha = jnp.exp(m_old - m_new)
        p = jnp.exp(s - m_new)
        l_ref[...] = alpha * l_ref[...] + jnp.sum(p, axis=1, keepdims=True)
        acc_ref[...] = alpha * acc_ref[...] + pv(p.astype(BF16))
        m_ref[...] = m_new

    @pl.when(g == 0)
    def _():
        lo = lax.broadcasted_iota(I32, (t, LANES), 1) < HEAD_DIM
        for c in range(N_HEADS // 2):
            qc = q_ref[:, c * LANES:(c + 1) * LANES]
            qs_ref[c * t:(c + 1) * t, :] = jnp.where(lo, qc, jnp.zeros_like(qc))
            qs_ref[(c + 4) * t:(c + 5) * t, :] = jnp.where(lo, jnp.zeros_like(qc), qc)
        m_ref[...] = jnp.full(m_ref.shape, NEG_BIG, F32)
        l_ref[...] = jnp.zeros(l_ref.shape, F32)
        acc_ref[...] = jnp.zeros(acc_ref.shape, F32)

    thr = thr_ref[:, 0:1]
    cut = cut_ref[:, 0:1]

    def selected(kk, col):
        return (kk > thr) | ((kk == thr) & (col < cut))

    qs = qs_ref[...]
    width = pg * PAGE_SIZE
    for j in range(pg):
        kt_ref[:, j * PAGE_SIZE:(j + 1) * PAGE_SIZE] = kpages[j][...].astype(BF16)
        vt_ref[:, j * PAGE_SIZE:(j + 1) * PAGE_SIZE] = vpages[j][...].astype(BF16)
    col_w = g * width + lax.broadcasted_iota(I32, (t, width), 1)
    flash_update(_dot(qs, kt_ref[...]), selected(keys_ref[...], col_w),
                 lambda p: _dot_t(p, vt_ref[...]))

    @pl.when(g == pl.num_programs(1) - 1)
    def _():
        pad = jnp.zeros((LANES - t, LANES), F32)
        kn = jnp.concatenate([kn_ref[...], pad], axis=0).astype(BF16)
        vn = jnp.concatenate([vn_ref[...], pad], axis=0).astype(BF16)
        col_n = past + lax.broadcasted_iota(I32, (t, LANES), 1)
        flash_update(_dot_t(qs, kn), selected(nkeys_ref[...], col_n), lambda p: _dot(p, vn))
        lo = lax.broadcasted_iota(I32, (t, LANES), 1) < HEAD_DIM
        out = acc_ref[...] / l_ref[...]
        for c in range(N_HEADS // 2):
            oc = jnp.where(lo, out[c * t:(c + 1) * t], out[(c + 4) * t:(c + 5) * t])
            o_ref[:, c * LANES:(c + 1) * LANES] = oc.astype(BF16)


def _sample_attn(page_table, keys, nkeys, thr, cut, q, k_new, v_new, cache_kt, cache_vt, db, t, pg):
    n_pages = page_table.shape[1]
    past = n_pages * PAGE_SIZE
    page_spec = lambda j: pl.BlockSpec((None, KV_DIM, PAGE_SIZE),
                                       lambda b, g, pt: (pt[b, g * pg + j], 0, 0))
    tok = lambda width: pl.BlockSpec((t, width), lambda b, g, pt: (b, 0))
    grid_spec = pltpu.PrefetchScalarGridSpec(
        num_scalar_prefetch=1,
        grid=(db, n_pages // pg),
        in_specs=[pl.BlockSpec((t, pg * PAGE_SIZE), lambda b, g, pt: (b, g)),
                  tok(LANES), tok(LANES), tok(LANES), tok(ATTN_DIM), tok(KV_DIM), tok(KV_DIM)]
                 + [page_spec(j) for j in range(pg)] + [page_spec(j) for j in range(pg)],
        out_specs=tok(ATTN_DIM),
        scratch_shapes=[
            pltpu.VMEM((N_HEADS * t, LANES), BF16),
            pltpu.VMEM((KV_DIM, pg * PAGE_SIZE), BF16),
            pltpu.VMEM((KV_DIM, pg * PAGE_SIZE), BF16),
            pltpu.VMEM((N_HEADS * t, 1), F32),
            pltpu.VMEM((N_HEADS * t, 1), F32),
            pltpu.VMEM((N_HEADS * t, LANES), F32),
        ],
    )
    return pl.pallas_call(
        functools.partial(_sample_attn_body, pg=pg, t=t, past=past),
        grid_spec=grid_spec,
        out_shape=jax.ShapeDtypeStruct((db * t, ATTN_DIM), BF16),
        compiler_params=_cparams(("arbitrary", "arbitrary")),
        name="sample_attn",
    )(page_table, keys, nkeys, thr, cut, q, k_new, v_new, *([cache_kt] * pg), *([cache_vt] * pg))


def _out_body(x_ref, a_ref, c_ref, wa_ref, wc_ref, o_ref):
    o_ref[...] = x_ref[...] + _dot(a_ref[...], wa_ref[...]) + _dot(c_ref[...], wc_ref[...])


def _out_proj(x, attn, conv, wa, wc, tm):
    n = x.shape[0]
    tok = lambda width: pl.BlockSpec((tm, width), lambda i: (i, 0))
    full = lambda a: pl.BlockSpec(a.shape, lambda i: (0, 0))
    return pl.pallas_call(
        _out_body,
        grid=(n // tm,),
        in_specs=[tok(D_MODEL), tok(ATTN_DIM), tok(CONV_DIM), full(wa), full(wc)],
        out_specs=tok(D_MODEL),
        out_shape=jax.ShapeDtypeStruct((n, D_MODEL), F32),
        compiler_params=_cparams(("arbitrary",)),
        name="out_proj",
    )(x, attn, conv, wa, wc)


def _ple_body(x_ref, pe_ref, g_ref, wg_ref, wp_ref, gf_ref, o_ref):
    x = x_ref[...]
    gate = jax.nn.sigmoid(_dot(_rms(x, g_ref[...]).astype(BF16), wg_ref[...]))
    x = x + gate * _dot(pe_ref[...].astype(BF16), wp_ref[...])
    o_ref[...] = _rms(x, gf_ref[...])


def _ple_final(x, pe, g, wg, wp, gf, tm):
    n = x.shape[0]
    tok = lambda width: pl.BlockSpec((tm, width), lambda i: (i, 0))
    full = lambda a: pl.BlockSpec(a.shape, lambda i: (0, 0))
    return pl.pallas_call(
        _ple_body,
        grid=(n // tm,),
        in_specs=[tok(D_MODEL), tok(PLE_DIM), full(g), full(wg), full(wp), full(gf)],
        out_specs=tok(D_MODEL),
        out_shape=jax.ShapeDtypeStruct((n, D_MODEL), F32),
        compiler_params=_cparams(("arbitrary",)),
        name="ple_final",
    )(x, pe, g, wg, wp, gf)


def _rope_tables(pos):
    half = HEAD_DIM // 2
    inv = ROPE_THETA ** (-jnp.arange(half, dtype=F32) / half)
    ang = pos.astype(F32)[:, None] * inv[None, :]
    cos, sin = jnp.cos(ang), jnp.sin(ang)
    cos2 = jnp.concatenate([cos, cos], axis=1)
    sin2 = jnp.concatenate([-sin, sin], axis=1)
    cos128 = jnp.concatenate([cos2, cos2], axis=1)
    sin128 = jnp.concatenate([sin2, sin2], axis=1)
    cosk = jnp.concatenate([cos2, jnp.ones_like(cos2)], axis=1)
    sink = jnp.concatenate([sin2, jnp.zeros_like(sin2)], axis=1)
    return cos128, sin128, cosk, sink


def _prep_w_in(w_in):
    d = w_in.shape[0]
    q = w_in[:, :ATTN_DIM].reshape(d, N_HEADS, HEAD_DIM)[:, jnp.array(_HEAD_PERM)].reshape(d, ATTN_DIM)
    o = ATTN_DIM
    kv = w_in[:, o:o + 2 * KV_DIM]
    o += 2 * KV_DIM
    iq = w_in[:, o:o + N_IDX_HEADS * IDX_DIM]
    o += N_IDX_HEADS * IDX_DIM
    ikw = w_in[:, o:o + IDX_DIM + N_IDX_HEADS]
    o += IDX_DIM + N_IDX_HEADS
    ikw = jnp.pad(ikw, ((0, 0), (0, LANES - IDX_DIM - N_IDX_HEADS)))
    conv = w_in[:, o:]
    return jnp.concatenate([q, kv, iq, ikw, conv], axis=1).astype(BF16)


def _layer(x, pe, mixer, tm, g_ffn1, w1g, w1u, w1d, g_mix, w_in, conv_w, wo_a, wo_c,
           g_ffn2, w2g, w2u, w2d, g_ple, w_pg, w_pp, g_final):
    x1 = _ffn(x, g_ffn1, w1g, w1u, w1d, tm)
    attn, conv, state = mixer(x1, g_mix, w_in, conv_w)
    x2 = _out_proj(x1, attn, conv, wo_a, wo_c, tm)
    x3 = _ffn(x2, g_ffn2, w2g, w2u, w2d, tm)
    return _ple_final(x3, pe, g_ple, w_pg, w_pp, g_final, tm), state


def kernel(x_prompt, x_sample, cache_k, cache_v, cache_idx_k, state_conv, page_table, p_prompt, p_sample,
           g_ffn1, w1_gate, w1_up, w1_down, g_mix, w_in, conv_w, w_out,
           g_ffn2, w2_gate, w2_up, w2_down, g_ple, w_ple_gate, w_ple_proj, g_final):
    depth = w_in.shape[0]
    assert depth == 1, "single-layer step"
    batch, seq, _ = x_prompt.shape
    db, t, _ = x_sample.shape
    n_pages = page_table.shape[1]
    past = n_pages * PAGE_SIZE
    assert t == SUBLANES and seq % 512 == 0

    row = lambda gvec: gvec.reshape(1, -1)
    perm = jnp.array(_HEAD_PERM)
    wo_a = w_out[0, :ATTN_DIM].reshape(N_HEADS, HEAD_DIM, D_MODEL)[perm].reshape(ATTN_DIM, D_MODEL).astype(BF16)
    wo_c = w_out[0, ATTN_DIM:].astype(BF16)
    shared = (row(g_ffn1[0]), w1_gate[0].astype(BF16), w1_up[0].astype(BF16), w1_down[0].astype(BF16),
              row(g_mix[0]), _prep_w_in(w_in[0]), conv_w[0], wo_a, wo_c,
              row(g_ffn2[0]), w2_gate[0].astype(BF16), w2_up[0].astype(BF16), w2_down[0].astype(BF16),
              row(g_ple[0]), w_ple_gate[0].astype(BF16), w_ple_proj[0].astype(BF16), row(g_final))

    tm_p = 512
    tabs_p = _rope_tables(jnp.arange(seq, dtype=I32))

    def prompt_mixer(x1, g, w, cw):
        q, iq, k, v, ikw, kb, vb, ik2, cy, tail = _proj(x1, g, w, *tabs_p, cw, tm_p, seq // tm_p)
        attn = _attn_prompt(q, iq, ikw, ik2, kb, vb, batch, seq, qb=128, kc=512)
        return attn, cy, (k, v, ikw, tail)

    yp, (kp, vp, ikwp, tailp) = _layer(x_prompt.reshape(batch * seq, D_MODEL),
                                       p_prompt[0].reshape(batch * seq, PLE_DIM), prompt_mixer, tm_p, *shared)

    n_s = db * t
    tabs_s = tuple(jnp.tile(a, (db, 1)) for a in _rope_tables(past + jnp.arange(t, dtype=I32)))
    buf = state_conv[0]
    zero = jnp.zeros((db, t - 2, CONV_DIM), F32)
    halo1 = jnp.concatenate([buf[:, 1:2], jnp.zeros((db, t - 1, CONV_DIM), F32)], axis=1).reshape(n_s, CONV_DIM)
    halo2 = jnp.concatenate([buf, zero], axis=1).reshape(n_s, CONV_DIM)
    idx_t = jnp.transpose(cache_idx_k[0], (0, 2, 1))
    k_t = jnp.transpose(cache_k[0], (0, 2, 3, 1)).reshape(-1, KV_DIM, PAGE_SIZE)
    v_t = jnp.transpose(cache_v[0], (0, 2, 3, 1)).reshape(-1, KV_DIM, PAGE_SIZE)

    def sample_mixer(x1, g, w, cw):
        q, iq, k, v, ikw, kb, vb, ik2, cy, u = _proj(x1, g, w, *tabs_s, cw, n_s, 1, halos=(halo1, halo2))
        keys, nkeys = _sample_scores(page_table, iq, ikw, idx_t, db, t, pg=16)
        thr, cut = _sample_thr(keys, nkeys, t, rows=min(64, n_s))
        attn = _sample_attn(page_table, keys, nkeys, thr, cut, q, k, v, k_t, v_t, db, t, pg=16)
        return attn, cy, (k, v, ikw, u)

    ys, (ks, vs, ikws, us) = _layer(x_sample.reshape(n_s, D_MODEL), p_sample[0].reshape(n_s, PLE_DIM),
                                    sample_mixer, n_s, *shared)

    return (yp.reshape(batch, seq, D_MODEL),
            ys.reshape(db, t, D_MODEL),
            kp.reshape(1, batch, seq, N_KV_HEADS, HEAD_DIM),
            vp.reshape(1, batch, seq, N_KV_HEADS, HEAD_DIM),
            ikwp[:, :IDX_DIM].reshape(1, batch, seq, IDX_DIM),
            tailp[:, SUBLANES - (CONV_WIDTH - 1):][None],
            ks.reshape(1, db, t, N_KV_HEADS, HEAD_DIM),
            vs.reshape(1, db, t, N_KV_HEADS, HEAD_DIM),
            ikws[:, :IDX_DIM].reshape(1, db, t, IDX_DIM),
            us.reshape(db, t, CONV_DIM)[:, t - (CONV_WIDTH - 1):][None])
```

```python
import functools

import jax
import jax.numpy as jnp
from jax import lax
from jax.experimental import pallas as pl
from jax.experimental.pallas import tpu as pltpu

F32 = jnp.float32
BF16 = jnp.bfloat16
I32 = jnp.int32

D_MODEL = 1024
N_HEADS = 8
HEAD_DIM = 64
N_KV_HEADS = 2
ATTN_DIM = N_HEADS * HEAD_DIM
KV_DIM = N_KV_HEADS * HEAD_DIM
N_IDX_HEADS = 4
IDX_DIM = 64
TOPK_MAX = 256
CONV_DIM = D_MODEL - ATTN_DIM
CONV_WIDTH = 3
D_FF = 2816
PLE_DIM = 256
PAGE_SIZE = 128
ROPE_THETA = 10000.0
RMS_EPS = 1e-6

LANES = 128
SUBLANES = 8
VMEM_LIMIT = 56 * 1024 * 1024

I16 = jnp.int16
INT_MIN = -(2 ** 31)
I16_MIN = -(2 ** 15)
NEG_BIG = -1e30
LOG2E = 1.4426950408889634

_Q0, _K0, _V0, _IQ0, _IK0, _CB0, _CC0, _CH0, _PROJ_W = 0, 512, 640, 768, 1024, 1152, 1664, 2176, 2688
_HEAD_PERM = (0, 4, 1, 5, 2, 6, 3, 7)


def _cparams(sem):
    return pltpu.CompilerParams(dimension_semantics=sem, vmem_limit_bytes=VMEM_LIMIT)


def _dot(a, b):
    return jnp.dot(a, b, preferred_element_type=F32)


def _dot_t(a, b):
    return lax.dot_general(a, b, (((1,), (1,)), ((), ())), preferred_element_type=F32)


def _rms(x, g):
    ms = jnp.mean(x * x, axis=-1, keepdims=True)
    return x * lax.rsqrt(ms + RMS_EPS) * g


def _ffn_body(x_ref, g_ref, wg_ref, wu_ref, wd_ref, o_ref, h_ref, acc_ref):
    j = pl.program_id(1)

    @pl.when(j == 0)
    def _():
        h_ref[...] = _rms(x_ref[...], g_ref[...]).astype(BF16)

    h = h_ref[...]
    gate = _dot(h, wg_ref[...])
    up = _dot(h, wu_ref[...])
    act = (gate * jax.nn.sigmoid(gate) * up).astype(BF16)
    part = _dot(act, wd_ref[...])

    @pl.when(j == 0)
    def _():
        acc_ref[...] = part

    @pl.when(j > 0)
    def _():
        acc_ref[...] += part

    @pl.when(j == pl.num_programs(1) - 1)
    def _():
        o_ref[...] = x_ref[...] + 0.5 * acc_ref[...]


def _ffn(x, g, wg, wu, wd, tm):
    n = x.shape[0]
    ff_chunk = D_FF // 2
    return pl.pallas_call(
        _ffn_body,
        grid=(n // tm, D_FF // ff_chunk),
        in_specs=[
            pl.BlockSpec((tm, D_MODEL), lambda i, j: (i, 0)),
            pl.BlockSpec((1, D_MODEL), lambda i, j: (0, 0)),
            pl.BlockSpec((D_MODEL, ff_chunk), lambda i, j: (0, j)),
            pl.BlockSpec((D_MODEL, ff_chunk), lambda i, j: (0, j)),
            pl.BlockSpec((ff_chunk, D_MODEL), lambda i, j: (j, 0)),
        ],
        out_specs=pl.BlockSpec((tm, D_MODEL), lambda i, j: (i, 0)),
        out_shape=jax.ShapeDtypeStruct((n, D_MODEL), F32),
        scratch_shapes=[pltpu.VMEM((tm, D_MODEL), BF16), pltpu.VMEM((tm, D_MODEL), F32)],
        compiler_params=_cparams(("arbitrary", "arbitrary")),
        name="ffn",
    )(x, g, wg, wu, wd)


def _swap32(x):
    w = x.shape[-1]
    lane = lax.broadcasted_iota(I32, x.shape, 1)
    return jnp.where((lane & 63) < 32, pltpu.roll(x, w - 32, 1), pltpu.roll(x, 32, 1))


def _rope_cols(z, cos, sin):
    cols = []
    for c in range(z.shape[-1] // LANES):
        zc = z[:, c * LANES:(c + 1) * LANES]
        cols.append(zc * cos + _swap32(zc) * sin)
    return cols[0] if len(cols) == 1 else jnp.concatenate(cols, axis=1)


def _proj_body(*refs, tm, tiles_per_seq, halo):
    if halo:
        (x_ref, g_ref, w_ref, cos_ref, sin_ref, cosk_ref, sink_ref, cw_ref, h1_ref, h2_ref,
         q_ref, iq_ref, k_ref, v_ref, ikw_ref, kb_ref, vb_ref, ik2_ref, cy_ref, u_ref) = refs
    else:
        (x_ref, g_ref, w_ref, cos_ref, sin_ref, cosk_ref, sink_ref, cw_ref,
         q_ref, iq_ref, k_ref, v_ref, ikw_ref, kb_ref, vb_ref, ik2_ref, cy_ref, u_ref, carry_ref) = refs

    h = _rms(x_ref[...], g_ref[...]).astype(BF16)
    cos, sin = cos_ref[...], sin_ref[...]

    zq = _dot(h, w_ref[:, _Q0:_K0])
    q_ref[...] = (_rope_cols(zq, cos, sin) * (HEAD_DIM ** -0.5 * LOG2E)).astype(BF16)
    kr = _rope_cols(_dot(h, w_ref[:, _K0:_V0]), cos, sin)
    k_ref[...] = kr
    kb_ref[...] = kr.astype(BF16)
    zv = _dot(h, w_ref[:, _V0:_IQ0])
    v_ref[...] = zv
    vb_ref[...] = jnp.concatenate([zv, jnp.ones_like(zv)], axis=1).astype(BF16)
    ziq = _dot(h, w_ref[:, _IQ0:_IK0])
    iq_ref[...] = (_rope_cols(ziq, cos, sin) * (IDX_DIM ** -0.5)).astype(BF16)
    ikr = _rope_cols(_dot(h, w_ref[:, _IK0:_CB0]), cosk_ref[...], sink_ref[...])
    ikw_ref[...] = ikr
    lane = lax.broadcasted_iota(I32, ikr.shape, 1)
    ik2_ref[...] = jnp.where(lane < IDX_DIM, ikr, pltpu.roll(ikr, IDX_DIM, 1)).astype(BF16)

    cb = _dot(h, w_ref[:, _CB0:_CC0])
    u = _dot(h, w_ref[:, _CC0:_CH0]) * _dot(h, w_ref[:, _CH0:_PROJ_W])
    row = lax.broadcasted_iota(I32, u.shape, 0)
    r1 = pltpu.roll(u, 1, 0)
    r2 = pltpu.roll(u, 2, 0)
    if halo:
        t = row & (SUBLANES - 1)
        us1 = jnp.where(t == 0, h1_ref[...], r1)
        us2 = jnp.where(t < 2, h2_ref[...], r2)
        u_ref[...] = u
    else:
        @pl.when(pl.program_id(0) % tiles_per_seq == 0)
        def _():
            carry_ref[...] = jnp.zeros_like(carry_ref)

        c0 = carry_ref[SUBLANES - 2:SUBLANES - 1, :]
        c1 = carry_ref[SUBLANES - 1:SUBLANES, :]
        us1 = jnp.where(row == 0, c1, r1)
        us2 = jnp.where(row == 0, c0, jnp.where(row == 1, c1, r2))
        tail = u[tm - SUBLANES:tm, :]
        carry_ref[...] = tail
        u_ref[0] = tail
    cw = cw_ref[...]
    y = cb * (cw[0:1, :] * us2 + cw[1:2, :] * us1 + cw[2:3, :] * u)
    cy_ref[...] = y.astype(BF16)


def _proj(x, g, w, cos, sin, cosk, sink, cw, tm, tiles_per_seq, halos=None):
    n = x.shape[0]
    n_tiles = n // tm
    halo = halos is not None
    tok = lambda width: pl.BlockSpec((tm, width), lambda i: (i, 0))
    tab = pl.BlockSpec((tm, LANES), lambda i: (i % tiles_per_seq, 0))
    full = lambda a: pl.BlockSpec(a.shape, lambda i: (0,) * a.ndim)
    in_specs = [tok(D_MODEL), full(g), full(w), tab, tab, tab, tab, full(cw)]
    args = [x, g, w, cos, sin, cosk, sink, cw]
    out_shape = [
        jax.ShapeDtypeStruct((n, ATTN_DIM), BF16),
        jax.ShapeDtypeStruct((n, N_IDX_HEADS * IDX_DIM), BF16),
        jax.ShapeDtypeStruct((n, KV_DIM), F32),
        jax.ShapeDtypeStruct((n, KV_DIM), F32),
        jax.ShapeDtypeStruct((n, LANES), F32),
        jax.ShapeDtypeStruct((n, KV_DIM), BF16),
        jax.ShapeDtypeStruct((n, 2 * KV_DIM), BF16),
        jax.ShapeDtypeStruct((n, LANES), BF16),
        jax.ShapeDtypeStruct((n, CONV_DIM), BF16),
    ]
    out_specs = [tok(ATTN_DIM), tok(N_IDX_HEADS * IDX_DIM), tok(KV_DIM), tok(KV_DIM), tok(LANES),
                 tok(KV_DIM), tok(2 * KV_DIM), tok(LANES), tok(CONV_DIM)]
    scratch = []
    if halo:
        in_specs += [tok(CONV_DIM), tok(CONV_DIM)]
        args += list(halos)
        out_shape.append(jax.ShapeDtypeStruct((n, CONV_DIM), F32))
        out_specs.append(tok(CONV_DIM))
    else:
        n_seq = n_tiles // tiles_per_seq
        out_shape.append(jax.ShapeDtypeStruct((n_seq, SUBLANES, CONV_DIM), F32))
        out_specs.append(pl.BlockSpec((1, SUBLANES, CONV_DIM), lambda i: (i // tiles_per_seq, 0, 0)))
        scratch.append(pltpu.VMEM((SUBLANES, CONV_DIM), F32))
    return pl.pallas_call(
        functools.partial(_proj_body, tm=tm, tiles_per_seq=tiles_per_seq, halo=halo),
        grid=(n_tiles,),
        in_specs=in_specs,
        out_specs=out_specs,
        out_shape=out_shape,
        scratch_shapes=scratch,
        compiler_params=_cparams(("arbitrary",)),
        name="proj_sample" if halo else "proj_prompt",
    )(*args)


def _sort_key(s):
    bits = lax.bitcast_convert_type(s + 0.0, I32)
    return bits ^ ((bits >> 31) & 0x7FFFFFFF)


def _lane_fold(ind):
    acc = ind[:, 0:LANES]
    for j in range(1, ind.shape[-1] // LANES):
        acc = acc + ind[:, j * LANES:(j + 1) * LANES]
    return acc


def _select_threshold(count_fn, rows, topk):
    def step(b, t):
        cand = t + lax.shift_left(jnp.int32(1), 31 - b)
        cnt = count_fn(lambda kk: kk >= cand)
        return jnp.where(cnt >= topk, cand, t)

    return lax.fori_loop(0, 32, step, jnp.full((rows, 1), INT_MIN, I32))


def _select_threshold16(count_fn, rows, need):
    def step(b, t):
        cand = t + lax.shift_left(jnp.int32(1), 15 - b)
        return jnp.where(count_fn(cand) >= need, cand, t)

    return lax.fori_loop(0, 16, step, jnp.full((rows, 1), I16_MIN, I32))


def _tie_cutoff(count_fn, t, need, idx_bits):
    def step(b, c):
        cand = c + lax.shift_left(jnp.int32(1), idx_bits - 1 - b)
        cnt = count_fn(lambda kk, col: (kk == t) & (col < cand))
        return jnp.where(cnt <= need, cand, c)

    return lax.fori_loop(0, idx_bits, step, jnp.zeros(t.shape, I32))


def _attn_prompt_body(q_ref, iq_ref, w_ref, ik2_ref, kb_ref, vb_ref, o_ref,
                      keys_ref, hi_ref, lo_ref, qs_ref, iqs_ref, s0_ref, s1_ref, p_ref, bias_ref,
                      m_ref, a_ref, acc_ref,
                      *, qb, kc, topk, idx_bits):
    i = pl.program_id(1)
    n_chunks = lax.shift_right_logical((i + 1) * qb + (kc - 1), jnp.int32(kc.bit_length() - 1))
    lo = lax.broadcasted_iota(I32, (qb, LANES), 1) < HEAD_DIM

    for c in range(N_HEADS // 2):
        qc = q_ref[:, c * LANES:(c + 1) * LANES]
        qs_ref[c * qb:(c + 1) * qb, :] = jnp.where(lo, qc, jnp.zeros_like(qc))
        qs_ref[(c + 4) * qb:(c + 5) * qb, :] = jnp.where(lo, jnp.zeros_like(qc), qc)
    for c in range(N_IDX_HEADS // 2):
        ic = iq_ref[:, c * LANES:(c + 1) * LANES]
        iqs_ref[(2 * c) * qb:(2 * c + 1) * qb, :] = jnp.where(lo, ic, jnp.zeros_like(ic))
        iqs_ref[(2 * c + 1) * qb:(2 * c + 2) * qb, :] = jnp.where(lo, jnp.zeros_like(ic), ic)

    wq = w_ref[...]
    wcol = [wq[:, IDX_DIM + h:IDX_DIM + h + 1] * (N_IDX_HEADS ** -0.5) for h in range(N_IDX_HEADS)]
    row_g = i * qb + lax.broadcasted_iota(I32, (qb, kc), 0)
    col_l = lax.broadcasted_iota(I32, (qb, kc), 1)

    def score_chunk(c, carry):
        off = pl.multiple_of(c * kc, kc)
        logits = _dot_t(iqs_ref[...], ik2_ref[pl.ds(off, kc), :])
        sc = wcol[0] * jnp.maximum(logits[0:qb], 0.0)
        for h in range(1, N_IDX_HEADS):
            sc = sc + wcol[h] * jnp.maximum(logits[h * qb:(h + 1) * qb], 0.0)
        key = jnp.where(off + col_l <= row_g, _sort_key(sc), INT_MIN)
        keys_ref[c] = key
        hi_ref[c] = (key >> 16).astype(I16)
        lo_ref[c] = ((key & 0xFFFF) + I16_MIN).astype(I16)
        return carry

    lax.fori_loop(0, n_chunks, score_chunk, 0)

    def count16(ref, cand, strict):
        cb = jnp.broadcast_to(cand, (qb, LANES)).astype(I16)
        cbw = jnp.concatenate([cb] * (kc // LANES), axis=1)

        def body(c, acc):
            hit = ref[c] > cbw if strict else ref[c] >= cbw
            return acc + _lane_fold(jnp.where(hit, jnp.int16(1), jnp.int16(0)))

        acc = lax.fori_loop(0, n_chunks, body, jnp.zeros((qb, LANES), I16))
        return jnp.sum(acc.astype(F32), axis=1, keepdims=True)

    def count_keys(pred):
        def body(c, acc):
            return acc + _lane_fold(jnp.where(pred(keys_ref[c]), 1.0, 0.0))
        acc = lax.fori_loop(0, n_chunks, body, jnp.zeros((qb, LANES), F32))
        return jnp.sum(acc, axis=1, keepdims=True)

    def count_keys_cols(pred):
        def body(c, acc):
            return acc + _lane_fold(jnp.where(pred(keys_ref[c], c * kc + col_l), 1.0, 0.0))
        acc = lax.fori_loop(0, n_chunks, body, jnp.zeros((qb, LANES), F32))
        return jnp.sum(acc, axis=1, keepdims=True)

    t_hi = _select_threshold16(lambda cand: count16(hi_ref, cand, False), qb, float(topk))
    need_lo = float(topk) - count16(hi_ref, t_hi, True)
    t_hi16 = jnp.concatenate([jnp.broadcast_to(t_hi, (qb, LANES)).astype(I16)] * (kc // LANES), axis=1)

    def keep_bucket(c, carry):
        lo_ref[c] = jnp.where(hi_ref[c] == t_hi16, lo_ref[c], jnp.int16(I16_MIN))
        return carry

    lax.fori_loop(0, n_chunks, keep_bucket, 0)
    t_lo = _select_threshold16(lambda cand: count16(lo_ref, cand, False), qb, need_lo)
    thr = lax.shift_left(t_hi, jnp.int32(16)) + (t_lo - I16_MIN)
    thr = jnp.maximum(thr, INT_MIN + 1)
    over = count_keys(lambda kk: kk >= thr) > float(topk)

    @pl.when(jnp.max(jnp.where(over, 1.0, 0.0)) > 0.0)
    def _():
        need = float(topk) - count_keys(lambda kk: kk > thr)
        cut = _tie_cutoff(count_keys_cols, thr, need, idx_bits)
        cut = jnp.where(over, cut, jnp.int32(1 << idx_bits))

        def drop(c, carry):
            kk = keys_ref[c]
            keys_ref[c] = jnp.where((kk == thr) & (c * kc + col_l >= cut), INT_MIN, kk)
            return carry

        lax.fori_loop(0, n_chunks, drop, 0)

    m_ref[...] = jnp.full(m_ref.shape, NEG_BIG, F32)
    acc_ref[...] = jnp.zeros(acc_ref.shape, F32)

    last = n_chunks - 1

    def logits(c):
        off = pl.multiple_of(jnp.minimum(c, last) * kc, kc)
        return _dot_t(qs_ref[...], kb_ref[pl.ds(off, kc), :])

    def softmax_pv(c, s_ref):
        cc = jnp.minimum(c, last)
        off = pl.multiple_of(cc * kc, kc)
        bias_ref[...] = jnp.where(c <= last, jnp.where(keys_ref[cc] >= thr, 0.0, NEG_BIG), NEG_BIG)
        for h in range(N_HEADS):
            r = slice(h * qb, (h + 1) * qb)
            s = s_ref[r, :] + bias_ref[...]
            m_old = m_ref[r, :]
            m_new = jnp.maximum(m_old, jnp.max(s, axis=1, keepdims=True))
            p_ref[r, :] = jnp.exp2(s - jnp.concatenate([m_new] * (kc // LANES), axis=1)).astype(BF16)
            a_ref[r, :] = jnp.exp2(m_old - m_new)
            m_ref[r, :] = m_new
        alpha = a_ref[...]
        acc_ref[...] = (jnp.concatenate([alpha, alpha], axis=1) * acc_ref[...]
                        + _dot(p_ref[...], vb_ref[pl.ds(off, kc), :]))

    s0_ref[...] = logits(0)

    def chunk_pair(j, carry):
        c0 = 2 * j
        s1_ref[...] = logits(c0 + 1)
        softmax_pv(c0, s0_ref)
        s0_ref[...] = logits(c0 + 2)
        softmax_pv(c0 + 1, s1_ref)
        return carry

    lax.fori_loop(0, lax.shift_right_logical(n_chunks + 1, jnp.int32(1)), chunk_pair, 0)

    out = acc_ref[:, 0:LANES] / acc_ref[:, LANES:2 * LANES]
    for c in range(N_HEADS // 2):
        oc = jnp.where(lo, out[c * qb:(c + 1) * qb], out[(c + 4) * qb:(c + 5) * qb])
        o_ref[:, c * LANES:(c + 1) * LANES] = oc.astype(BF16)


def _attn_prompt(q, iq, ikw, ik2, kb, vb, batch, seq, qb, kc):
    nqb = seq // qb
    topk = min(TOPK_MAX, seq // 4)
    tokq = lambda width: pl.BlockSpec((qb, width), lambda b, i: (b * nqb + i, 0))
    seqk = lambda width: pl.BlockSpec((None, seq, width), lambda b, i: (b, 0, 0))
    as_seq = lambda a: a.reshape(batch, seq, a.shape[-1])
    return pl.pallas_call(
        functools.partial(_attn_prompt_body, qb=qb, kc=kc, topk=topk, idx_bits=(seq - 1).bit_length()),
        grid=(batch, nqb),
        in_specs=[tokq(ATTN_DIM), tokq(N_IDX_HEADS * IDX_DIM), tokq(LANES),
                  seqk(LANES), seqk(KV_DIM), seqk(2 * KV_DIM)],
        out_specs=tokq(ATTN_DIM),
        out_shape=jax.ShapeDtypeStruct((batch * seq, ATTN_DIM), BF16),
        scratch_shapes=[
            pltpu.VMEM((seq // kc, qb, kc), I32),
            pltpu.VMEM((seq // kc, qb, kc), I16),
            pltpu.VMEM((seq // kc, qb, kc), I16),
            pltpu.VMEM((N_HEADS * qb, LANES), BF16),
            pltpu.VMEM((N_IDX_HEADS * qb, LANES), BF16),
            pltpu.VMEM((N_HEADS * qb, kc), F32),
            pltpu.VMEM((N_HEADS * qb, kc), F32),
            pltpu.VMEM((N_HEADS * qb, kc), BF16),
            pltpu.VMEM((qb, kc), F32),
            pltpu.VMEM((N_HEADS * qb, LANES), F32),
            pltpu.VMEM((N_HEADS * qb, LANES), F32),
            pltpu.VMEM((N_HEADS * qb, 2 * KV_DIM), F32),
        ],
        compiler_params=_cparams(("arbitrary", "arbitrary")),
        name="attn_prompt",
    )(q, iq, ikw, as_seq(ik2), as_seq(kb), as_seq(vb))


def _stack_heads(x_bf, n_heads):
    x = x_bf.astype(F32)
    return jnp.concatenate([x[:, h * HEAD_DIM:(h + 1) * HEAD_DIM] for h in range(n_heads)], axis=0).astype(BF16)


def _weighted_relu_sum(logits, wcol, t):
    sc = wcol[0] * jnp.maximum(logits[0:t], 0.0)
    for h in range(1, N_IDX_HEADS):
        sc = sc + wcol[h] * jnp.maximum(logits[h * t:(h + 1) * t], 0.0)
    return sc


def _head_weights(w_ref):
    wq = w_ref[...]
    return [wq[:, IDX_DIM + h:IDX_DIM + h + 1] * (N_IDX_HEADS ** -0.5) for h in range(N_IDX_HEADS)]


def _sample_scores_body(pt_ref, iq_ref, w_ref, *refs, pg, t):
    del pt_ref
    pages = refs[:pg]
    keys_ref, nkeys_ref, ikt_ref = refs[pg:]
    iqs = _stack_heads(iq_ref[...], N_IDX_HEADS)
    wcol = _head_weights(w_ref)
    for j in range(pg):
        ikt_ref[:, j * PAGE_SIZE:(j + 1) * PAGE_SIZE] = pages[j][...].astype(BF16)
    logits = _dot(iqs, ikt_ref[...])
    keys_ref[...] = _sort_key(_weighted_relu_sum(logits, wcol, t))

    @pl.when(pl.program_id(1) == 0)
    def _():
        ikn = jnp.concatenate([w_ref[...][:, 0:IDX_DIM], jnp.zeros((LANES - t, IDX_DIM), F32)], axis=0)
        sc = _weighted_relu_sum(_dot_t(iqs, ikn.astype(BF16)), wcol, t)
        col_n = lax.broadcasted_iota(I32, (t, LANES), 1)
        row_n = lax.broadcasted_iota(I32, (t, LANES), 0)
        nkeys_ref[...] = jnp.where(col_n <= row_n, _sort_key(sc), INT_MIN)


def _sample_scores(page_table, iq, ikw, cache_idx_t, db, t, pg):
    n_pages = page_table.shape[1]
    page_spec = lambda j: pl.BlockSpec((None, IDX_DIM, PAGE_SIZE),
                                       lambda b, g, pt: (pt[b, g * pg + j], 0, 0))
    grid_spec = pltpu.PrefetchScalarGridSpec(
        num_scalar_prefetch=1,
        grid=(db, n_pages // pg),
        in_specs=[pl.BlockSpec((t, N_IDX_HEADS * IDX_DIM), lambda b, g, pt: (b, 0)),
                  pl.BlockSpec((t, LANES), lambda b, g, pt: (b, 0))] + [page_spec(j) for j in range(pg)],
        out_specs=[pl.BlockSpec((t, pg * PAGE_SIZE), lambda b, g, pt: (b, g)),
                   pl.BlockSpec((t, LANES), lambda b, g, pt: (b, 0))],
        scratch_shapes=[pltpu.VMEM((IDX_DIM, pg * PAGE_SIZE), BF16)],
    )
    return pl.pallas_call(
        functools.partial(_sample_scores_body, pg=pg, t=t),
        grid_spec=grid_spec,
        out_shape=[jax.ShapeDtypeStruct((db * t, n_pages * PAGE_SIZE), I32),
                   jax.ShapeDtypeStruct((db * t, LANES), I32)],
        compiler_params=_cparams(("arbitrary", "arbitrary")),
        name="sample_scores",
    )(page_table, iq, ikw, *([cache_idx_t] * pg))


def _sample_thr_body(keys_ref, nkeys_ref, thr_ref, cut_ref, *, rows, topk, idx_bits, cw):
    past = keys_ref.shape[-1]
    col_c = lax.broadcasted_iota(I32, (rows, cw), 1)
    col_n = lax.broadcasted_iota(I32, (rows, LANES), 1)

    def count_keys(pred):
        acc = jnp.where(pred(nkeys_ref[...]), 1.0, 0.0)
        for c in range(past // cw):
            acc = acc + _lane_fold(jnp.where(pred(keys_ref[:, c * cw:(c + 1) * cw]), 1.0, 0.0))
        return jnp.sum(acc, axis=1, keepdims=True)

    def count_keys_cols(pred):
        acc = jnp.where(pred(nkeys_ref[...], past + col_n), 1.0, 0.0)
        for c in range(past // cw):
            acc = acc + _lane_fold(jnp.where(pred(keys_ref[:, c * cw:(c + 1) * cw], c * cw + col_c), 1.0, 0.0))
        return jnp.sum(acc, axis=1, keepdims=True)

    thr = _select_threshold(count_keys, rows, float(topk))
    thr = jnp.maximum(thr, INT_MIN + 1)
    over = count_keys(lambda kk: kk >= thr) > float(topk)
    thr_ref[...] = jnp.broadcast_to(thr, (rows, LANES))
    cut_ref[...] = jnp.full((rows, LANES), 1 << idx_bits, I32)

    @pl.when(jnp.max(jnp.where(over, 1.0, 0.0)) > 0.0)
    def _():
        need = float(topk) - count_keys(lambda kk: kk > thr)
        cut = _tie_cutoff(count_keys_cols, thr, need, idx_bits)
        cut_ref[...] = jnp.broadcast_to(jnp.where(over, cut, jnp.int32(1 << idx_bits)), (rows, LANES))


def _sample_thr(keys, nkeys, t, rows):
    n, past = keys.shape
    topk = min(TOPK_MAX, (past + t) // 4)
    blk = lambda width: pl.BlockSpec((rows, width), lambda i: (i, 0))
    return pl.pallas_call(
        functools.partial(_sample_thr_body, rows=rows, topk=topk,
                          idx_bits=(past + LANES - 1).bit_length(), cw=2048),
        grid=(n // rows,),
        in_specs=[blk(past), blk(LANES)],
        out_specs=[blk(LANES), blk(LANES)],
        out_shape=[jax.ShapeDtypeStruct((n, LANES), I32), jax.ShapeDtypeStruct((n, LANES), I32)],
        compiler_params=_cparams(("arbitrary",)),
        name="sample_thr",
    )(keys, nkeys)


def _sample_attn_body(pt_ref, keys_ref, nkeys_ref, thr_ref, cut_ref, q_ref, kn_ref, vn_ref, *refs,
                      pg, t, past):
    del pt_ref
    kpages, vpages = refs[:pg], refs[pg:2 * pg]
    o_ref = refs[2 * pg]
    qs_ref, kt_ref, vt_ref, m_ref, l_ref, acc_ref = refs[2 * pg + 1:]
    g = pl.program_id(1)
    rows = N_HEADS * t

    def flash_update(s, sel, pv):
        n = s.shape[-1]
        s = jnp.where(sel[None], s.reshape(N_HEADS, t, n), NEG_BIG).reshape(rows, n)
        m_old = m_ref[...]
        m_new = jnp.maximum(m_old, jnp.max(s, axis=1, keepdims=True))
        alpha = jnp.exp2(m_old - m_new)
        p = jnp.exp2(s - m_new)
        l_ref[...] = alpha * l_ref[...] + jnp.sum(p, axis=1, keepdims=True)
        acc_ref[...] = alpha * acc_ref[...] + pv(p.astype(BF16))
        m_ref[...] = m_new

    @pl.when(g == 0)
    def _():
        lo = lax.broadcasted_iota(I32, (t, LANES), 1) < HEAD_DIM
        for c in range(N_HEADS // 2):
            qc = q_ref[:, c * LANES:(c + 1) * LANES]
            qs_ref[c * t:(c + 1) * t, :] = jnp.where(lo, qc, jnp.zeros_like(qc))
            qs_ref[(c + 4) * t:(c + 5) * t, :] = jnp.where(lo, jnp.zeros_like(qc), qc)
        m_ref[...] = jnp.full(m_ref.shape, NEG_BIG, F32)
        l_ref[...] = jnp.zeros(l_ref.shape, F32)
        acc_ref[...] = jnp.zeros(acc_ref.shape, F32)

    thr = thr_ref[:, 0:1]
    cut = cut_ref[:, 0:1]

    def selected(kk, col):
        return (kk > thr) | ((kk == thr) & (col < cut))

    qs = qs_ref[...]
    width = pg * PAGE_SIZE
    for j in range(pg):
        kt_ref[:, j * PAGE_SIZE:(j + 1) * PAGE_SIZE] = kpages[j][...].astype(BF16)
        vt_ref[:, j * PAGE_SIZE:(j + 1) * PAGE_SIZE] = vpages[j][...].astype(BF16)
    col_w = g * width + lax.broadcasted_iota(I32, (t, width), 1)
    flash_update(_dot(qs, kt_ref[...]), selected(keys_ref[...], col_w),
                 lambda p: _dot_t(p, vt_ref[...]))

    @pl.when(g == pl.num_programs(1) - 1)
    def _():
        pad = jnp.zeros((LANES - t, LANES), F32)
        kn = jnp.concatenate([kn_ref[...], pad], axis=0).astype(BF16)
        vn = jnp.concatenate([vn_ref[...], pad], axis=0).astype(BF16)
        col_n = past + lax.broadcasted_iota(I32, (t, LANES), 1)
        flash_update(_dot_t(qs, kn), selected(nkeys_ref[...], col_n), lambda p: _dot(p, vn))
        lo = lax.broadcasted_iota(I32, (t, LANES), 1) < HEAD_DIM
        out = acc_ref[...] / l_ref[...]
        for c in range(N_HEADS // 2):
            oc = jnp.where(lo, out[c * t:(c + 1) * t], out[(c + 4) * t:(c + 5) * t])
            o_ref[:, c * LANES:(c + 1) * LANES] = oc.astype(BF16)


def _sample_attn(page_table, keys, nkeys, thr, cut, q, k_new, v_new, cache_kt, cache_vt, db, t, pg):
    n_pages = page_table.shape[1]
    past = n_pages * PAGE_SIZE
    page_spec = lambda j: pl.BlockSpec((None, KV_DIM, PAGE_SIZE),
                                       lambda b, g, pt: (pt[b, g * pg + j], 0, 0))
    tok = lambda width: pl.BlockSpec((t, width), lambda b, g, pt: (b, 0))
    grid_spec = pltpu.PrefetchScalarGridSpec(
        num_scalar_prefetch=1,
        grid=(db, n_pages // pg),
        in_specs=[pl.BlockSpec((t, pg * PAGE_SIZE), lambda b, g, pt: (b, g)),
                  tok(LANES), tok(LANES), tok(LANES), tok(ATTN_DIM), tok(KV_DIM), tok(KV_DIM)]
                 + [page_spec(j) for j in range(pg)] + [page_spec(j) for j in range(pg)],
        out_specs=tok(ATTN_DIM),
        scratch_shapes=[
            pltpu.VMEM((N_HEADS * t, LANES), BF16),
            pltpu.VMEM((KV_DIM, pg * PAGE_SIZE), BF16),
            pltpu.VMEM((KV_DIM, pg * PAGE_SIZE), BF16),
            pltpu.VMEM((N_HEADS * t, 1), F32),
            pltpu.VMEM((N_HEADS * t, 1), F32),
            pltpu.VMEM((N_HEADS * t, LANES), F32),
        ],
    )
    return pl.pallas_call(
        functools.partial(_sample_attn_body, pg=pg, t=t, past=past),
        grid_spec=grid_spec,
        out_shape=jax.ShapeDtypeStruct((db * t, ATTN_DIM), BF16),
        compiler_params=_cparams(("arbitrary", "arbitrary")),
        name="sample_attn",
    )(page_table, keys, nkeys, thr, cut, q, k_new, v_new, *([cache_kt] * pg), *([cache_vt] * pg))


def _out_body(x_ref, a_ref, c_ref, wa_ref, wc_ref, o_ref):
    o_ref[...] = x_ref[...] + _dot(a_ref[...], wa_ref[...]) + _dot(c_ref[...], wc_ref[...])


def _out_proj(x, attn, conv, wa, wc, tm):
    n = x.shape[0]
    tok = lambda width: pl.BlockSpec((tm, width), lambda i: (i, 0))
    full = lambda a: pl.BlockSpec(a.shape, lambda i: (0, 0))
    return pl.pallas_call(
        _out_body,
        grid=(n // tm,),
        in_specs=[tok(D_MODEL), tok(ATTN_DIM), tok(CONV_DIM), full(wa), full(wc)],
        out_specs=tok(D_MODEL),
        out_shape=jax.ShapeDtypeStruct((n, D_MODEL), F32),
        compiler_params=_cparams(("arbitrary",)),
        name="out_proj",
    )(x, attn, conv, wa, wc)


def _ple_body(x_ref, pe_ref, g_ref, wg_ref, wp_ref, gf_ref, o_ref):
    x = x_ref[...]
    gate = jax.nn.sigmoid(_dot(_rms(x, g_ref[...]).astype(BF16), wg_ref[...]))
    x = x + gate * _dot(pe_ref[...].astype(BF16), wp_ref[...])
    o_ref[...] = _rms(x, gf_ref[...])


def _ple_final(x, pe, g, wg, wp, gf, tm):
    n = x.shape[0]
    tok = lambda width: pl.BlockSpec((tm, width), lambda i: (i, 0))
    full = lambda a: pl.BlockSpec(a.shape, lambda i: (0, 0))
    return pl.pallas_call(
        _ple_body,
        grid=(n // tm,),
        in_specs=[tok(D_MODEL), tok(PLE_DIM), full(g), full(wg), full(wp), full(gf)],
        out_specs=tok(D_MODEL),
        out_shape=jax.ShapeDtypeStruct((n, D_MODEL), F32),
        compiler_params=_cparams(("arbitrary",)),
        name="ple_final",
    )(x, pe, g, wg, wp, gf)


def _rope_tables(pos):
    half = HEAD_DIM // 2
    inv = ROPE_THETA ** (-jnp.arange(half, dtype=F32) / half)
    ang = pos.astype(F32)[:, None] * inv[None, :]
    cos, sin = jnp.cos(ang), jnp.sin(ang)
    cos2 = jnp.concatenate([cos, cos], axis=1)
    sin2 = jnp.concatenate([-sin, sin], axis=1)
    cos128 = jnp.concatenate([cos2, cos2], axis=1)
    sin128 = jnp.concatenate([sin2, sin2], axis=1)
    cosk = jnp.concatenate([cos2, jnp.ones_like(cos2)], axis=1)
    sink = jnp.concatenate([sin2, jnp.zeros_like(sin2)], axis=1)
    return cos128, sin128, cosk, sink


def _prep_w_in(w_in):
    d = w_in.shape[0]
    q = w_in[:, :ATTN_DIM].reshape(d, N_HEADS, HEAD_DIM)[:, jnp.array(_HEAD_PERM)].reshape(d, ATTN_DIM)
    o = ATTN_DIM
    kv = w_in[:, o:o + 2 * KV_DIM]
    o += 2 * KV_DIM
    iq = w_in[:, o:o + N_IDX_HEADS * IDX_DIM]
    o += N_IDX_HEADS * IDX_DIM
    ikw = w_in[:, o:o + IDX_DIM + N_IDX_HEADS]
    o += IDX_DIM + N_IDX_HEADS
    ikw = jnp.pad(ikw, ((0, 0), (0, LANES - IDX_DIM - N_IDX_HEADS)))
    conv = w_in[:, o:]
    return jnp.concatenate([q, kv, iq, ikw, conv], axis=1).astype(BF16)


def _layer(x, pe, mixer, tm, g_ffn1, w1g, w1u, w1d, g_mix, w_in, conv_w, wo_a, wo_c,
           g_ffn2, w2g, w2u, w2d, g_ple, w_pg, w_pp, g_final):
    x1 = _ffn(x, g_ffn1, w1g, w1u, w1d, tm)
    attn, conv, state = mixer(x1, g_mix, w_in, conv_w)
    x2 = _out_proj(x1, attn, conv, wo_a, wo_c, tm)
    x3 = _ffn(x2, g_ffn2, w2g, w2u, w2d, tm)
    return _ple_final(x3, pe, g_ple, w_pg, w_pp, g_final, tm), state


def kernel(x_prompt, x_sample, cache_k, cache_v, cache_idx_k, state_conv, page_table, p_prompt, p_sample,
           g_ffn1, w1_gate, w1_up, w1_down, g_mix, w_in, conv_w, w_out,
           g_ffn2, w2_gate, w2_up, w2_down, g_ple, w_ple_gate, w_ple_proj, g_final):
    depth = w_in.shape[0]
    assert depth == 1, "single-layer step"
    batch, seq, _ = x_prompt.shape
    db, t, _ = x_sample.shape
    n_pages = page_table.shape[1]
    past = n_pages * PAGE_SIZE
    assert t == SUBLANES and seq % 512 == 0

    row = lambda gvec: gvec.reshape(1, -1)
    perm = jnp.array(_HEAD_PERM)
    wo_a = w_out[0, :ATTN_DIM].reshape(N_HEADS, HEAD_DIM, D_MODEL)[perm].reshape(ATTN_DIM, D_MODEL).astype(BF16)
    wo_c = w_out[0, ATTN_DIM:].astype(BF16)
    shared = (row(g_ffn1[0]), w1_gate[0].astype(BF16), w1_up[0].astype(BF16), w1_down[0].astype(BF16),
              row(g_mix[0]), _prep_w_in(w_in[0]), conv_w[0], wo_a, wo_c,
              row(g_ffn2[0]), w2_gate[0].astype(BF16), w2_up[0].astype(BF16), w2_down[0].astype(BF16),
              row(g_ple[0]), w_ple_gate[0].astype(BF16), w_ple_proj[0].astype(BF16), row(g_final))

    tm_p = 512
    tabs_p = _rope_tables(jnp.arange(seq, dtype=I32))

    def prompt_mixer(x1, g, w, cw):
        q, iq, k, v, ikw, kb, vb, ik2, cy, tail = _proj(x1, g, w, *tabs_p, cw, tm_p, seq // tm_p)
        attn = _attn_prompt(q, iq, ikw, ik2, kb, vb, batch, seq, qb=256, kc=512)
        return attn, cy, (k, v, ikw, tail)

    yp, (kp, vp, ikwp, tailp) = _layer(x_prompt.reshape(batch * seq, D_MODEL),
                                       p_prompt[0].reshape(batch * seq, PLE_DIM), prompt_mixer, tm_p, *shared)

    n_s = db * t
    tabs_s = tuple(jnp.tile(a, (db, 1)) for a in _rope_tables(past + jnp.arange(t, dtype=I32)))
    buf = state_conv[0]
    zero = jnp.zeros((db, t - 2, CONV_DIM), F32)
    halo1 = jnp.concatenate([buf[:, 1:2], jnp.zeros((db, t - 1, CONV_DIM), F32)], axis=1).reshape(n_s, CONV_DIM)
    halo2 = jnp.concatenate([buf, zero], axis=1).reshape(n_s, CONV_DIM)
    idx_t = jnp.transpose(cache_idx_k[0], (0, 2, 1))
    k_t = jnp.transpose(cache_k[0], (0, 2, 3, 1)).reshape(-1, KV_DIM, PAGE_SIZE)
    v_t = jnp.transpose(cache_v[0], (0, 2, 3, 1)).reshape(-1, KV_DIM, PAGE_SIZE)

    def sample_mixer(x1, g, w, cw):
        q, iq, k, v, ikw, kb, vb, ik2, cy, u = _proj(x1, g, w, *tabs_s, cw, n_s, 1, halos=(halo1, halo2))
        keys, nkeys = _sample_scores(page_table, iq, ikw, idx_t, db, t, pg=16)
        thr, cut = _sample_thr(keys, nkeys, t, rows=min(64, n_s))
        attn = _sample_attn(page_table, keys, nkeys, thr, cut, q, k, v, k_t, v_t, db, t, pg=16)
        return attn, cy, (k, v, ikw, u)

    ys, (ks, vs, ikws, us) = _layer(x_sample.reshape(n_s, D_MODEL), p_sample[0].reshape(n_s, PLE_DIM),
                                    sample_mixer, n_s, *shared)

    return (yp.reshape(batch, seq, D_MODEL),
            ys.reshape(db, t, D_MODEL),
            kp.reshape(1, batch, seq, N_KV_HEADS, HEAD_DIM),
            vp.reshape(1, batch, seq, N_KV_HEADS, HEAD_DIM),
            ikwp[:, :IDX_DIM].reshape(1, batch, seq, IDX_DIM),
            tailp[:, SUBLANES - (CONV_WIDTH - 1):][None],
            ks.reshape(1, db, t, N_KV_HEADS, HEAD_DIM),
            vs.reshape(1, db, t, N_KV_HEADS, HEAD_DIM),
            ikws[:, :IDX_DIM].reshape(1, db, t, IDX_DIM),
            us.reshape(db, t, CONV_DIM)[:, t - (CONV_WIDTH - 1):][None])
```

```python
import functools

import jax
import jax.numpy as jnp
from jax import lax
from jax.experimental import pallas as pl
from jax.experimental.pallas import tpu as pltpu

F32 = jnp.float32
BF16 = jnp.bfloat16
I32 = jnp.int32

D_MODEL = 1024
N_HEADS = 8
HEAD_DIM = 64
N_KV_HEADS = 2
ATTN_DIM = N_HEADS * HEAD_DIM
KV_DIM = N_KV_HEADS * HEAD_DIM
N_IDX_HEADS = 4
IDX_DIM = 64
TOPK_MAX = 256
CONV_DIM = D_MODEL - ATTN_DIM
CONV_WIDTH = 3
D_FF = 2816
PLE_DIM = 256
PAGE_SIZE = 128
ROPE_THETA = 10000.0
RMS_EPS = 1e-6

ROW_TILE = 64
LANES = 128
SUBLANES = 8
VMEM_LIMIT = 56 * 1024 * 1024

INT_MIN = -(2 ** 31)
MASKED_SCORE = -1e38
MIN_THRESHOLD = -5e37
NEG_BIG = -1e30
LOG2E = 1.4426950408889634

_Q0, _K0, _V0, _IQ0, _IK0, _CB0, _CC0, _CH0, _PROJ_W = 0, 512, 640, 768, 1024, 1152, 1664, 2176, 2688
_HEAD_PERM = (0, 4, 1, 5, 2, 6, 3, 7)


def _cparams(sem):
    return pltpu.CompilerParams(dimension_semantics=sem, vmem_limit_bytes=VMEM_LIMIT)


def _dot(a, b):
    return jnp.dot(a, b, preferred_element_type=F32)


def _dot_t(a, b):
    return lax.dot_general(a, b, (((1,), (1,)), ((), ())), preferred_element_type=F32)


def _rms(x, g):
    ms = jnp.mean(x * x, axis=-1, keepdims=True)
    return x * lax.rsqrt(ms + RMS_EPS) * g


def _ffn_body(x_ref, g_ref, wg_ref, wu_ref, wd_ref, o_ref, h_ref, acc_ref):
    j = pl.program_id(1)

    @pl.when(j == 0)
    def _():
        h_ref[...] = _rms(x_ref[...], g_ref[...]).astype(BF16)

    h = h_ref[...]
    gate = _dot(h, wg_ref[...])
    up = _dot(h, wu_ref[...])
    act = (gate * jax.nn.sigmoid(gate) * up).astype(BF16)
    part = _dot(act, wd_ref[...])

    @pl.when(j == 0)
    def _():
        acc_ref[...] = part

    @pl.when(j > 0)
    def _():
        acc_ref[...] += part

    @pl.when(j == pl.num_programs(1) - 1)
    def _():
        o_ref[...] = x_ref[...] + 0.5 * acc_ref[...]


def _ffn(x, g, wg, wu, wd, tm):
    n = x.shape[0]
    ff_chunk = D_FF // 2
    return pl.pallas_call(
        _ffn_body,
        grid=(n // tm, D_FF // ff_chunk),
        in_specs=[
            pl.BlockSpec((tm, D_MODEL), lambda i, j: (i, 0)),
            pl.BlockSpec((1, D_MODEL), lambda i, j: (0, 0)),
            pl.BlockSpec((D_MODEL, ff_chunk), lambda i, j: (0, j)),
            pl.BlockSpec((D_MODEL, ff_chunk), lambda i, j: (0, j)),
            pl.BlockSpec((ff_chunk, D_MODEL), lambda i, j: (j, 0)),
        ],
        out_specs=pl.BlockSpec((tm, D_MODEL), lambda i, j: (i, 0)),
        out_shape=jax.ShapeDtypeStruct((n, D_MODEL), F32),
        scratch_shapes=[pltpu.VMEM((tm, D_MODEL), BF16), pltpu.VMEM((tm, D_MODEL), F32)],
        compiler_params=_cparams(("arbitrary", "arbitrary")),
        name="ffn",
    )(x, g, wg, wu, wd)


def _swap32(x):
    w = x.shape[-1]
    lane = lax.broadcasted_iota(I32, x.shape, 1)
    return jnp.where((lane & 63) < 32, pltpu.roll(x, w - 32, 1), pltpu.roll(x, 32, 1))


def _rope_cols(z, cos, sin):
    cols = []
    for c in range(z.shape[-1] // LANES):
        zc = z[:, c * LANES:(c + 1) * LANES]
        cols.append(zc * cos + _swap32(zc) * sin)
    return cols[0] if len(cols) == 1 else jnp.concatenate(cols, axis=1)


def _proj_body(*refs, tm, tiles_per_seq, halo):
    if halo:
        (x_ref, g_ref, w_ref, cos_ref, sin_ref, cosk_ref, sink_ref, cw_ref, h1_ref, h2_ref,
         q_ref, iq_ref, k_ref, v_ref, ikw_ref, kb_ref, vb_ref, ik2_ref, cy_ref, u_ref) = refs
    else:
        (x_ref, g_ref, w_ref, cos_ref, sin_ref, cosk_ref, sink_ref, cw_ref,
         q_ref, iq_ref, k_ref, v_ref, ikw_ref, kb_ref, vb_ref, ik2_ref, cy_ref, u_ref, carry_ref) = refs

    h = _rms(x_ref[...], g_ref[...]).astype(BF16)
    cos, sin = cos_ref[...], sin_ref[...]

    zq = _dot(h, w_ref[:, _Q0:_K0])
    q_ref[...] = (_rope_cols(zq, cos, sin) * (HEAD_DIM ** -0.5 * LOG2E)).astype(BF16)
    kr = _rope_cols(_dot(h, w_ref[:, _K0:_V0]), cos, sin)
    k_ref[...] = kr
    kb_ref[...] = kr.astype(BF16)
    zv = _dot(h, w_ref[:, _V0:_IQ0])
    v_ref[...] = zv
    vb_ref[...] = jnp.concatenate([zv, jnp.ones_like(zv)], axis=1).astype(BF16)
    ziq = _dot(h, w_ref[:, _IQ0:_IK0])
    iq_ref[...] = (_rope_cols(ziq, cos, sin) * (IDX_DIM ** -0.5)).astype(BF16)
    ikr = _rope_cols(_dot(h, w_ref[:, _IK0:_CB0]), cosk_ref[...], sink_ref[...])
    ikw_ref[...] = ikr
    lane = lax.broadcasted_iota(I32, ikr.shape, 1)
    ik2_ref[...] = jnp.where(lane < IDX_DIM, ikr, pltpu.roll(ikr, IDX_DIM, 1)).astype(BF16)

    cb = _dot(h, w_ref[:, _CB0:_CC0])
    u = _dot(h, w_ref[:, _CC0:_CH0]) * _dot(h, w_ref[:, _CH0:_PROJ_W])
    row = lax.broadcasted_iota(I32, u.shape, 0)
    r1 = pltpu.roll(u, 1, 0)
    r2 = pltpu.roll(u, 2, 0)
    if halo:
        t = row & (SUBLANES - 1)
        us1 = jnp.where(t == 0, h1_ref[...], r1)
        us2 = jnp.where(t < 2, h2_ref[...], r2)
        u_ref[...] = u
    else:
        @pl.when(pl.program_id(0) % tiles_per_seq == 0)
        def _():
            carry_ref[...] = jnp.zeros_like(carry_ref)

        c0 = carry_ref[SUBLANES - 2:SUBLANES - 1, :]
        c1 = carry_ref[SUBLANES - 1:SUBLANES, :]
        us1 = jnp.where(row == 0, c1, r1)
        us2 = jnp.where(row == 0, c0, jnp.where(row == 1, c1, r2))
        tail = u[tm - SUBLANES:tm, :]
        carry_ref[...] = tail
        u_ref[0] = tail
    cw = cw_ref[...]
    y = cb * (cw[0:1, :] * us2 + cw[1:2, :] * us1 + cw[2:3, :] * u)
    cy_ref[...] = y.astype(BF16)


def _proj(x, g, w, cos, sin, cosk, sink, cw, tm, tiles_per_seq, halos=None):
    n = x.shape[0]
    n_tiles = n // tm
    halo = halos is not None
    tok = lambda width: pl.BlockSpec((tm, width), lambda i: (i, 0))
    tab = pl.BlockSpec((tm, LANES), lambda i: (i % tiles_per_seq, 0))
    full = lambda a: pl.BlockSpec(a.shape, lambda i: (0,) * a.ndim)
    in_specs = [tok(D_MODEL), full(g), full(w), tab, tab, tab, tab, full(cw)]
    args = [x, g, w, cos, sin, cosk, sink, cw]
    out_shape = [
        jax.ShapeDtypeStruct((n, ATTN_DIM), BF16),
        jax.ShapeDtypeStruct((n, N_IDX_HEADS * IDX_DIM), BF16),
        jax.ShapeDtypeStruct((n, KV_DIM), F32),
        jax.ShapeDtypeStruct((n, KV_DIM), F32),
        jax.ShapeDtypeStruct((n, LANES), F32),
        jax.ShapeDtypeStruct((n, KV_DIM), BF16),
        jax.ShapeDtypeStruct((n, 2 * KV_DIM), BF16),
        jax.ShapeDtypeStruct((n, LANES), BF16),
        jax.ShapeDtypeStruct((n, CONV_DIM), BF16),
    ]
    out_specs = [tok(ATTN_DIM), tok(N_IDX_HEADS * IDX_DIM), tok(KV_DIM), tok(KV_DIM), tok(LANES),
                 tok(KV_DIM), tok(2 * KV_DIM), tok(LANES), tok(CONV_DIM)]
    scratch = []
    if halo:
        in_specs += [tok(CONV_DIM), tok(CONV_DIM)]
        args += list(halos)
        out_shape.append(jax.ShapeDtypeStruct((n, CONV_DIM), F32))
        out_specs.append(tok(CONV_DIM))
    else:
        n_seq = n_tiles // tiles_per_seq
        out_shape.append(jax.ShapeDtypeStruct((n_seq, SUBLANES, CONV_DIM), F32))
        out_specs.append(pl.BlockSpec((1, SUBLANES, CONV_DIM), lambda i: (i // tiles_per_seq, 0, 0)))
        scratch.append(pltpu.VMEM((SUBLANES, CONV_DIM), F32))
    return pl.pallas_call(
        functools.partial(_proj_body, tm=tm, tiles_per_seq=tiles_per_seq, halo=halo),
        grid=(n_tiles,),
        in_specs=in_specs,
        out_specs=out_specs,
        out_shape=out_shape,
        scratch_shapes=scratch,
        compiler_params=_cparams(("arbitrary",)),
        name="proj_sample" if halo else "proj_prompt",
    )(*args)


def _lane_fold(ind):
    acc = ind[:, 0:LANES]
    for j in range(1, ind.shape[-1] // LANES):
        acc = acc + ind[:, j * LANES:(j + 1) * LANES]
    return acc


def _key_to_f32(key):
    return lax.bitcast_convert_type(key ^ ((key >> 31) & 0x7FFFFFFF), F32)


def _select_threshold(count_ge, rows, topk):
    def step(b, t):
        cand = t + lax.shift_left(jnp.int32(1), jnp.int32(31) - b.astype(I32))
        return jnp.where(count_ge(_key_to_f32(cand)) >= topk, cand, t)

    return _key_to_f32(lax.fori_loop(0, 32, step, jnp.full((rows, 1), INT_MIN, I32)))


def _tie_cutoff(count_fn, t, need, idx_bits):
    def step(b, c):
        cand = c + lax.shift_left(jnp.int32(1), jnp.int32(idx_bits - 1) - b.astype(I32))
        cnt = count_fn(lambda s, col: (s == t) & (col < cand))
        return jnp.where(cnt <= need, cand, c)

    return lax.fori_loop(0, idx_bits, step, jnp.zeros(t.shape, I32))


def _attn_prompt_body(q_ref, iq_ref, w_ref, ik2_ref, kb_ref, vb_ref, o_ref,
                      sc_ref, qs_ref, iqs_ref, s0_ref, s1_ref, p_ref, bias_ref,
                      m_ref, a_ref, acc_ref,
                      *, qb, kc, topk, idx_bits):
    i = pl.program_id(1)
    n_chunks = lax.shift_right_logical((i + 1) * qb + (kc - 1), jnp.int32(kc.bit_length() - 1))
    lo = lax.broadcasted_iota(I32, (qb, LANES), 1) < HEAD_DIM

    for c in range(N_HEADS // 2):
        qc = q_ref[:, c * LANES:(c + 1) * LANES]
        qs_ref[c * qb:(c + 1) * qb, :] = jnp.where(lo, qc, jnp.zeros_like(qc))
        qs_ref[(c + 4) * qb:(c + 5) * qb, :] = jnp.where(lo, jnp.zeros_like(qc), qc)
    for c in range(N_IDX_HEADS // 2):
        ic = iq_ref[:, c * LANES:(c + 1) * LANES]
        iqs_ref[(2 * c) * qb:(2 * c + 1) * qb, :] = jnp.where(lo, ic, jnp.zeros_like(ic))
        iqs_ref[(2 * c + 1) * qb:(2 * c + 2) * qb, :] = jnp.where(lo, jnp.zeros_like(ic), ic)

    row_t = lax.broadcasted_iota(I32, (ROW_TILE, kc), 0)
    col_t = lax.broadcasted_iota(I32, (ROW_TILE, kc), 1)
    col_l = lax.broadcasted_iota(I32, (qb, kc), 1)

    def score_chunk(c, carry):
        off = pl.multiple_of(c * kc, kc)
        s0_ref[0:N_IDX_HEADS * qb, :] = _dot_t(iqs_ref[...], ik2_ref[pl.ds(off, kc), :])
        for r0 in range(0, qb, ROW_TILE):
            r = slice(r0, r0 + ROW_TILE)
            wt = w_ref[r, :] * (N_IDX_HEADS ** -0.5)
            sc = wt[:, IDX_DIM:IDX_DIM + 1] * jnp.maximum(s0_ref[r, :], 0.0)
            for h in range(1, N_IDX_HEADS):
                sc = sc + (wt[:, IDX_DIM + h:IDX_DIM + h + 1]
                           * jnp.maximum(s0_ref[h * qb + r0:h * qb + r0 + ROW_TILE, :], 0.0))
            sc_ref[c, r, :] = jnp.where(off + col_t <= i * qb + r0 + row_t, sc, MASKED_SCORE)
        return carry

    lax.fori_loop(0, n_chunks, score_chunk, 0)

    def kth_largest_score(n):
        def count_ge(cand):
            acc = jnp.zeros((qb, LANES), F32)
            for c in range(n):
                acc = acc + _lane_fold(jnp.where(sc_ref[c] >= cand, 1.0, 0.0))
            return jnp.sum(acc, axis=1, keepdims=True)

        return _select_threshold(count_ge, qb, float(topk))

    def count_scores(pred):
        def body(c, acc):
            return acc + _lane_fold(jnp.where(pred(sc_ref[c]), 1.0, 0.0))
        acc = lax.fori_loop(0, n_chunks, body, jnp.zeros((qb, LANES), F32))
        return jnp.sum(acc, axis=1, keepdims=True)

    def count_scores_cols(pred):
        def body(c, acc):
            return acc + _lane_fold(jnp.where(pred(sc_ref[c], c * kc + col_l), 1.0, 0.0))
        acc = lax.fori_loop(0, n_chunks, body, jnp.zeros((qb, LANES), F32))
        return jnp.sum(acc, axis=1, keepdims=True)

    max_chunks = sc_ref.shape[0]
    thr = lax.switch(n_chunks - 1, [functools.partial(kth_largest_score, n) for n in range(1, max_chunks + 1)])
    thr = jnp.maximum(thr, MIN_THRESHOLD)
    over = count_scores(lambda s: s >= thr) > float(topk)

    @pl.when(jnp.max(jnp.where(over, 1.0, 0.0)) > 0.0)
    def _():
        need = float(topk) - count_scores(lambda s: s > thr)
        cut = _tie_cutoff(count_scores_cols, thr, need, idx_bits)
        cut = jnp.where(over, cut, jnp.int32(1 << idx_bits))

        def drop(c, carry):
            s = sc_ref[c]
            sc_ref[c] = jnp.where((s == thr) & (c * kc + col_l >= cut), MASKED_SCORE, s)
            return carry

        lax.fori_loop(0, n_chunks, drop, 0)

    m_ref[...] = jnp.full(m_ref.shape, NEG_BIG, F32)
    acc_ref[...] = jnp.zeros(acc_ref.shape, F32)

    last = n_chunks - 1

    def logits(c):
        off = pl.multiple_of(jnp.minimum(c, last) * kc, kc)
        return _dot_t(qs_ref[...], kb_ref[pl.ds(off, kc), :])

    def softmax_pv(c, s_ref):
        cc = jnp.minimum(c, last)
        off = pl.multiple_of(cc * kc, kc)
        bias_ref[...] = jnp.where(c <= last, jnp.where(sc_ref[cc] >= thr, 0.0, NEG_BIG), NEG_BIG)
        for r0 in range(0, N_HEADS * qb, ROW_TILE):
            r = slice(r0, r0 + ROW_TILE)
            s = s_ref[r, :] + bias_ref[r0 % qb:r0 % qb + ROW_TILE, :]
            m_old = m_ref[r, :]
            m_new = jnp.maximum(m_old, jnp.max(s, axis=1, keepdims=True))
            p_ref[r, :] = jnp.exp2(s - jnp.concatenate([m_new] * (kc // LANES), axis=1)).astype(BF16)
            a_ref[r, :] = jnp.exp2(m_old - m_new)
            m_ref[r, :] = m_new
        alpha = a_ref[...]
        acc_ref[...] = (jnp.concatenate([alpha, alpha], axis=1) * acc_ref[...]
                        + _dot(p_ref[...], vb_ref[pl.ds(off, kc), :]))

    s0_ref[...] = logits(0)

    def chunk_pair(j, carry):
        c0 = 2 * j
        s1_ref[...] = logits(c0 + 1)
        softmax_pv(c0, s0_ref)
        s0_ref[...] = logits(c0 + 2)
        softmax_pv(c0 + 1, s1_ref)
        return carry

    lax.fori_loop(0, lax.shift_right_logical(n_chunks + 1, jnp.int32(1)), chunk_pair, 0)

    out = acc_ref[:, 0:LANES] / acc_ref[:, LANES:2 * LANES]
    for c in range(N_HEADS // 2):
        oc = jnp.where(lo, out[c * qb:(c + 1) * qb], out[(c + 4) * qb:(c + 5) * qb])
        o_ref[:, c * LANES:(c + 1) * LANES] = oc.astype(BF16)


def _attn_prompt(q, iq, ikw, ik2, kb, vb, batch, seq, qb, kc):
    nqb = seq // qb
    topk = min(TOPK_MAX, seq // 4)
    tokq = lambda width: pl.BlockSpec((qb, width), lambda b, i: (b * nqb + i, 0))
    seqk = lambda width: pl.BlockSpec((None, seq, width), lambda b, i: (b, 0, 0))
    as_seq = lambda a: a.reshape(batch, seq, a.shape[-1])
    return pl.pallas_call(
        functools.partial(_attn_prompt_body, qb=qb, kc=kc, topk=topk, idx_bits=(seq - 1).bit_length()),
        grid=(batch, nqb),
        in_specs=[tokq(ATTN_DIM), tokq(N_IDX_HEADS * IDX_DIM), tokq(LANES),
                  seqk(LANES), seqk(KV_DIM), seqk(2 * KV_DIM)],
        out_specs=tokq(ATTN_DIM),
        out_shape=jax.ShapeDtypeStruct((batch * seq, ATTN_DIM), BF16),
        scratch_shapes=[
            pltpu.VMEM((seq // kc, qb, kc), F32),
            pltpu.VMEM((N_HEADS * qb, LANES), BF16),
            pltpu.VMEM((N_IDX_HEADS * qb, LANES), BF16),
            pltpu.VMEM((N_HEADS * qb, kc), F32),
            pltpu.VMEM((N_HEADS * qb, kc), F32),
            pltpu.VMEM((N_HEADS * qb, kc), BF16),
            pltpu.VMEM((qb, kc), F32),
            pltpu.VMEM((N_HEADS * qb, LANES), F32),
            pltpu.VMEM((N_HEADS * qb, LANES), F32),
            pltpu.VMEM((N_HEADS * qb, 2 * KV_DIM), F32),
        ],
        compiler_params=_cparams(("arbitrary", "arbitrary")),
        name="attn_prompt",
    )(q, iq, ikw, as_seq(ik2), as_seq(kb), as_seq(vb))


def _stack_heads(x_bf, n_heads):
    x = x_bf.astype(F32)
    return jnp.concatenate([x[:, h * HEAD_DIM:(h + 1) * HEAD_DIM] for h in range(n_heads)], axis=0).astype(BF16)


def _weighted_relu_sum(logits, wcol, t):
    sc = wcol[0] * jnp.maximum(logits[0:t], 0.0)
    for h in range(1, N_IDX_HEADS):
        sc = sc + wcol[h] * jnp.maximum(logits[h * t:(h + 1) * t], 0.0)
    return sc


def _head_weights(w_ref):
    wq = w_ref[...]
    return [wq[:, IDX_DIM + h:IDX_DIM + h + 1] * (N_IDX_HEADS ** -0.5) for h in range(N_IDX_HEADS)]


def _sample_scores_body(pt_ref, iq_ref, w_ref, *refs, pg, t):
    del pt_ref
    pages = refs[:pg]
    keys_ref, nkeys_ref, ikt_ref = refs[pg:]
    iqs = _stack_heads(iq_ref[...], N_IDX_HEADS)
    wcol = _head_weights(w_ref)
    for j in range(pg):
        ikt_ref[:, j * PAGE_SIZE:(j + 1) * PAGE_SIZE] = pages[j][...].astype(BF16)
    logits = _dot(iqs, ikt_ref[...])
    keys_ref[...] = _weighted_relu_sum(logits, wcol, t)

    @pl.when(pl.program_id(1) == 0)
    def _():
        ikn = jnp.concatenate([w_ref[...][:, 0:IDX_DIM], jnp.zeros((LANES - t, IDX_DIM), F32)], axis=0)
        sc = _weighted_relu_sum(_dot_t(iqs, ikn.astype(BF16)), wcol, t)
        col_n = lax.broadcasted_iota(I32, (t, LANES), 1)
        row_n = lax.broadcasted_iota(I32, (t, LANES), 0)
        nkeys_ref[...] = jnp.where(col_n <= row_n, sc, MASKED_SCORE)


def _sample_scores(page_table, iq, ikw, cache_idx_t, db, t, pg):
    n_pages = page_table.shape[1]
    page_spec = lambda j: pl.BlockSpec((None, IDX_DIM, PAGE_SIZE),
                                       lambda b, g, pt: (pt[b, g * pg + j], 0, 0))
    grid_spec = pltpu.PrefetchScalarGridSpec(
        num_scalar_prefetch=1,
        grid=(db, n_pages // pg),
        in_specs=[pl.BlockSpec((t, N_IDX_HEADS * IDX_DIM), lambda b, g, pt: (b, 0)),
                  pl.BlockSpec((t, LANES), lambda b, g, pt: (b, 0))] + [page_spec(j) for j in range(pg)],
        out_specs=[pl.BlockSpec((t, pg * PAGE_SIZE), lambda b, g, pt: (b, g)),
                   pl.BlockSpec((t, LANES), lambda b, g, pt: (b, 0))],
        scratch_shapes=[pltpu.VMEM((IDX_DIM, pg * PAGE_SIZE), BF16)],
    )
    return pl.pallas_call(
        functools.partial(_sample_scores_body, pg=pg, t=t),
        grid_spec=grid_spec,
        out_shape=[jax.ShapeDtypeStruct((db * t, n_pages * PAGE_SIZE), F32),
                   jax.ShapeDtypeStruct((db * t, LANES), F32)],
        compiler_params=_cparams(("arbitrary", "arbitrary")),
        name="sample_scores",
    )(page_table, iq, ikw, *([cache_idx_t] * pg))


def _sample_thr_body(keys_ref, nkeys_ref, thr_ref, cut_ref, *, rows, topk, idx_bits, cw):
    past = keys_ref.shape[-1]
    col_c = lax.broadcasted_iota(I32, (rows, cw), 1)
    col_n = lax.broadcasted_iota(I32, (rows, LANES), 1)

    def count_keys(pred):
        acc = jnp.where(pred(nkeys_ref[...]), 1.0, 0.0)
        for c in range(past // cw):
            acc = acc + _lane_fold(jnp.where(pred(keys_ref[:, c * cw:(c + 1) * cw]), 1.0, 0.0))
        return jnp.sum(acc, axis=1, keepdims=True)

    def count_keys_cols(pred):
        acc = jnp.where(pred(nkeys_ref[...], past + col_n), 1.0, 0.0)
        for c in range(past // cw):
            acc = acc + _lane_fold(jnp.where(pred(keys_ref[:, c * cw:(c + 1) * cw], c * cw + col_c), 1.0, 0.0))
        return jnp.sum(acc, axis=1, keepdims=True)

    thr = _select_threshold(lambda cand: count_keys(lambda kk: kk >= cand), rows, float(topk))
    thr = jnp.maximum(thr, MIN_THRESHOLD)
    over = count_keys(lambda kk: kk >= thr) > float(topk)
    thr_ref[...] = jnp.broadcast_to(thr, (rows, LANES))
    cut_ref[...] = jnp.full((rows, LANES), 1 << idx_bits, I32)

    @pl.when(jnp.max(jnp.where(over, 1.0, 0.0)) > 0.0)
    def _():
        need = float(topk) - count_keys(lambda kk: kk > thr)
        cut = _tie_cutoff(count_keys_cols, thr, need, idx_bits)
        cut_ref[...] = jnp.broadcast_to(jnp.where(over, cut, jnp.int32(1 << idx_bits)), (rows, LANES))


def _sample_thr(keys, nkeys, t, rows):
    n, past = keys.shape
    topk = min(TOPK_MAX, (past + t) // 4)
    blk = lambda width: pl.BlockSpec((rows, width), lambda i: (i, 0))
    return pl.pallas_call(
        functools.partial(_sample_thr_body, rows=rows, topk=topk,
                          idx_bits=(past + LANES - 1).bit_length(), cw=2048),
        grid=(n // rows,),
        in_specs=[blk(past), blk(LANES)],
        out_specs=[blk(LANES), blk(LANES)],
        out_shape=[jax.ShapeDtypeStruct((n, LANES), F32), jax.ShapeDtypeStruct((n, LANES), I32)],
        compiler_params=_cparams(("arbitrary",)),
        name="sample_thr",
    )(keys, nkeys)


def _sample_attn_body(pt_ref, keys_ref, nkeys_ref, thr_ref, cut_ref, q_ref, kn_ref, vn_ref, *refs,
                      pg, t, past):
    del pt_ref
    kpages, vpages = refs[:pg], refs[pg:2 * pg]
    o_ref = refs[2 * pg]
    qs_ref, kt_ref, vt_ref, m_ref, l_ref, acc_ref = refs[2 * pg + 1:]
    g = pl.program_id(1)
    rows = N_HEADS * t

    def flash_update(s, sel, pv):
        n = s.shape[-1]
        s = jnp.where(sel[None], s.reshape(N_HEADS, t, n), NEG_BIG).reshape(rows, n)
        m_old = m_ref[...]
        m_new = jnp.maximum(m_old, jnp.max(s, axis=1, keepdims=True))
        alpha = jnp.exp2(m_old - m_new)
        p = jnp.exp2(s - m_new)
        l_ref[...] = alpha * l_ref[...] + jnp.sum(p, axis=1, keepdims=True)
        acc_ref[...] = alpha * acc_ref[...] + pv(p.astype(BF16))
        m_ref[...] = m_new

    @pl.when(g == 0)
    def _():
        lo = lax.broadcasted_iota(I32, (t, LANES), 1) < HEAD_DIM
        for c in range(N_HEADS // 2):
            qc = q_ref[:, c * LANES:(c + 1) * LANES]
            qs_ref[c * t:(c + 1) * t, :] = jnp.where(lo, qc, jnp.zeros_like(qc))
            qs_ref[(c + 4) * t:(c + 5) * t, :] = jnp.where(lo, jnp.zeros_like(qc), qc)
        m_ref[...] = jnp.full(m_ref.shape, NEG_BIG, F32)
        l_ref[...] = jnp.zeros(l_ref.shape, F32)
        acc_ref[...] = jnp.zeros(acc_ref.shape, F32)

    thr = thr_ref[:, 0:1]
    cut = cut_ref[:, 0:1]

    def selected(kk, col):
        return (kk > thr) | ((kk == thr) & (col < cut))

    qs = qs_ref[...]
    width = pg * PAGE_SIZE
    for j in range(pg):
        kt_ref[:, j * PAGE_SIZE:(j + 1) * PAGE_SIZE] = kpages[j][...].astype(BF16)
        vt_ref[:, j * PAGE_SIZE:(j + 1) * PAGE_SIZE] = vpages[j][...].astype(BF16)
    col_w = g * width + lax.broadcasted_iota(I32, (t, width), 1)
    flash_update(_dot(qs, kt_ref[...]), selected(keys_ref[...], col_w),
                 lambda p: _dot_t(p, vt_ref[...]))

    @pl.when(g == pl.num_programs(1) - 1)
    def _():
        pad = jnp.zeros((LANES - t, LANES), F32)
        kn = jnp.concatenate([kn_ref[...], pad], axis=0).astype(BF16)
        vn = jnp.concatenate([vn_ref[...], pad], axis=0).astype(BF16)
        col_n = past + lax.broadcasted_iota(I32, (t, LANES), 1)
        flash_update(_dot_t(qs, kn), selected(nkeys_ref[...], col_n), lambda p: _dot(p, vn))
        lo = lax.broadcasted_iota(I32, (t, LANES), 1) < HEAD_DIM
        out = acc_ref[...] / l_ref[...]
        for c in range(N_HEADS // 2):
            oc = jnp.where(lo, out[c * t:(c + 1) * t], out[(c + 4) * t:(c + 5) * t])
            o_ref[:, c * LANES:(c + 1) * LANES] = oc.astype(BF16)


def _sample_attn(page_table, keys, nkeys, thr, cut, q, k_new, v_new, cache_kt, cache_vt, db, t, pg):
    n_pages = page_table.shape[1]
    past = n_pages * PAGE_SIZE
    page_spec = lambda j: pl.BlockSpec((None, KV_DIM, PAGE_SIZE),
                                       lambda b, g, pt: (pt[b, g * pg + j], 0, 0))
    tok = lambda width: pl.BlockSpec((t, width), lambda b, g, pt: (b, 0))
    grid_spec = pltpu.PrefetchScalarGridSpec(
        num_scalar_prefetch=1,
        grid=(db, n_pages // pg),
        in_specs=[pl.BlockSpec((t, pg * PAGE_SIZE), lambda b, g, pt: (b, g)),
                  tok(LANES), tok(LANES), tok(LANES), tok(ATTN_DIM), tok(KV_DIM), tok(KV_DIM)]
                 + [page_spec(j) for j in range(pg)] + [page_spec(j) for j in range(pg)],
        out_specs=tok(ATTN_DIM),
        scratch_shapes=[
            pltpu.VMEM((N_HEADS * t, LANES), BF16),
            pltpu.VMEM((KV_DIM, pg * PAGE_SIZE), BF16),
            pltpu.VMEM((KV_DIM, pg * PAGE_SIZE), BF16),
            pltpu.VMEM((N_HEADS * t, 1), F32),
            pltpu.VMEM((N_HEADS * t, 1), F32),
            pltpu.VMEM((N_HEADS * t, LANES), F32),
        ],
    )
    return pl.pallas_call(
        functools.partial(_sample_attn_body, pg=pg, t=t, past=past),
        grid_spec=grid_spec,
        out_shape=jax.ShapeDtypeStruct((db * t, ATTN_DIM), BF16),
        compiler_params=_cparams(("arbitrary", "arbitrary")),
        name="sample_attn",
    )(page_table, keys, nkeys, thr, cut, q, k_new, v_new, *([cache_kt] * pg), *([cache_vt] * pg))


def _out_body(x_ref, a_ref, c_ref, wa_ref, wc_ref, o_ref):
    o_ref[...] = x_ref[...] + _dot(a_ref[...], wa_ref[...]) + _dot(c_ref[...], wc_ref[...])


def _out_proj(x, attn, conv, wa, wc, tm):
    n = x.shape[0]
    tok = lambda width: pl.BlockSpec((tm, width), lambda i: (i, 0))
    full = lambda a: pl.BlockSpec(a.shape, lambda i: (0, 0))
    return pl.pallas_call(
        _out_body,
        grid=(n // tm,),
        in_specs=[tok(D_MODEL), tok(ATTN_DIM), tok(CONV_DIM), full(wa), full(wc)],
        out_specs=tok(D_MODEL),
        out_shape=jax.ShapeDtypeStruct((n, D_MODEL), F32),
        compiler_params=_cparams(("arbitrary",)),
        name="out_proj",
    )(x, attn, conv, wa, wc)


def _ple_body(x_ref, pe_ref, g_ref, wg_ref, wp_ref, gf_ref, o_ref):
    x = x_ref[...]
    gate = jax.nn.sigmoid(_dot(_rms(x, g_ref[...]).astype(BF16), wg_ref[...]))
    x = x + gate * _dot(pe_ref[...].astype(BF16), wp_ref[...])
    o_ref[...] = _rms(x, gf_ref[...])


def _ple_final(x, pe, g, wg, wp, gf, tm):
    n = x.shape[0]
    tok = lambda width: pl.BlockSpec((tm, width), lambda i: (i, 0))
    full = lambda a: pl.BlockSpec(a.shape, lambda i: (0, 0))
    return pl.pallas_call(
        _ple_body,
        grid=(n // tm,),
        in_specs=[tok(D_MODEL), tok(PLE_DIM), full(g), full(wg), full(wp), full(gf)],
        out_specs=tok(D_MODEL),
        out_shape=jax.ShapeDtypeStruct((n, D_MODEL), F32),
        compiler_params=_cparams(("arbitrary",)),
        name="ple_final",
    )(x, pe, g, wg, wp, gf)


def _rope_tables(pos):
    half = HEAD_DIM // 2
    inv = ROPE_THETA ** (-jnp.arange(half, dtype=F32) / half)
    ang = pos.astype(F32)[:, None] * inv[None, :]
    cos, sin = jnp.cos(ang), jnp.sin(ang)
    cos2 = jnp.concatenate([cos, cos], axis=1)
    sin2 = jnp.concatenate([-sin, sin], axis=1)
    cos128 = jnp.concatenate([cos2, cos2], axis=1)
    sin128 = jnp.concatenate([sin2, sin2], axis=1)
    cosk = jnp.concatenate([cos2, jnp.ones_like(cos2)], axis=1)
    sink = jnp.concatenate([sin2, jnp.zeros_like(sin2)], axis=1)
    return cos128, sin128, cosk, sink


def _prep_w_in(w_in):
    d = w_in.shape[0]
    q = w_in[:, :ATTN_DIM].reshape(d, N_HEADS, HEAD_DIM)[:, jnp.array(_HEAD_PERM)].reshape(d, ATTN_DIM)
    o = ATTN_DIM
    kv = w_in[:, o:o + 2 * KV_DIM]
    o += 2 * KV_DIM
    iq = w_in[:, o:o + N_IDX_HEADS * IDX_DIM]
    o += N_IDX_HEADS * IDX_DIM
    ikw = w_in[:, o:o + IDX_DIM + N_IDX_HEADS]
    o += IDX_DIM + N_IDX_HEADS
    ikw = jnp.pad(ikw, ((0, 0), (0, LANES - IDX_DIM - N_IDX_HEADS)))
    conv = w_in[:, o:]
    return jnp.concatenate([q, kv, iq, ikw, conv], axis=1).astype(BF16)


def _layer(x, pe, mixer, tm, g_ffn1, w1g, w1u, w1d, g_mix, w_in, conv_w, wo_a, wo_c,
           g_ffn2, w2g, w2u, w2d, g_ple, w_pg, w_pp, g_final):
    x1 = _ffn(x, g_ffn1, w1g, w1u, w1d, tm)
    attn, conv, state = mixer(x1, g_mix, w_in, conv_w)
    x2 = _out_proj(x1, attn, conv, wo_a, wo_c, tm)
    x3 = _ffn(x2, g_ffn2, w2g, w2u, w2d, tm)
    return _ple_final(x3, pe, g_ple, w_pg, w_pp, g_final, tm), state


def kernel(x_prompt, x_sample, cache_k, cache_v, cache_idx_k, state_conv, page_table, p_prompt, p_sample,
           g_ffn1, w1_gate, w1_up, w1_down, g_mix, w_in, conv_w, w_out,
           g_ffn2, w2_gate, w2_up, w2_down, g_ple, w_ple_gate, w_ple_proj, g_final):
    depth = w_in.shape[0]
    assert depth == 1, "single-layer step"
    batch, seq, _ = x_prompt.shape
    db, t, _ = x_sample.shape
    n_pages = page_table.shape[1]
    past = n_pages * PAGE_SIZE
    assert t == SUBLANES and seq % 512 == 0

    row = lambda gvec: gvec.reshape(1, -1)
    perm = jnp.array(_HEAD_PERM)
    wo_a = w_out[0, :ATTN_DIM].reshape(N_HEADS, HEAD_DIM, D_MODEL)[perm].reshape(ATTN_DIM, D_MODEL).astype(BF16)
    wo_c = w_out[0, ATTN_DIM:].astype(BF16)
    shared = (row(g_ffn1[0]), w1_gate[0].astype(BF16), w1_up[0].astype(BF16), w1_down[0].astype(BF16),
              row(g_mix[0]), _prep_w_in(w_in[0]), conv_w[0], wo_a, wo_c,
              row(g_ffn2[0]), w2_gate[0].astype(BF16), w2_up[0].astype(BF16), w2_down[0].astype(BF16),
              row(g_ple[0]), w_ple_gate[0].astype(BF16), w_ple_proj[0].astype(BF16), row(g_final))

    tm_p = 512
    tabs_p = _rope_tables(jnp.arange(seq, dtype=I32))

    def prompt_mixer(x1, g, w, cw):
        q, iq, k, v, ikw, kb, vb, ik2, cy, tail = _proj(x1, g, w, *tabs_p, cw, tm_p, seq // tm_p)
        attn = _attn_prompt(q, iq, ikw, ik2, kb, vb, batch, seq, qb=256, kc=512)
        return attn, cy, (k, v, ikw, tail)

    yp, (kp, vp, ikwp, tailp) = _layer(x_prompt.reshape(batch * seq, D_MODEL),
                                       p_prompt[0].reshape(batch * seq, PLE_DIM), prompt_mixer, tm_p, *shared)

    n_s = db * t
    tabs_s = tuple(jnp.tile(a, (db, 1)) for a in _rope_tables(past + jnp.arange(t, dtype=I32)))
    buf = state_conv[0]
    zero = jnp.zeros((db, t - 2, CONV_DIM), F32)
    halo1 = jnp.concatenate([buf[:, 1:2], jnp.zeros((db, t - 1, CONV_DIM), F32)], axis=1).reshape(n_s, CONV_DIM)
    halo2 = jnp.concatenate([buf, zero], axis=1).reshape(n_s, CONV_DIM)
    idx_t = jnp.transpose(cache_idx_k[0], (0, 2, 1))
    k_t = jnp.transpose(cache_k[0], (0, 2, 3, 1)).reshape(-1, KV_DIM, PAGE_SIZE)
    v_t = jnp.transpose(cache_v[0], (0, 2, 3, 1)).reshape(-1, KV_DIM, PAGE_SIZE)

    def sample_mixer(x1, g, w, cw):
        q, iq, k, v, ikw, kb, vb, ik2, cy, u = _proj(x1, g, w, *tabs_s, cw, n_s, 1, halos=(halo1, halo2))
        keys, nkeys = _sample_scores(page_table, iq, ikw, idx_t, db, t, pg=min(32, n_pages))
        thr, cut = _sample_thr(keys, nkeys, t, rows=min(64, n_s))
        attn = _sample_attn(page_table, keys, nkeys, thr, cut, q, k, v, k_t, v_t, db, t, pg=min(16, n_pages))
        return attn, cy, (k, v, ikw, u)

    ys, (ks, vs, ikws, us) = _layer(x_sample.reshape(n_s, D_MODEL), p_sample[0].reshape(n_s, PLE_DIM),
                                    sample_mixer, n_s, *shared)

    return (yp.reshape(batch, seq, D_MODEL),
            ys.reshape(db, t, D_MODEL),
            kp.reshape(1, batch, seq, N_KV_HEADS, HEAD_DIM),
            vp.reshape(1, batch, seq, N_KV_HEADS, HEAD_DIM),
            ikwp[:, :IDX_DIM].reshape(1, batch, seq, IDX_DIM),
            tailp[:, SUBLANES - (CONV_WIDTH - 1):][None],
            ks.reshape(1, db, t, N_KV_HEADS, HEAD_DIM),
            vs.reshape(1, db, t, N_KV_HEADS, HEAD_DIM),
            ikws[:, :IDX_DIM].reshape(1, db, t, IDX_DIM),
            us.reshape(db, t, CONV_DIM)[:, t - (CONV_WIDTH - 1):][None])
```

```python
import functools

import jax
import jax.numpy as jnp
from jax import lax
from jax.experimental import pallas as pl
from jax.experimental.pallas import tpu as pltpu

F32 = jnp.float32
BF16 = jnp.bfloat16
I32 = jnp.int32

D_MODEL = 1024
N_HEADS = 8
HEAD_DIM = 64
N_KV_HEADS = 2
ATTN_DIM = N_HEADS * HEAD_DIM
KV_DIM = N_KV_HEADS * HEAD_DIM
N_IDX_HEADS = 4
IDX_DIM = 64
TOPK_MAX = 256
CONV_DIM = D_MODEL - ATTN_DIM
CONV_WIDTH = 3
D_FF = 2816
PLE_DIM = 256
PAGE_SIZE = 128
ROPE_THETA = 10000.0
RMS_EPS = 1e-6

ROW_TILE = 64
LANES = 128
SUBLANES = 8
VMEM_LIMIT = 56 * 1024 * 1024

INT_MIN = -(2 ** 31)
MASKED_SCORE = -1e38
MIN_THRESHOLD = -5e37
NEG_BIG = -1e30
LOG2E = 1.4426950408889634

_Q0, _K0, _V0, _IQ0, _IK0, _CB0, _CC0, _CH0, _PROJ_W = 0, 512, 640, 768, 1024, 1152, 1664, 2176, 2688
_HEAD_PERM = (0, 4, 1, 5, 2, 6, 3, 7)


def _cparams(sem):
    return pltpu.CompilerParams(dimension_semantics=sem, vmem_limit_bytes=VMEM_LIMIT)


def _dot(a, b):
    return jnp.dot(a, b, preferred_element_type=F32)


def _dot_t(a, b):
    return lax.dot_general(a, b, (((1,), (1,)), ((), ())), preferred_element_type=F32)


def _rms(x, g):
    ms = jnp.mean(x * x, axis=-1, keepdims=True)
    return x * lax.rsqrt(ms + RMS_EPS) * g


def _ffn_body(x_ref, g_ref, wg_ref, wu_ref, wd_ref, o_ref, h_ref, acc_ref):
    j = pl.program_id(1)

    @pl.when(j == 0)
    def _():
        h_ref[...] = _rms(x_ref[...], g_ref[...]).astype(BF16)

    h = h_ref[...]
    gate = _dot(h, wg_ref[...])
    up = _dot(h, wu_ref[...])
    act = (gate * jax.nn.sigmoid(gate) * up).astype(BF16)
    part = _dot(act, wd_ref[...])

    @pl.when(j == 0)
    def _():
        acc_ref[...] = part

    @pl.when(j > 0)
    def _():
        acc_ref[...] += part

    @pl.when(j == pl.num_programs(1) - 1)
    def _():
        o_ref[...] = x_ref[...] + 0.5 * acc_ref[...]


def _ffn(x, g, wg, wu, wd, tm):
    n = x.shape[0]
    ff_chunk = D_FF // 2
    return pl.pallas_call(
        _ffn_body,
        grid=(n // tm, D_FF // ff_chunk),
        in_specs=[
            pl.BlockSpec((tm, D_MODEL), lambda i, j: (i, 0)),
            pl.BlockSpec((1, D_MODEL), lambda i, j: (0, 0)),
            pl.BlockSpec((D_MODEL, ff_chunk), lambda i, j: (0, j)),
            pl.BlockSpec((D_MODEL, ff_chunk), lambda i, j: (0, j)),
            pl.BlockSpec((ff_chunk, D_MODEL), lambda i, j: (j, 0)),
        ],
        out_specs=pl.BlockSpec((tm, D_MODEL), lambda i, j: (i, 0)),
        out_shape=jax.ShapeDtypeStruct((n, D_MODEL), F32),
        scratch_shapes=[pltpu.VMEM((tm, D_MODEL), BF16), pltpu.VMEM((tm, D_MODEL), F32)],
        compiler_params=_cparams(("arbitrary", "arbitrary")),
        name="ffn",
    )(x, g, wg, wu, wd)


def _swap32(x):
    w = x.shape[-1]
    lane = lax.broadcasted_iota(I32, x.shape, 1)
    return jnp.where((lane & 63) < 32, pltpu.roll(x, w - 32, 1), pltpu.roll(x, 32, 1))


def _rope_cols(z, cos, sin):
    cols = []
    for c in range(z.shape[-1] // LANES):
        zc = z[:, c * LANES:(c + 1) * LANES]
        cols.append(zc * cos + _swap32(zc) * sin)
    return cols[0] if len(cols) == 1 else jnp.concatenate(cols, axis=1)


def _proj_body(*refs, tm, tiles_per_seq, halo):
    if halo:
        (x_ref, g_ref, w_ref, cos_ref, sin_ref, cosk_ref, sink_ref, cw_ref, h1_ref, h2_ref,
         q_ref, iq_ref, k_ref, v_ref, ikw_ref, kb_ref, vb_ref, ik2_ref, cy_ref, u_ref) = refs
    else:
        (x_ref, g_ref, w_ref, cos_ref, sin_ref, cosk_ref, sink_ref, cw_ref,
         q_ref, iq_ref, k_ref, v_ref, ikw_ref, kb_ref, vb_ref, ik2_ref, cy_ref, u_ref, carry_ref) = refs

    h = _rms(x_ref[...], g_ref[...]).astype(BF16)
    cos, sin = cos_ref[...], sin_ref[...]

    zq = _dot(h, w_ref[:, _Q0:_K0])
    q_ref[...] = (_rope_cols(zq, cos, sin) * (HEAD_DIM ** -0.5 * LOG2E)).astype(BF16)
    kr = _rope_cols(_dot(h, w_ref[:, _K0:_V0]), cos, sin)
    k_ref[...] = kr
    kb_ref[...] = kr.astype(BF16)
    zv = _dot(h, w_ref[:, _V0:_IQ0])
    v_ref[...] = zv
    vb_ref[...] = jnp.concatenate([zv, jnp.ones_like(zv)], axis=1).astype(BF16)
    ziq = _dot(h, w_ref[:, _IQ0:_IK0])
    iq_ref[...] = (_rope_cols(ziq, cos, sin) * (IDX_DIM ** -0.5)).astype(BF16)
    ikr = _rope_cols(_dot(h, w_ref[:, _IK0:_CB0]), cosk_ref[...], sink_ref[...])
    ikw_ref[...] = ikr
    lane = lax.broadcasted_iota(I32, ikr.shape, 1)
    ik2_ref[...] = jnp.where(lane < IDX_DIM, ikr, pltpu.roll(ikr, IDX_DIM, 1)).astype(BF16)

    cb = _dot(h, w_ref[:, _CB0:_CC0])
    u = _dot(h, w_ref[:, _CC0:_CH0]) * _dot(h, w_ref[:, _CH0:_PROJ_W])
    row = lax.broadcasted_iota(I32, u.shape, 0)
    r1 = pltpu.roll(u, 1, 0)
    r2 = pltpu.roll(u, 2, 0)
    if halo:
        t = row & (SUBLANES - 1)
        us1 = jnp.where(t == 0, h1_ref[...], r1)
        us2 = jnp.where(t < 2, h2_ref[...], r2)
        u_ref[...] = u
    else:
        @pl.when(pl.program_id(0) % tiles_per_seq == 0)
        def _():
            carry_ref[...] = jnp.zeros_like(carry_ref)

        c0 = carry_ref[SUBLANES - 2:SUBLANES - 1, :]
        c1 = carry_ref[SUBLANES - 1:SUBLANES, :]
        us1 = jnp.where(row == 0, c1, r1)
        us2 = jnp.where(row == 0, c0, jnp.where(row == 1, c1, r2))
        tail = u[tm - SUBLANES:tm, :]
        carry_ref[...] = tail
        u_ref[0] = tail
    cw = cw_ref[...]
    y = cb * (cw[0:1, :] * us2 + cw[1:2, :] * us1 + cw[2:3, :] * u)
    cy_ref[...] = y.astype(BF16)


def _proj(x, g, w, cos, sin, cosk, sink, cw, tm, tiles_per_seq, halos=None):
    n = x.shape[0]
    n_tiles = n // tm
    halo = halos is not None
    tok = lambda width: pl.BlockSpec((tm, width), lambda i: (i, 0))
    tab = pl.BlockSpec((tm, LANES), lambda i: (i % tiles_per_seq, 0))
    full = lambda a: pl.BlockSpec(a.shape, lambda i: (0,) * a.ndim)
    in_specs = [tok(D_MODEL), full(g), full(w), tab, tab, tab, tab, full(cw)]
    args = [x, g, w, cos, sin, cosk, sink, cw]
    out_shape = [
        jax.ShapeDtypeStruct((n, ATTN_DIM), BF16),
        jax.ShapeDtypeStruct((n, N_IDX_HEADS * IDX_DIM), BF16),
        jax.ShapeDtypeStruct((n, KV_DIM), F32),
        jax.ShapeDtypeStruct((n, KV_DIM), F32),
        jax.ShapeDtypeStruct((n, LANES), F32),
        jax.ShapeDtypeStruct((n, KV_DIM), BF16),
        jax.ShapeDtypeStruct((n, 2 * KV_DIM), BF16),
        jax.ShapeDtypeStruct((n, LANES), BF16),
        jax.ShapeDtypeStruct((n, CONV_DIM), BF16),
    ]
    out_specs = [tok(ATTN_DIM), tok(N_IDX_HEADS * IDX_DIM), tok(KV_DIM), tok(KV_DIM), tok(LANES),
                 tok(KV_DIM), tok(2 * KV_DIM), tok(LANES), tok(CONV_DIM)]
    scratch = []
    if halo:
        in_specs += [tok(CONV_DIM), tok(CONV_DIM)]
        args += list(halos)
        out_shape.append(jax.ShapeDtypeStruct((n, CONV_DIM), F32))
        out_specs.append(tok(CONV_DIM))
    else:
        n_seq = n_tiles // tiles_per_seq
        out_shape.append(jax.ShapeDtypeStruct((n_seq, SUBLANES, CONV_DIM), F32))
        out_specs.append(pl.BlockSpec((1, SUBLANES, CONV_DIM), lambda i: (i // tiles_per_seq, 0, 0)))
        scratch.append(pltpu.VMEM((SUBLANES, CONV_DIM), F32))
    return pl.pallas_call(
        functools.partial(_proj_body, tm=tm, tiles_per_seq=tiles_per_seq, halo=halo),
        grid=(n_tiles,),
        in_specs=in_specs,
        out_specs=out_specs,
        out_shape=out_shape,
        scratch_shapes=scratch,
        compiler_params=_cparams(("arbitrary",)),
        name="proj_sample" if halo else "proj_prompt",
    )(*args)


def _lane_fold(ind):
    acc = ind[:, 0:LANES]
    for j in range(1, ind.shape[-1] // LANES):
        acc = acc + ind[:, j * LANES:(j + 1) * LANES]
    return acc


def _key_to_f32(key):
    return lax.bitcast_convert_type(key ^ ((key >> 31) & 0x7FFFFFFF), F32)


def _select_threshold(count_ge, rows, topk):
    def step(b, t):
        cand = t + lax.shift_left(jnp.int32(1), jnp.int32(31) - jnp.asarray(b, I32))
        return jnp.where(count_ge(_key_to_f32(cand)) >= topk, cand, t)

    return _key_to_f32(lax.fori_loop(0, 32, step, jnp.full((rows, 1), INT_MIN, I32)))


def _tie_cutoff(count_fn, t, need, idx_bits):
    def step(b, c):
        cand = c + lax.shift_left(jnp.int32(1), jnp.int32(idx_bits - 1) - jnp.asarray(b, I32))
        cnt = count_fn(lambda s, col: (s == t) & (col < cand))
        return jnp.where(cnt <= need, cand, c)

    return lax.fori_loop(0, idx_bits, step, jnp.zeros(t.shape, I32))


def _attn_prompt_body(q_ref, iq_ref, w_ref, ik2_ref, kb_ref, vb_ref, o_ref,
                      sc_ref, qs_ref, iqs_ref, s0_ref, s1_ref, p_ref, bias_ref,
                      m_ref, a_ref, acc_ref,
                      *, qb, kc, topk, idx_bits, n_chunks):
    i = (n_chunks - 1) * (kc // qb) + pl.program_id(1)
    lo = lax.broadcasted_iota(I32, (qb, LANES), 1) < HEAD_DIM

    for c in range(N_HEADS // 2):
        qc = q_ref[:, c * LANES:(c + 1) * LANES]
        qs_ref[c * qb:(c + 1) * qb, :] = jnp.where(lo, qc, jnp.zeros_like(qc))
        qs_ref[(c + 4) * qb:(c + 5) * qb, :] = jnp.where(lo, jnp.zeros_like(qc), qc)
    for c in range(N_IDX_HEADS // 2):
        ic = iq_ref[:, c * LANES:(c + 1) * LANES]
        iqs_ref[(2 * c) * qb:(2 * c + 1) * qb, :] = jnp.where(lo, ic, jnp.zeros_like(ic))
        iqs_ref[(2 * c + 1) * qb:(2 * c + 2) * qb, :] = jnp.where(lo, jnp.zeros_like(ic), ic)

    row_t = lax.broadcasted_iota(I32, (ROW_TILE, kc), 0)
    col_t = lax.broadcasted_iota(I32, (ROW_TILE, kc), 1)
    col_l = lax.broadcasted_iota(I32, (qb, kc), 1)

    def score_chunk(c, carry):
        off = pl.multiple_of(c * kc, kc)
        s0_ref[0:N_IDX_HEADS * qb, :] = _dot_t(iqs_ref[...], ik2_ref[pl.ds(off, kc), :])
        for r0 in range(0, qb, ROW_TILE):
            r = slice(r0, r0 + ROW_TILE)
            wt = w_ref[r, :] * (N_IDX_HEADS ** -0.5)
            sc = wt[:, IDX_DIM:IDX_DIM + 1] * jnp.maximum(s0_ref[r, :], 0.0)
            for h in range(1, N_IDX_HEADS):
                sc = sc + (wt[:, IDX_DIM + h:IDX_DIM + h + 1]
                           * jnp.maximum(s0_ref[h * qb + r0:h * qb + r0 + ROW_TILE, :], 0.0))
            sc_ref[c, r, :] = jnp.where(off + col_t <= i * qb + r0 + row_t, sc, MASKED_SCORE)
        return carry

    lax.fori_loop(0, n_chunks, score_chunk, 0)

    def kth_largest_score(n):
        def count_ge(cand):
            cw = jnp.concatenate([jnp.broadcast_to(cand, (qb, LANES))] * (kc // LANES), axis=1)
            acc = jnp.zeros((qb, LANES), F32)
            for c in range(n):
                acc = acc + _lane_fold(jnp.where(sc_ref[c] >= cw, 1.0, 0.0))
            return jnp.sum(acc, axis=1, keepdims=True)

        return _select_threshold(count_ge, qb, float(topk))

    def count_scores(pred):
        def body(c, acc):
            return acc + _lane_fold(jnp.where(pred(sc_ref[c]), 1.0, 0.0))
        acc = lax.fori_loop(0, n_chunks, body, jnp.zeros((qb, LANES), F32))
        return jnp.sum(acc, axis=1, keepdims=True)

    def count_scores_cols(pred):
        def body(c, acc):
            return acc + _lane_fold(jnp.where(pred(sc_ref[c], c * kc + col_l), 1.0, 0.0))
        acc = lax.fori_loop(0, n_chunks, body, jnp.zeros((qb, LANES), F32))
        return jnp.sum(acc, axis=1, keepdims=True)

    thr = kth_largest_score(n_chunks)
    thr = jnp.maximum(thr, MIN_THRESHOLD)
    over = count_scores(lambda s: s >= thr) > float(topk)

    @pl.when(jnp.max(jnp.where(over, 1.0, 0.0)) > 0.0)
    def _():
        need = float(topk) - count_scores(lambda s: s > thr)
        cut = _tie_cutoff(count_scores_cols, thr, need, idx_bits)
        cut = jnp.where(over, cut, jnp.int32(1 << idx_bits))

        def drop(c, carry):
            s = sc_ref[c]
            sc_ref[c] = jnp.where((s == thr) & (c * kc + col_l >= cut), MASKED_SCORE, s)
            return carry

        lax.fori_loop(0, n_chunks, drop, 0)

    m_ref[...] = jnp.full(m_ref.shape, NEG_BIG, F32)
    acc_ref[...] = jnp.zeros(acc_ref.shape, F32)

    last = n_chunks - 1

    def logits(c):
        off = pl.multiple_of(jnp.minimum(c, last) * kc, kc)
        return _dot_t(qs_ref[...], kb_ref[pl.ds(off, kc), :])

    def softmax_pv(c, s_ref):
        cc = jnp.minimum(c, last)
        off = pl.multiple_of(cc * kc, kc)
        bias_ref[...] = jnp.where(c <= last, jnp.where(sc_ref[cc] >= thr, 0.0, NEG_BIG), NEG_BIG)
        for r0 in range(0, N_HEADS * qb, ROW_TILE):
            r = slice(r0, r0 + ROW_TILE)
            s = s_ref[r, :] + bias_ref[r0 % qb:r0 % qb + ROW_TILE, :]
            m_old = m_ref[r, :]
            m_new = jnp.maximum(m_old, jnp.max(s, axis=1, keepdims=True))
            p_ref[r, :] = jnp.exp2(s - jnp.concatenate([m_new] * (kc // LANES), axis=1)).astype(BF16)
            a_ref[r, :] = jnp.exp2(m_old - m_new)
            m_ref[r, :] = m_new
        alpha = a_ref[...]
        acc_ref[...] = (jnp.concatenate([alpha, alpha], axis=1) * acc_ref[...]
                        + _dot(p_ref[...], vb_ref[pl.ds(off, kc), :]))

    s0_ref[...] = logits(0)

    def chunk_pair(j, carry):
        c0 = 2 * j
        s1_ref[...] = logits(c0 + 1)
        softmax_pv(c0, s0_ref)
        s0_ref[...] = logits(c0 + 2)
        softmax_pv(c0 + 1, s1_ref)
        return carry

    lax.fori_loop(0, (n_chunks + 1) // 2, chunk_pair, 0)

    out = acc_ref[:, 0:LANES] / acc_ref[:, LANES:2 * LANES]
    for c in range(N_HEADS // 2):
        oc = jnp.where(lo, out[c * qb:(c + 1) * qb], out[(c + 4) * qb:(c + 5) * qb])
        o_ref[:, c * LANES:(c + 1) * LANES] = oc.astype(BF16)


def _attn_prompt(q, iq, ikw, ik2, kb, vb, batch, seq, qb, kc):
    nqb = seq // qb
    per_call = kc // qb
    topk = min(TOPK_MAX, seq // 4)
    as_seq = lambda a: a.reshape(batch, seq, a.shape[-1])
    ik2, kb, vb = as_seq(ik2), as_seq(kb), as_seq(vb)
    outs = []
    for n in range(1, seq // kc + 1):
        outs.append(_attn_prompt_call(q, iq, ikw, ik2, kb, vb, batch, seq, qb, kc, topk, n, nqb, per_call))
    out = jnp.concatenate([o.reshape(batch, per_call * qb, ATTN_DIM) for o in outs], axis=1)
    return out.reshape(batch * seq, ATTN_DIM)


def _attn_prompt_call(q, iq, ikw, ik2, kb, vb, batch, seq, qb, kc, topk, n, nqb, per_call):
    i0 = (n - 1) * per_call
    tokq = lambda width: pl.BlockSpec((qb, width), lambda b, i: (b * nqb + i0 + i, 0))
    seqk = lambda width: pl.BlockSpec((None, n * kc, width), lambda b, i: (b, 0, 0))
    return pl.pallas_call(
        functools.partial(_attn_prompt_body, qb=qb, kc=kc, topk=topk, idx_bits=(seq - 1).bit_length(),
                          n_chunks=n),
        grid=(batch, per_call),
        in_specs=[tokq(ATTN_DIM), tokq(N_IDX_HEADS * IDX_DIM), tokq(LANES),
                  seqk(LANES), seqk(KV_DIM), seqk(2 * KV_DIM)],
        out_specs=pl.BlockSpec((qb, ATTN_DIM), lambda b, i: (b * per_call + i, 0)),
        out_shape=jax.ShapeDtypeStruct((batch * per_call * qb, ATTN_DIM), BF16),
        scratch_shapes=[
            pltpu.VMEM((n, qb, kc), F32),
            pltpu.VMEM((N_HEADS * qb, LANES), BF16),
            pltpu.VMEM((N_IDX_HEADS * qb, LANES), BF16),
            pltpu.VMEM((N_HEADS * qb, kc), F32),
            pltpu.VMEM((N_HEADS * qb, kc), F32),
            pltpu.VMEM((N_HEADS * qb, kc), BF16),
            pltpu.VMEM((qb, kc), F32),
            pltpu.VMEM((N_HEADS * qb, LANES), F32),
            pltpu.VMEM((N_HEADS * qb, LANES), F32),
            pltpu.VMEM((N_HEADS * qb, 2 * KV_DIM), F32),
        ],
        compiler_params=_cparams(("arbitrary", "arbitrary")),
        name=f"attn_prompt_{n}",
    )(q, iq, ikw, ik2, kb, vb)


def _stack_heads(x_bf, n_heads):
    x = x_bf.astype(F32)
    return jnp.concatenate([x[:, h * HEAD_DIM:(h + 1) * HEAD_DIM] for h in range(n_heads)], axis=0).astype(BF16)


def _weighted_relu_sum(logits, wcol, t):
    sc = wcol[0] * jnp.maximum(logits[0:t], 0.0)
    for h in range(1, N_IDX_HEADS):
        sc = sc + wcol[h] * jnp.maximum(logits[h * t:(h + 1) * t], 0.0)
    return sc


def _head_weights(w_ref):
    wq = w_ref[...]
    return [wq[:, IDX_DIM + h:IDX_DIM + h + 1] * (N_IDX_HEADS ** -0.5) for h in range(N_IDX_HEADS)]


def _sample_scores_body(pt_ref, iq_ref, w_ref, *refs, pg, t):
    del pt_ref
    pages = refs[:pg]
    keys_ref, nkeys_ref, ikt_ref = refs[pg:]
    iqs = _stack_heads(iq_ref[...], N_IDX_HEADS)
    wcol = _head_weights(w_ref)
    for j in range(pg):
        ikt_ref[:, j * PAGE_SIZE:(j + 1) * PAGE_SIZE] = pages[j][...].astype(BF16)
    logits = _dot(iqs, ikt_ref[...])
    keys_ref[...] = _weighted_relu_sum(logits, wcol, t)

    @pl.when(pl.program_id(1) == 0)
    def _():
        ikn = jnp.concatenate([w_ref[...][:, 0:IDX_DIM], jnp.zeros((LANES - t, IDX_DIM), F32)], axis=0)
        sc = _weighted_relu_sum(_dot_t(iqs, ikn.astype(BF16)), wcol, t)
        col_n = lax.broadcasted_iota(I32, (t, LANES), 1)
        row_n = lax.broadcasted_iota(I32, (t, LANES), 0)
        nkeys_ref[...] = jnp.where(col_n <= row_n, sc, MASKED_SCORE)


def _sample_scores(page_table, iq, ikw, cache_idx_t, db, t, pg):
    n_pages = page_table.shape[1]
    page_spec = lambda j: pl.BlockSpec((None, IDX_DIM, PAGE_SIZE),
                                       lambda b, g, pt: (pt[b, g * pg + j], 0, 0))
    grid_spec = pltpu.PrefetchScalarGridSpec(
        num_scalar_prefetch=1,
        grid=(db, n_pages // pg),
        in_specs=[pl.BlockSpec((t, N_IDX_HEADS * IDX_DIM), lambda b, g, pt: (b, 0)),
                  pl.BlockSpec((t, LANES), lambda b, g, pt: (b, 0))] + [page_spec(j) for j in range(pg)],
        out_specs=[pl.BlockSpec((t, pg * PAGE_SIZE), lambda b, g, pt: (b, g)),
                   pl.BlockSpec((t, LANES), lambda b, g, pt: (b, 0))],
        scratch_shapes=[pltpu.VMEM((IDX_DIM, pg * PAGE_SIZE), BF16)],
    )
    return pl.pallas_call(
        functools.partial(_sample_scores_body, pg=pg, t=t),
        grid_spec=grid_spec,
        out_shape=[jax.ShapeDtypeStruct((db * t, n_pages * PAGE_SIZE), F32),
                   jax.ShapeDtypeStruct((db * t, LANES), F32)],
        compiler_params=_cparams(("arbitrary", "arbitrary")),
        name="sample_scores",
    )(page_table, iq, ikw, *([cache_idx_t] * pg))


def _sample_thr_body(keys_ref, nkeys_ref, thr_ref, cut_ref, *, rows, topk, idx_bits, cw):
    past = keys_ref.shape[-1]
    col_c = lax.broadcasted_iota(I32, (rows, cw), 1)
    col_n = lax.broadcasted_iota(I32, (rows, LANES), 1)

    def count_keys(pred):
        acc = jnp.where(pred(nkeys_ref[...]), 1.0, 0.0)
        for c in range(past // cw):
            acc = acc + _lane_fold(jnp.where(pred(keys_ref[:, c * cw:(c + 1) * cw]), 1.0, 0.0))
        return jnp.sum(acc, axis=1, keepdims=True)

    def count_keys_cols(pred):
        acc = jnp.where(pred(nkeys_ref[...], past + col_n), 1.0, 0.0)
        for c in range(past // cw):
            acc = acc + _lane_fold(jnp.where(pred(keys_ref[:, c * cw:(c + 1) * cw], c * cw + col_c), 1.0, 0.0))
        return jnp.sum(acc, axis=1, keepdims=True)

    thr = _select_threshold(lambda cand: count_keys(lambda kk: kk >= cand), rows, float(topk))
    thr = jnp.maximum(thr, MIN_THRESHOLD)
    over = count_keys(lambda kk: kk >= thr) > float(topk)
    thr_ref[...] = jnp.broadcast_to(thr, (rows, LANES))
    cut_ref[...] = jnp.full((rows, LANES), 1 << idx_bits, I32)

    @pl.when(jnp.max(jnp.where(over, 1.0, 0.0)) > 0.0)
    def _():
        need = float(topk) - count_keys(lambda kk: kk > thr)
        cut = _tie_cutoff(count_keys_cols, thr, need, idx_bits)
        cut_ref[...] = jnp.broadcast_to(jnp.where(over, cut, jnp.int32(1 << idx_bits)), (rows, LANES))


def _sample_thr(keys, nkeys, t, rows):
    n, past = keys.shape
    topk = min(TOPK_MAX, (past + t) // 4)
    blk = lambda width: pl.BlockSpec((rows, width), lambda i: (i, 0))
    return pl.pallas_call(
        functools.partial(_sample_thr_body, rows=rows, topk=topk,
                          idx_bits=(past + LANES - 1).bit_length(), cw=2048),
        grid=(n // rows,),
        in_specs=[blk(past), blk(LANES)],
        out_specs=[blk(LANES), blk(LANES)],
        out_shape=[jax.ShapeDtypeStruct((n, LANES), F32), jax.ShapeDtypeStruct((n, LANES), I32)],
        compiler_params=_cparams(("arbitrary",)),
        name="sample_thr",
    )(keys, nkeys)


def _sample_attn_body(pt_ref, keys_ref, nkeys_ref, thr_ref, cut_ref, q_ref, kn_ref, vn_ref, *refs,
                      pg, t, past):
    del pt_ref
    kpages, vpages = refs[:pg], refs[pg:2 * pg]
    o_ref = refs[2 * pg]
    qs_ref, kt_ref, vt_ref, m_ref, l_ref, acc_ref = refs[2 * pg + 1:]
    g = pl.program_id(1)
    rows = N_HEADS * t

    def flash_update(s, sel, pv):
        n = s.shape[-1]
        s = jnp.where(sel[None], s.reshape(N_HEADS, t, n), NEG_BIG).reshape(rows, n)
        m_old = m_ref[...]
        m_new = jnp.maximum(m_old, jnp.max(s, axis=1, keepdims=True))
        alpha = jnp.exp2(m_old - m_new)
        p = jnp.exp2(s - m_new)
        l_ref[...] = alpha * l_ref[...] + jnp.sum(p, axis=1, keepdims=True)
        acc_ref[...] = alpha * acc_ref[...] + pv(p.astype(BF16))
        m_ref[...] = m_new

    @pl.when(g == 0)
    def _():
        lo = lax.broadcasted_iota(I32, (t, LANES), 1) < HEAD_DIM
        for c in range(N_HEADS // 2):
            qc = q_ref[:, c * LANES:(c + 1) * LANES]
            qs_ref[c * t:(c + 1) * t, :] = jnp.where(lo, qc, jnp.zeros_like(qc))
            qs_ref[(c + 4) * t:(c + 5) * t, :] = jnp.where(lo, jnp.zeros_like(qc), qc)
        m_ref[...] = jnp.full(m_ref.shape, NEG_BIG, F32)
        l_ref[...] = jnp.zeros(l_ref.shape, F32)
        acc_ref[...] = jnp.zeros(acc_ref.shape, F32)

    thr = thr_ref[:, 0:1]
    cut = cut_ref[:, 0:1]

    def selected(kk, col):
        return (kk > thr) | ((kk == thr) & (col < cut))

    qs = qs_ref[...]
    width = pg * PAGE_SIZE
    for j in range(pg):
        kt_ref[:, j * PAGE_SIZE:(j + 1) * PAGE_SIZE] = kpages[j][...].astype(BF16)
        vt_ref[:, j * PAGE_SIZE:(j + 1) * PAGE_SIZE] = vpages[j][...].astype(BF16)
    col_w = g * width + lax.broadcasted_iota(I32, (t, width), 1)
    flash_update(_dot(qs, kt_ref[...]), selected(keys_ref[...], col_w),
                 lambda p: _dot_t(p, vt_ref[...]))

    @pl.when(g == pl.num_programs(1) - 1)
    def _():
        pad = jnp.zeros((LANES - t, LANES), F32)
        kn = jnp.concatenate([kn_ref[...], pad], axis=0).astype(BF16)
        vn = jnp.concatenate([vn_ref[...], pad], axis=0).astype(BF16)
        col_n = past + lax.broadcasted_iota(I32, (t, LANES), 1)
        flash_update(_dot_t(qs, kn), selected(nkeys_ref[...], col_n), lambda p: _dot(p, vn))
        lo = lax.broadcasted_iota(I32, (t, LANES), 1) < HEAD_DIM
        out = acc_ref[...] / l_ref[...]
        for c in range(N_HEADS // 2):
            oc = jnp.where(lo, out[c * t:(c + 1) * t], out[(c + 4) * t:(c + 5) * t])
            o_ref[:, c * LANES:(c + 1) * LANES] = oc.astype(BF16)


def _sample_attn(page_table, keys, nkeys, thr, cut, q, k_new, v_new, cache_kt, cache_vt, db, t, pg):
    n_pages = page_table.shape[1]
    past = n_pages * PAGE_SIZE
    page_spec = lambda j: pl.BlockSpec((None, KV_DIM, PAGE_SIZE),
                                       lambda b, g, pt: (pt[b, g * pg + j], 0, 0))
    tok = lambda width: pl.BlockSpec((t, width), lambda b, g, pt: (b, 0))
    grid_spec = pltpu.PrefetchScalarGridSpec(
        num_scalar_prefetch=1,
        grid=(db, n_pages // pg),
        in_specs=[pl.BlockSpec((t, pg * PAGE_SIZE), lambda b, g, pt: (b, g)),
                  tok(LANES), tok(LANES), tok(LANES), tok(ATTN_DIM), tok(KV_DIM), tok(KV_DIM)]
                 + [page_spec(j) for j in range(pg)] + [page_spec(j) for j in range(pg)],
        out_specs=tok(ATTN_DIM),
        scratch_shapes=[
            pltpu.VMEM((N_HEADS * t, LANES), BF16),
            pltpu.VMEM((KV_DIM, pg * PAGE_SIZE), BF16),
            pltpu.VMEM((KV_DIM, pg * PAGE_SIZE), BF16),
            pltpu.VMEM((N_HEADS * t, 1), F32),
            pltpu.VMEM((N_HEADS * t, 1), F32),
            pltpu.VMEM((N_HEADS * t, LANES), F32),
        ],
    )
    return pl.pallas_call(
        functools.partial(_sample_attn_body, pg=pg, t=t, past=past),
        grid_spec=grid_spec,
        out_shape=jax.ShapeDtypeStruct((db * t, ATTN_DIM), BF16),
        compiler_params=_cparams(("arbitrary", "arbitrary")),
        name="sample_attn",
    )(page_table, keys, nkeys, thr, cut, q, k_new, v_new, *([cache_kt] * pg), *([cache_vt] * pg))


def _out_body(x_ref, a_ref, c_ref, wa_ref, wc_ref, o_ref):
    o_ref[...] = x_ref[...] + _dot(a_ref[...], wa_ref[...]) + _dot(c_ref[...], wc_ref[...])


def _out_proj(x, attn, conv, wa, wc, tm):
    n = x.shape[0]
    tok = lambda width: pl.BlockSpec((tm, width), lambda i: (i, 0))
    full = lambda a: pl.BlockSpec(a.shape, lambda i: (0, 0))
    return pl.pallas_call(
        _out_body,
        grid=(n // tm,),
        in_specs=[tok(D_MODEL), tok(ATTN_DIM), tok(CONV_DIM), full(wa), full(wc)],
        out_specs=tok(D_MODEL),
        out_shape=jax.ShapeDtypeStruct((n, D_MODEL), F32),
        compiler_params=_cparams(("arbitrary",)),
        name="out_proj",
    )(x, attn, conv, wa, wc)


def _ple_body(x_ref, pe_ref, g_ref, wg_ref, wp_ref, gf_ref, o_ref):
    x = x_ref[...]
    gate = jax.nn.sigmoid(_dot(_rms(x, g_ref[...]).astype(BF16), wg_ref[...]))
    x = x + gate * _dot(pe_ref[...].astype(BF16), wp_ref[...])
    o_ref[...] = _rms(x, gf_ref[...])


def _ple_final(x, pe, g, wg, wp, gf, tm):
    n = x.shape[0]
    tok = lambda width: pl.BlockSpec((tm, width), lambda i: (i, 0))
    full = lambda a: pl.BlockSpec(a.shape, lambda i: (0, 0))
    return pl.pallas_call(
        _ple_body,
        grid=(n // tm,),
        in_specs=[tok(D_MODEL), tok(PLE_DIM), full(g), full(wg), full(wp), full(gf)],
        out_specs=tok(D_MODEL),
        out_shape=jax.ShapeDtypeStruct((n, D_MODEL), F32),
        compiler_params=_cparams(("arbitrary",)),
        name="ple_final",
    )(x, pe, g, wg, wp, gf)


def _rope_tables(pos):
    half = HEAD_DIM // 2
    inv = ROPE_THETA ** (-jnp.arange(half, dtype=F32) / half)
    ang = pos.astype(F32)[:, None] * inv[None, :]
    cos, sin = jnp.cos(ang), jnp.sin(ang)
    cos2 = jnp.concatenate([cos, cos], axis=1)
    sin2 = jnp.concatenate([-sin, sin], axis=1)
    cos128 = jnp.concatenate([cos2, cos2], axis=1)
    sin128 = jnp.concatenate([sin2, sin2], axis=1)
    cosk = jnp.concatenate([cos2, jnp.ones_like(cos2)], axis=1)
    sink = jnp.concatenate([sin2, jnp.zeros_like(sin2)], axis=1)
    return cos128, sin128, cosk, sink


def _prep_w_in(w_in):
    d = w_in.shape[0]
    q = w_in[:, :ATTN_DIM].reshape(d, N_HEADS, HEAD_DIM)[:, jnp.array(_HEAD_PERM)].reshape(d, ATTN_DIM)
    o = ATTN_DIM
    kv = w_in[:, o:o + 2 * KV_DIM]
    o += 2 * KV_DIM
    iq = w_in[:, o:o + N_IDX_HEADS * IDX_DIM]
    o += N_IDX_HEADS * IDX_DIM
    ikw = w_in[:, o:o + IDX_DIM + N_IDX_HEADS]
    o += IDX_DIM + N_IDX_HEADS
    ikw = jnp.pad(ikw, ((0, 0), (0, LANES - IDX_DIM - N_IDX_HEADS)))
    conv = w_in[:, o:]
    return jnp.concatenate([q, kv, iq, ikw, conv], axis=1).astype(BF16)


def _layer(x, pe, mixer, tm, g_ffn1, w1g, w1u, w1d, g_mix, w_in, conv_w, wo_a, wo_c,
           g_ffn2, w2g, w2u, w2d, g_ple, w_pg, w_pp, g_final):
    x1 = _ffn(x, g_ffn1, w1g, w1u, w1d, tm)
    attn, conv, state = mixer(x1, g_mix, w_in, conv_w)
    x2 = _out_proj(x1, attn, conv, wo_a, wo_c, tm)
    x3 = _ffn(x2, g_ffn2, w2g, w2u, w2d, tm)
    return _ple_final(x3, pe, g_ple, w_pg, w_pp, g_final, tm), state


def kernel(x_prompt, x_sample, cache_k, cache_v, cache_idx_k, state_conv, page_table, p_prompt, p_sample,
           g_ffn1, w1_gate, w1_up, w1_down, g_mix, w_in, conv_w, w_out,
           g_ffn2, w2_gate, w2_up, w2_down, g_ple, w_ple_gate, w_ple_proj, g_final):
    depth = w_in.shape[0]
    assert depth == 1, "single-layer step"
    batch, seq, _ = x_prompt.shape
    db, t, _ = x_sample.shape
    n_pages = page_table.shape[1]
    past = n_pages * PAGE_SIZE
    assert t == SUBLANES and seq % 512 == 0

    row = lambda gvec: gvec.reshape(1, -1)
    perm = jnp.array(_HEAD_PERM)
    wo_a = w_out[0, :ATTN_DIM].reshape(N_HEADS, HEAD_DIM, D_MODEL)[perm].reshape(ATTN_DIM, D_MODEL).astype(BF16)
    wo_c = w_out[0, ATTN_DIM:].astype(BF16)
    shared = (row(g_ffn1[0]), w1_gate[0].astype(BF16), w1_up[0].astype(BF16), w1_down[0].astype(BF16),
              row(g_mix[0]), _prep_w_in(w_in[0]), conv_w[0], wo_a, wo_c,
              row(g_ffn2[0]), w2_gate[0].astype(BF16), w2_up[0].astype(BF16), w2_down[0].astype(BF16),
              row(g_ple[0]), w_ple_gate[0].astype(BF16), w_ple_proj[0].astype(BF16), row(g_final))

    tm_p = 512
    tabs_p = _rope_tables(jnp.arange(seq, dtype=I32))

    def prompt_mixer(x1, g, w, cw):
        q, iq, k, v, ikw, kb, vb, ik2, cy, tail = _proj(x1, g, w, *tabs_p, cw, tm_p, seq // tm_p)
        attn = _attn_prompt(q, iq, ikw, ik2, kb, vb, batch, seq, qb=256, kc=512)
        return attn, cy, (k, v, ikw, tail)

    yp, (kp, vp, ikwp, tailp) = _layer(x_prompt.reshape(batch * seq, D_MODEL),
                                       p_prompt[0].reshape(batch * seq, PLE_DIM), prompt_mixer, tm_p, *shared)

    n_s = db * t
    tabs_s = tuple(jnp.tile(a, (db, 1)) for a in _rope_tables(past + jnp.arange(t, dtype=I32)))
    buf = state_conv[0]
    zero = jnp.zeros((db, t - 2, CONV_DIM), F32)
    halo1 = jnp.concatenate([buf[:, 1:2], jnp.zeros((db, t - 1, CONV_DIM), F32)], axis=1).reshape(n_s, CONV_DIM)
    halo2 = jnp.concatenate([buf, zero], axis=1).reshape(n_s, CONV_DIM)
    idx_t = jnp.transpose(cache_idx_k[0], (0, 2, 1))
    k_t = jnp.transpose(cache_k[0], (0, 2, 3, 1)).reshape(-1, KV_DIM, PAGE_SIZE)
    v_t = jnp.transpose(cache_v[0], (0, 2, 3, 1)).reshape(-1, KV_DIM, PAGE_SIZE)

    def sample_mixer(x1, g, w, cw):
        q, iq, k, v, ikw, kb, vb, ik2, cy, u = _proj(x1, g, w, *tabs_s, cw, n_s, 1, halos=(halo1, halo2))
        keys, nkeys = _sample_scores(page_table, iq, ikw, idx_t, db, t, pg=min(32, n_pages))
        thr, cut = _sample_thr(keys, nkeys, t, rows=min(64, n_s))
        attn = _sample_attn(page_table, keys, nkeys, thr, cut, q, k, v, k_t, v_t, db, t, pg=min(16, n_pages))
        return attn, cy, (k, v, ikw, u)

    ys, (ks, vs, ikws, us) = _layer(x_sample.reshape(n_s, D_MODEL), p_sample[0].reshape(n_s, PLE_DIM),
                                    sample_mixer, n_s, *shared)

    return (yp.reshape(batch, seq, D_MODEL),
            ys.reshape(db, t, D_MODEL),
            kp.reshape(1, batch, seq, N_KV_HEADS, HEAD_DIM),
            vp.reshape(1, batch, seq, N_KV_HEADS, HEAD_DIM),
            ikwp[:, :IDX_DIM].reshape(1, batch, seq, IDX_DIM),
            tailp[:, SUBLANES - (CONV_WIDTH - 1):][None],
            ks.reshape(1, db, t, N_KV_HEADS, HEAD_DIM),
            vs.reshape(1, db, t, N_KV_HEADS, HEAD_DIM),
            ikws[:, :IDX_DIM].reshape(1, db, t, IDX_DIM),
            us.reshape(db, t, CONV_DIM)[:, t - (CONV_WIDTH - 1):][None])
```

```python
import functools

import jax
import jax.numpy as jnp
from jax import lax
from jax.experimental import pallas as pl
from jax.experimental.pallas import tpu as pltpu

F32 = jnp.float32
BF16 = jnp.bfloat16
I32 = jnp.int32

D_MODEL = 1024
N_HEADS = 8
HEAD_DIM = 64
N_KV_HEADS = 2
ATTN_DIM = N_HEADS * HEAD_DIM
KV_DIM = N_KV_HEADS * HEAD_DIM
N_IDX_HEADS = 4
IDX_DIM = 64
TOPK_MAX = 256
CONV_DIM = D_MODEL - ATTN_DIM
CONV_WIDTH = 3
D_FF = 2816
PLE_DIM = 256
PAGE_SIZE = 128
ROPE_THETA = 10000.0
RMS_EPS = 1e-6

ROW_TILE = 64
LANES = 128
SUBLANES = 8
VMEM_LIMIT = 56 * 1024 * 1024

INT_MIN = -(2 ** 31)
MASKED_SCORE = -1e38
MIN_THRESHOLD = -5e37
NEG_BIG = -1e30
LOG2E = 1.4426950408889634

_Q0, _K0, _V0, _IQ0, _IK0, _CB0, _CC0, _CH0, _PROJ_W = 0, 512, 640, 768, 1024, 1152, 1664, 2176, 2688
_HEAD_PERM = (0, 4, 1, 5, 2, 6, 3, 7)


def _cparams(sem):
    return pltpu.CompilerParams(dimension_semantics=sem, vmem_limit_bytes=VMEM_LIMIT)


def _dot(a, b):
    return jnp.dot(a, b, preferred_element_type=F32)


def _dot_t(a, b):
    return lax.dot_general(a, b, (((1,), (1,)), ((), ())), preferred_element_type=F32)


def _rms(x, g):
    ms = jnp.mean(x * x, axis=-1, keepdims=True)
    return x * lax.rsqrt(ms + RMS_EPS) * g


def _ffn_body(x_ref, g_ref, wg_ref, wu_ref, wd_ref, o_ref, h_ref, acc_ref):
    j = pl.program_id(1)

    @pl.when(j == 0)
    def _():
        h_ref[...] = _rms(x_ref[...], g_ref[...]).astype(BF16)

    h = h_ref[...]
    gate = _dot(h, wg_ref[...])
    up = _dot(h, wu_ref[...])
    act = (gate * jax.nn.sigmoid(gate) * up).astype(BF16)
    part = _dot(act, wd_ref[...])

    @pl.when(j == 0)
    def _():
        acc_ref[...] = part

    @pl.when(j > 0)
    def _():
        acc_ref[...] += part

    @pl.when(j == pl.num_programs(1) - 1)
    def _():
        o_ref[...] = x_ref[...] + 0.5 * acc_ref[...]


def _ffn(x, g, wg, wu, wd, tm):
    n = x.shape[0]
    ff_chunk = D_FF // 2
    return pl.pallas_call(
        _ffn_body,
        grid=(n // tm, D_FF // ff_chunk),
        in_specs=[
            pl.BlockSpec((tm, D_MODEL), lambda i, j: (i, 0)),
            pl.BlockSpec((1, D_MODEL), lambda i, j: (0, 0)),
            pl.BlockSpec((D_MODEL, ff_chunk), lambda i, j: (0, j)),
            pl.BlockSpec((D_MODEL, ff_chunk), lambda i, j: (0, j)),
            pl.BlockSpec((ff_chunk, D_MODEL), lambda i, j: (j, 0)),
        ],
        out_specs=pl.BlockSpec((tm, D_MODEL), lambda i, j: (i, 0)),
        out_shape=jax.ShapeDtypeStruct((n, D_MODEL), F32),
        scratch_shapes=[pltpu.VMEM((tm, D_MODEL), BF16), pltpu.VMEM((tm, D_MODEL), F32)],
        compiler_params=_cparams(("arbitrary", "arbitrary")),
        name="ffn",
    )(x, g, wg, wu, wd)


def _swap32(x):
    w = x.shape[-1]
    lane = lax.broadcasted_iota(I32, x.shape, 1)
    return jnp.where((lane & 63) < 32, pltpu.roll(x, w - 32, 1), pltpu.roll(x, 32, 1))


def _rope_cols(z, cos, sin):
    cols = []
    for c in range(z.shape[-1] // LANES):
        zc = z[:, c * LANES:(c + 1) * LANES]
        cols.append(zc * cos + _swap32(zc) * sin)
    return cols[0] if len(cols) == 1 else jnp.concatenate(cols, axis=1)


def _proj_body(*refs, tm, tiles_per_seq, halo):
    if halo:
        (x_ref, g_ref, w_ref, cos_ref, sin_ref, cosk_ref, sink_ref, cw_ref, h1_ref, h2_ref,
         q_ref, iq_ref, k_ref, v_ref, ikw_ref, kb_ref, vb_ref, ik2_ref, cy_ref, u_ref) = refs
    else:
        (x_ref, g_ref, w_ref, cos_ref, sin_ref, cosk_ref, sink_ref, cw_ref,
         q_ref, iq_ref, k_ref, v_ref, ikw_ref, kb_ref, vb_ref, ik2_ref, cy_ref, u_ref, carry_ref) = refs

    h = _rms(x_ref[...], g_ref[...]).astype(BF16)
    cos, sin = cos_ref[...], sin_ref[...]

    zq = _dot(h, w_ref[:, _Q0:_K0])
    q_ref[...] = (_rope_cols(zq, cos, sin) * (HEAD_DIM ** -0.5 * LOG2E)).astype(BF16)
    kr = _rope_cols(_dot(h, w_ref[:, _K0:_V0]), cos, sin)
    k_ref[...] = kr
    kb_ref[...] = kr.astype(BF16)
    zv = _dot(h, w_ref[:, _V0:_IQ0])
    v_ref[...] = zv
    vb_ref[...] = jnp.concatenate([zv, jnp.ones_like(zv)], axis=1).astype(BF16)
    ziq = _dot(h, w_ref[:, _IQ0:_IK0])
    iq_ref[...] = (_rope_cols(ziq, cos, sin) * (IDX_DIM ** -0.5)).astype(BF16)
    ikr = _rope_cols(_dot(h, w_ref[:, _IK0:_CB0]), cosk_ref[...], sink_ref[...])
    ikw_ref[...] = ikr
    lane = lax.broadcasted_iota(I32, ikr.shape, 1)
    ik2_ref[...] = jnp.where(lane < IDX_DIM, ikr, pltpu.roll(ikr, IDX_DIM, 1)).astype(BF16)

    cb = _dot(h, w_ref[:, _CB0:_CC0])
    u = _dot(h, w_ref[:, _CC0:_CH0]) * _dot(h, w_ref[:, _CH0:_PROJ_W])
    row = lax.broadcasted_iota(I32, u.shape, 0)
    r1 = pltpu.roll(u, 1, 0)
    r2 = pltpu.roll(u, 2, 0)
    if halo:
        t = row & (SUBLANES - 1)
        us1 = jnp.where(t == 0, h1_ref[...], r1)
        us2 = jnp.where(t < 2, h2_ref[...], r2)
        u_ref[...] = u
    else:
        @pl.when(pl.program_id(0) % tiles_per_seq == 0)
        def _():
            carry_ref[...] = jnp.zeros_like(carry_ref)

        c0 = carry_ref[SUBLANES - 2:SUBLANES - 1, :]
        c1 = carry_ref[SUBLANES - 1:SUBLANES, :]
        us1 = jnp.where(row == 0, c1, r1)
        us2 = jnp.where(row == 0, c0, jnp.where(row == 1, c1, r2))
        tail = u[tm - SUBLANES:tm, :]
        carry_ref[...] = tail
        u_ref[0] = tail
    cw = cw_ref[...]
    y = cb * (cw[0:1, :] * us2 + cw[1:2, :] * us1 + cw[2:3, :] * u)
    cy_ref[...] = y.astype(BF16)


def _proj(x, g, w, cos, sin, cosk, sink, cw, tm, tiles_per_seq, halos=None):
    n = x.shape[0]
    n_tiles = n // tm
    halo = halos is not None
    tok = lambda width: pl.BlockSpec((tm, width), lambda i: (i, 0))
    tab = pl.BlockSpec((tm, LANES), lambda i: (i % tiles_per_seq, 0))
    full = lambda a: pl.BlockSpec(a.shape, lambda i: (0,) * a.ndim)
    in_specs = [tok(D_MODEL), full(g), full(w), tab, tab, tab, tab, full(cw)]
    args = [x, g, w, cos, sin, cosk, sink, cw]
    out_shape = [
        jax.ShapeDtypeStruct((n, ATTN_DIM), BF16),
        jax.ShapeDtypeStruct((n, N_IDX_HEADS * IDX_DIM), BF16),
        jax.ShapeDtypeStruct((n, KV_DIM), F32),
        jax.ShapeDtypeStruct((n, KV_DIM), F32),
        jax.ShapeDtypeStruct((n, LANES), F32),
        jax.ShapeDtypeStruct((n, KV_DIM), BF16),
        jax.ShapeDtypeStruct((n, 2 * KV_DIM), BF16),
        jax.ShapeDtypeStruct((n, LANES), BF16),
        jax.ShapeDtypeStruct((n, CONV_DIM), BF16),
    ]
    out_specs = [tok(ATTN_DIM), tok(N_IDX_HEADS * IDX_DIM), tok(KV_DIM), tok(KV_DIM), tok(LANES),
                 tok(KV_DIM), tok(2 * KV_DIM), tok(LANES), tok(CONV_DIM)]
    scratch = []
    if halo:
        in_specs += [tok(CONV_DIM), tok(CONV_DIM)]
        args += list(halos)
        out_shape.append(jax.ShapeDtypeStruct((n, CONV_DIM), F32))
        out_specs.append(tok(CONV_DIM))
    else:
        n_seq = n_tiles // tiles_per_seq
        out_shape.append(jax.ShapeDtypeStruct((n_seq, SUBLANES, CONV_DIM), F32))
        out_specs.append(pl.BlockSpec((1, SUBLANES, CONV_DIM), lambda i: (i // tiles_per_seq, 0, 0)))
        scratch.append(pltpu.VMEM((SUBLANES, CONV_DIM), F32))
    return pl.pallas_call(
        functools.partial(_proj_body, tm=tm, tiles_per_seq=tiles_per_seq, halo=halo),
        grid=(n_tiles,),
        in_specs=in_specs,
        out_specs=out_specs,
        out_shape=out_shape,
        scratch_shapes=scratch,
        compiler_params=_cparams(("arbitrary",)),
        name="proj_sample" if halo else "proj_prompt",
    )(*args)


def _lane_fold(ind):
    acc = ind[:, 0:LANES]
    for j in range(1, ind.shape[-1] // LANES):
        acc = acc + ind[:, j * LANES:(j + 1) * LANES]
    return acc


def _key_to_f32(key):
    return lax.bitcast_convert_type(key ^ ((key >> 31) & 0x7FFFFFFF), F32)


def _select_threshold(count_ge, rows, topk):
    def step(b, t):
        cand = t + lax.shift_left(jnp.int32(1), jnp.int32(31) - jnp.asarray(b, I32))
        return jnp.where(count_ge(_key_to_f32(cand)) >= topk, cand, t)

    return _key_to_f32(lax.fori_loop(0, 32, step, jnp.full((rows, 1), INT_MIN, I32)))


def _tie_cutoff(count_fn, t, need, idx_bits):
    def step(b, c):
        cand = c + lax.shift_left(jnp.int32(1), jnp.int32(idx_bits - 1) - jnp.asarray(b, I32))
        cnt = count_fn(lambda s, col: (s == t) & (col < cand))
        return jnp.where(cnt <= need, cand, c)

    return lax.fori_loop(0, idx_bits, step, jnp.zeros(t.shape, I32))


def _attn_prompt_body(q_ref, iq_ref, w_ref, ik2_ref, kb_ref, vb_ref, tri_ref, o_ref,
                      sc_ref, qs_ref, iqs_ref, s0_ref, s1_ref, p_ref, bias_ref,
                      m_ref, a_ref, acc_ref,
                      *, qb, kc, topk, n_chunks):
    i = (n_chunks - 1) * (kc // qb) + pl.program_id(1)
    lo = lax.broadcasted_iota(I32, (qb, LANES), 1) < HEAD_DIM

    for c in range(N_HEADS // 2):
        qc = q_ref[:, c * LANES:(c + 1) * LANES]
        qs_ref[c * qb:(c + 1) * qb, :] = jnp.where(lo, qc, jnp.zeros_like(qc))
        qs_ref[(c + 4) * qb:(c + 5) * qb, :] = jnp.where(lo, jnp.zeros_like(qc), qc)
    for c in range(N_IDX_HEADS // 2):
        ic = iq_ref[:, c * LANES:(c + 1) * LANES]
        iqs_ref[(2 * c) * qb:(2 * c + 1) * qb, :] = jnp.where(lo, ic, jnp.zeros_like(ic))
        iqs_ref[(2 * c + 1) * qb:(2 * c + 2) * qb, :] = jnp.where(lo, jnp.zeros_like(ic), ic)

    row_t = lax.broadcasted_iota(I32, (ROW_TILE, kc), 0)
    col_t = lax.broadcasted_iota(I32, (ROW_TILE, kc), 1)

    def score_chunk(c, carry):
        off = pl.multiple_of(c * kc, kc)
        s0_ref[0:N_IDX_HEADS * qb, :] = _dot_t(iqs_ref[...], ik2_ref[pl.ds(off, kc), :])
        for r0 in range(0, qb, ROW_TILE):
            r = slice(r0, r0 + ROW_TILE)
            wt = w_ref[r, :] * (N_IDX_HEADS ** -0.5)
            sc = wt[:, IDX_DIM:IDX_DIM + 1] * jnp.maximum(s0_ref[r, :], 0.0)
            for h in range(1, N_IDX_HEADS):
                sc = sc + (wt[:, IDX_DIM + h:IDX_DIM + h + 1]
                           * jnp.maximum(s0_ref[h * qb + r0:h * qb + r0 + ROW_TILE, :], 0.0))
            sc_ref[c, r, :] = jnp.where(off + col_t <= i * qb + r0 + row_t, sc, MASKED_SCORE)
        return carry

    lax.fori_loop(0, n_chunks, score_chunk, 0)

    def kth_largest_score(n):
        def count_ge(cand):
            cw = jnp.concatenate([jnp.broadcast_to(cand, (qb, LANES))] * (kc // LANES), axis=1)
            acc = jnp.zeros((qb, LANES), F32)
            for c in range(n):
                acc = acc + _lane_fold(jnp.where(sc_ref[c] >= cw, 1.0, 0.0))
            return jnp.sum(acc, axis=1, keepdims=True)

        return _select_threshold(count_ge, qb, float(topk))

    def count_scores(pred):
        def body(c, acc):
            return acc + _lane_fold(jnp.where(pred(sc_ref[c]), 1.0, 0.0))
        acc = lax.fori_loop(0, n_chunks, body, jnp.zeros((qb, LANES), F32))
        return jnp.sum(acc, axis=1, keepdims=True)

    thr = kth_largest_score(n_chunks)
    thr = jnp.maximum(thr, MIN_THRESHOLD)
    over = count_scores(lambda s: s >= thr) > float(topk)

    @pl.when(jnp.max(jnp.where(over, 1.0, 0.0)) > 0.0)
    def _():
        need = float(topk) - count_scores(lambda s: s > thr)

        def drop(c, seen):
            s = sc_ref[c]
            tie = s == thr
            ind = jnp.where(tie, 1.0, 0.0)
            before = seen + _dot(ind.astype(BF16), tri_ref[...])
            sc_ref[c] = jnp.where(tie & (before >= need), MASKED_SCORE, s)
            return seen + jnp.sum(ind, axis=1, keepdims=True)

        lax.fori_loop(0, n_chunks, drop, jnp.zeros((qb, 1), F32))

    m_ref[...] = jnp.full(m_ref.shape, NEG_BIG, F32)
    acc_ref[...] = jnp.zeros(acc_ref.shape, F32)

    last = n_chunks - 1

    def logits(c):
        off = pl.multiple_of(jnp.minimum(c, last) * kc, kc)
        return _dot_t(qs_ref[...], kb_ref[pl.ds(off, kc), :])

    def softmax_pv(c, s_ref):
        cc = jnp.minimum(c, last)
        off = pl.multiple_of(cc * kc, kc)
        bias_ref[...] = jnp.where(c <= last, jnp.where(sc_ref[cc] >= thr, 0.0, NEG_BIG), NEG_BIG)
        for r0 in range(0, N_HEADS * qb, ROW_TILE):
            r = slice(r0, r0 + ROW_TILE)
            s = s_ref[r, :] + bias_ref[r0 % qb:r0 % qb + ROW_TILE, :]
            m_old = m_ref[r, :]
            m_new = jnp.maximum(m_old, jnp.max(s, axis=1, keepdims=True))
            p_ref[r, :] = jnp.exp2(s - jnp.concatenate([m_new] * (kc // LANES), axis=1)).astype(BF16)
            a_ref[r, :] = jnp.exp2(m_old - m_new)
            m_ref[r, :] = m_new
        alpha = a_ref[...]
        acc_ref[...] = (jnp.concatenate([alpha, alpha], axis=1) * acc_ref[...]
                        + _dot(p_ref[...], vb_ref[pl.ds(off, kc), :]))

    s0_ref[...] = logits(0)

    def chunk_pair(j, carry):
        c0 = 2 * j
        s1_ref[...] = logits(c0 + 1)
        softmax_pv(c0, s0_ref)
        s0_ref[...] = logits(c0 + 2)
        softmax_pv(c0 + 1, s1_ref)
        return carry

    lax.fori_loop(0, (n_chunks + 1) // 2, chunk_pair, 0)

    out = acc_ref[:, 0:LANES] / acc_ref[:, LANES:2 * LANES]
    for c in range(N_HEADS // 2):
        oc = jnp.where(lo, out[c * qb:(c + 1) * qb], out[(c + 4) * qb:(c + 5) * qb])
        o_ref[:, c * LANES:(c + 1) * LANES] = oc.astype(BF16)


def _attn_prompt(q, iq, ikw, ik2, kb, vb, batch, seq, qb, kc):
    nqb = seq // qb
    per_call = kc // qb
    topk = min(TOPK_MAX, seq // 4)
    as_seq = lambda a: a.reshape(batch, seq, a.shape[-1])
    ik2, kb, vb = as_seq(ik2), as_seq(kb), as_seq(vb)
    tri = jnp.triu(jnp.ones((kc, kc), BF16), k=1)
    outs = []
    for n in range(1, seq // kc + 1):
        outs.append(_attn_prompt_call(q, iq, ikw, ik2, kb, vb, tri, batch, qb, kc, topk, n, nqb, per_call))
    out = jnp.concatenate([o.reshape(batch, per_call * qb, ATTN_DIM) for o in outs], axis=1)
    return out.reshape(batch * seq, ATTN_DIM)


def _attn_prompt_call(q, iq, ikw, ik2, kb, vb, tri, batch, qb, kc, topk, n, nqb, per_call):
    i0 = (n - 1) * per_call
    tokq = lambda width: pl.BlockSpec((qb, width), lambda b, i: (b * nqb + i0 + i, 0))
    seqk = lambda width: pl.BlockSpec((None, n * kc, width), lambda b, i: (b, 0, 0))
    return pl.pallas_call(
        functools.partial(_attn_prompt_body, qb=qb, kc=kc, topk=topk, n_chunks=n),
        grid=(batch, per_call),
        in_specs=[tokq(ATTN_DIM), tokq(N_IDX_HEADS * IDX_DIM), tokq(LANES),
                  seqk(LANES), seqk(KV_DIM), seqk(2 * KV_DIM),
                  pl.BlockSpec((kc, kc), lambda b, i: (0, 0))],
        out_specs=pl.BlockSpec((qb, ATTN_DIM), lambda b, i: (b * per_call + i, 0)),
        out_shape=jax.ShapeDtypeStruct((batch * per_call * qb, ATTN_DIM), BF16),
        scratch_shapes=[
            pltpu.VMEM((n, qb, kc), F32),
            pltpu.VMEM((N_HEADS * qb, LANES), BF16),
            pltpu.VMEM((N_IDX_HEADS * qb, LANES), BF16),
            pltpu.VMEM((N_HEADS * qb, kc), F32),
            pltpu.VMEM((N_HEADS * qb, kc), F32),
            pltpu.VMEM((N_HEADS * qb, kc), BF16),
            pltpu.VMEM((qb, kc), F32),
            pltpu.VMEM((N_HEADS * qb, LANES), F32),
            pltpu.VMEM((N_HEADS * qb, LANES), F32),
            pltpu.VMEM((N_HEADS * qb, 2 * KV_DIM), F32),
        ],
        compiler_params=_cparams(("arbitrary", "arbitrary")),
        name=f"attn_prompt_{n}",
    )(q, iq, ikw, ik2, kb, vb, tri)


def _stack_heads(x_bf, n_heads):
    x = x_bf.astype(F32)
    return jnp.concatenate([x[:, h * HEAD_DIM:(h + 1) * HEAD_DIM] for h in range(n_heads)], axis=0).astype(BF16)


def _weighted_relu_sum(logits, wcol, t):
    sc = wcol[0] * jnp.maximum(logits[0:t], 0.0)
    for h in range(1, N_IDX_HEADS):
        sc = sc + wcol[h] * jnp.maximum(logits[h * t:(h + 1) * t], 0.0)
    return sc


def _head_weights(w_ref):
    wq = w_ref[...]
    return [wq[:, IDX_DIM + h:IDX_DIM + h + 1] * (N_IDX_HEADS ** -0.5) for h in range(N_IDX_HEADS)]


def _sample_scores_body(pt_ref, iq_ref, w_ref, *refs, pg, t):
    del pt_ref
    pages = refs[:pg]
    keys_ref, nkeys_ref, ikt_ref = refs[pg:]
    iqs = _stack_heads(iq_ref[...], N_IDX_HEADS)
    wcol = _head_weights(w_ref)
    for j in range(pg):
        ikt_ref[:, j * PAGE_SIZE:(j + 1) * PAGE_SIZE] = pages[j][...].astype(BF16)
    logits = _dot(iqs, ikt_ref[...])
    keys_ref[...] = _weighted_relu_sum(logits, wcol, t)

    @pl.when(pl.program_id(1) == 0)
    def _():
        ikn = jnp.concatenate([w_ref[...][:, 0:IDX_DIM], jnp.zeros((LANES - t, IDX_DIM), F32)], axis=0)
        sc = _weighted_relu_sum(_dot_t(iqs, ikn.astype(BF16)), wcol, t)
        col_n = lax.broadcasted_iota(I32, (t, LANES), 1)
        row_n = lax.broadcasted_iota(I32, (t, LANES), 0)
        nkeys_ref[...] = jnp.where(col_n <= row_n, sc, MASKED_SCORE)


def _sample_scores(page_table, iq, ikw, cache_idx_t, db, t, pg):
    n_pages = page_table.shape[1]
    page_spec = lambda j: pl.BlockSpec((None, IDX_DIM, PAGE_SIZE),
                                       lambda b, g, pt: (pt[b, g * pg + j], 0, 0))
    grid_spec = pltpu.PrefetchScalarGridSpec(
        num_scalar_prefetch=1,
        grid=(db, n_pages // pg),
        in_specs=[pl.BlockSpec((t, N_IDX_HEADS * IDX_DIM), lambda b, g, pt: (b, 0)),
                  pl.BlockSpec((t, LANES), lambda b, g, pt: (b, 0))] + [page_spec(j) for j in range(pg)],
        out_specs=[pl.BlockSpec((t, pg * PAGE_SIZE), lambda b, g, pt: (b, g)),
                   pl.BlockSpec((t, LANES), lambda b, g, pt: (b, 0))],
        scratch_shapes=[pltpu.VMEM((IDX_DIM, pg * PAGE_SIZE), BF16)],
    )
    return pl.pallas_call(
        functools.partial(_sample_scores_body, pg=pg, t=t),
        grid_spec=grid_spec,
        out_shape=[jax.ShapeDtypeStruct((db * t, n_pages * PAGE_SIZE), F32),
                   jax.ShapeDtypeStruct((db * t, LANES), F32)],
        compiler_params=_cparams(("arbitrary", "arbitrary")),
        name="sample_scores",
    )(page_table, iq, ikw, *([cache_idx_t] * pg))


def _sample_thr_body(keys_ref, nkeys_ref, thr_ref, cut_ref, *, rows, topk, idx_bits, cw):
    past = keys_ref.shape[-1]
    col_c = lax.broadcasted_iota(I32, (rows, cw), 1)
    col_n = lax.broadcasted_iota(I32, (rows, LANES), 1)

    def count_keys(pred):
        acc = jnp.where(pred(nkeys_ref[...]), 1.0, 0.0)
        for c in range(past // cw):
            acc = acc + _lane_fold(jnp.where(pred(keys_ref[:, c * cw:(c + 1) * cw]), 1.0, 0.0))
        return jnp.sum(acc, axis=1, keepdims=True)

    def count_keys_cols(pred):
        acc = jnp.where(pred(nkeys_ref[...], past + col_n), 1.0, 0.0)
        for c in range(past // cw):
            acc = acc + _lane_fold(jnp.where(pred(keys_ref[:, c * cw:(c + 1) * cw], c * cw + col_c), 1.0, 0.0))
        return jnp.sum(acc, axis=1, keepdims=True)

    thr = _select_threshold(lambda cand: count_keys(lambda kk: kk >= cand), rows, float(topk))
    thr = jnp.maximum(thr, MIN_THRESHOLD)
    over = count_keys(lambda kk: kk >= thr) > float(topk)
    thr_ref[...] = jnp.broadcast_to(thr, (rows, LANES))
    cut_ref[...] = jnp.full((rows, LANES), 1 << idx_bits, I32)

    @pl.when(jnp.max(jnp.where(over, 1.0, 0.0)) > 0.0)
    def _():
        need = float(topk) - count_keys(lambda kk: kk > thr)
        cut = _tie_cutoff(count_keys_cols, thr, need, idx_bits)
        cut_ref[...] = jnp.broadcast_to(jnp.where(over, cut, jnp.int32(1 << idx_bits)), (rows, LANES))


def _sample_thr(keys, nkeys, t, rows):
    n, past = keys.shape
    topk = min(TOPK_MAX, (past + t) // 4)
    blk = lambda width: pl.BlockSpec((rows, width), lambda i: (i, 0))
    return pl.pallas_call(
        functools.partial(_sample_thr_body, rows=rows, topk=topk,
                          idx_bits=(past + LANES - 1).bit_length(), cw=2048),
        grid=(n // rows,),
        in_specs=[blk(past), blk(LANES)],
        out_specs=[blk(LANES), blk(LANES)],
        out_shape=[jax.ShapeDtypeStruct((n, LANES), F32), jax.ShapeDtypeStruct((n, LANES), I32)],
        compiler_params=_cparams(("arbitrary",)),
        name="sample_thr",
    )(keys, nkeys)


def _sample_attn_body(pt_ref, keys_ref, nkeys_ref, thr_ref, cut_ref, q_ref, kn_ref, vn_ref, *refs,
                      pg, t, past):
    del pt_ref
    kpages, vpages = refs[:pg], refs[pg:2 * pg]
    o_ref = refs[2 * pg]
    qs_ref, kt_ref, vt_ref, m_ref, l_ref, acc_ref = refs[2 * pg + 1:]
    g = pl.program_id(1)
    rows = N_HEADS * t

    def flash_update(s, sel, pv):
        n = s.shape[-1]
        s = jnp.where(sel[None], s.reshape(N_HEADS, t, n), NEG_BIG).reshape(rows, n)
        m_old = m_ref[...]
        m_new = jnp.maximum(m_old, jnp.max(s, axis=1, keepdims=True))
        alpha = jnp.exp2(m_old - m_new)
        p = jnp.exp2(s - m_new)
        l_ref[...] = alpha * l_ref[...] + jnp.sum(p, axis=1, keepdims=True)
        acc_ref[...] = alpha * acc_ref[...] + pv(p.astype(BF16))
        m_ref[...] = m_new

    @pl.when(g == 0)
    def _():
        lo = lax.broadcasted_iota(I32, (t, LANES), 1) < HEAD_DIM
        for c in range(N_HEADS // 2):
            qc = q_ref[:, c * LANES:(c + 1) * LANES]
            qs_ref[c * t:(c + 1) * t, :] = jnp.where(lo, qc, jnp.zeros_like(qc))
            qs_ref[(c + 4) * t:(c + 5) * t, :] = jnp.where(lo, jnp.zeros_like(qc), qc)
        m_ref[...] = jnp.full(m_ref.shape, NEG_BIG, F32)
        l_ref[...] = jnp.zeros(l_ref.shape, F32)
        acc_ref[...] = jnp.zeros(acc_ref.shape, F32)

    thr = thr_ref[:, 0:1]
    cut = cut_ref[:, 0:1]

    def selected(kk, col):
        return (kk > thr) | ((kk == thr) & (col < cut))

    qs = qs_ref[...]
    width = pg * PAGE_SIZE
    for j in range(pg):
        kt_ref[:, j * PAGE_SIZE:(j + 1) * PAGE_SIZE] = kpages[j][...].astype(BF16)
        vt_ref[:, j * PAGE_SIZE:(j + 1) * PAGE_SIZE] = vpages[j][...].astype(BF16)
    col_w = g * width + lax.broadcasted_iota(I32, (t, width), 1)
    flash_update(_dot(qs, kt_ref[...]), selected(keys_ref[...], col_w),
                 lambda p: _dot_t(p, vt_ref[...]))

    @pl.when(g == pl.num_programs(1) - 1)
    def _():
        pad = jnp.zeros((LANES - t, LANES), F32)
        kn = jnp.concatenate([kn_ref[...], pad], axis=0).astype(BF16)
        vn = jnp.concatenate([vn_ref[...], pad], axis=0).astype(BF16)
        col_n = past + lax.broadcasted_iota(I32, (t, LANES), 1)
        flash_update(_dot_t(qs, kn), selected(nkeys_ref[...], col_n), lambda p: _dot(p, vn))
        lo = lax.broadcasted_iota(I32, (t, LANES), 1) < HEAD_DIM
        out = acc_ref[...] / l_ref[...]
        for c in range(N_HEADS // 2):
            oc = jnp.where(lo, out[c * t:(c + 1) * t], out[(c + 4) * t:(c + 5) * t])
            o_ref[:, c * LANES:(c + 1) * LANES] = oc.astype(BF16)


def _sample_attn(page_table, keys, nkeys, thr, cut, q, k_new, v_new, cache_kt, cache_vt, db, t, pg):
    n_pages = page_table.shape[1]
    past = n_pages * PAGE_SIZE
    page_spec = lambda j: pl.BlockSpec((None, KV_DIM, PAGE_SIZE),
                                       lambda b, g, pt: (pt[b, g * pg + j], 0, 0))
    tok = lambda width: pl.BlockSpec((t, width), lambda b, g, pt: (b, 0))
    grid_spec = pltpu.PrefetchScalarGridSpec(
        num_scalar_prefetch=1,
        grid=(db, n_pages // pg),
        in_specs=[pl.BlockSpec((t, pg * PAGE_SIZE), lambda b, g, pt: (b, g)),
                  tok(LANES), tok(LANES), tok(LANES), tok(ATTN_DIM), tok(KV_DIM), tok(KV_DIM)]
                 + [page_spec(j) for j in range(pg)] + [page_spec(j) for j in range(pg)],
        out_specs=tok(ATTN_DIM),
        scratch_shapes=[
            pltpu.VMEM((N_HEADS * t, LANES), BF16),
            pltpu.VMEM((KV_DIM, pg * PAGE_SIZE), BF16),
            pltpu.VMEM((KV_DIM, pg * PAGE_SIZE), BF16),
            pltpu.VMEM((N_HEADS * t, 1), F32),
            pltpu.VMEM((N_HEADS * t, 1), F32),
            pltpu.VMEM((N_HEADS * t, LANES), F32),
        ],
    )
    return pl.pallas_call(
        functools.partial(_sample_attn_body, pg=pg, t=t, past=past),
        grid_spec=grid_spec,
        out_shape=jax.ShapeDtypeStruct((db * t, ATTN_DIM), BF16),
        compiler_params=_cparams(("arbitrary", "arbitrary")),
        name="sample_attn",
    )(page_table, keys, nkeys, thr, cut, q, k_new, v_new, *([cache_kt] * pg), *([cache_vt] * pg))


def _out_body(x_ref, a_ref, c_ref, wa_ref, wc_ref, o_ref):
    o_ref[...] = x_ref[...] + _dot(a_ref[...], wa_ref[...]) + _dot(c_ref[...], wc_ref[...])


def _out_proj(x, attn, conv, wa, wc, tm):
    n = x.shape[0]
    tok = lambda width: pl.BlockSpec((tm, width), lambda i: (i, 0))
    full = lambda a: pl.BlockSpec(a.shape, lambda i: (0, 0))
    return pl.pallas_call(
        _out_body,
        grid=(n // tm,),
        in_specs=[tok(D_MODEL), tok(ATTN_DIM), tok(CONV_DIM), full(wa), full(wc)],
        out_specs=tok(D_MODEL),
        out_shape=jax.ShapeDtypeStruct((n, D_MODEL), F32),
        compiler_params=_cparams(("arbitrary",)),
        name="out_proj",
    )(x, attn, conv, wa, wc)


def _ple_body(x_ref, pe_ref, g_ref, wg_ref, wp_ref, gf_ref, o_ref):
    x = x_ref[...]
    gate = jax.nn.sigmoid(_dot(_rms(x, g_ref[...]).astype(BF16), wg_ref[...]))
    x = x + gate * _dot(pe_ref[...].astype(BF16), wp_ref[...])
    o_ref[...] = _rms(x, gf_ref[...])


def _ple_final(x, pe, g, wg, wp, gf, tm):
    n = x.shape[0]
    tok = lambda width: pl.BlockSpec((tm, width), lambda i: (i, 0))
    full = lambda a: pl.BlockSpec(a.shape, lambda i: (0, 0))
    return pl.pallas_call(
        _ple_body,
        grid=(n // tm,),
        in_specs=[tok(D_MODEL), tok(PLE_DIM), full(g), full(wg), full(wp), full(gf)],
        out_specs=tok(D_MODEL),
        out_shape=jax.ShapeDtypeStruct((n, D_MODEL), F32),
        compiler_params=_cparams(("arbitrary",)),
        name="ple_final",
    )(x, pe, g, wg, wp, gf)


def _rope_tables(pos):
    half = HEAD_DIM // 2
    inv = ROPE_THETA ** (-jnp.arange(half, dtype=F32) / half)
    ang = pos.astype(F32)[:, None] * inv[None, :]
    cos, sin = jnp.cos(ang), jnp.sin(ang)
    cos2 = jnp.concatenate([cos, cos], axis=1)
    sin2 = jnp.concatenate([-sin, sin], axis=1)
    cos128 = jnp.concatenate([cos2, cos2], axis=1)
    sin128 = jnp.concatenate([sin2, sin2], axis=1)
    cosk = jnp.concatenate([cos2, jnp.ones_like(cos2)], axis=1)
    sink = jnp.concatenate([sin2, jnp.zeros_like(sin2)], axis=1)
    return cos128, sin128, cosk, sink


def _prep_w_in(w_in):
    d = w_in.shape[0]
    q = w_in[:, :ATTN_DIM].reshape(d, N_HEADS, HEAD_DIM)[:, jnp.array(_HEAD_PERM)].reshape(d, ATTN_DIM)
    o = ATTN_DIM
    kv = w_in[:, o:o + 2 * KV_DIM]
    o += 2 * KV_DIM
    iq = w_in[:, o:o + N_IDX_HEADS * IDX_DIM]
    o += N_IDX_HEADS * IDX_DIM
    ikw = w_in[:, o:o + IDX_DIM + N_IDX_HEADS]
    o += IDX_DIM + N_IDX_HEADS
    ikw = jnp.pad(ikw, ((0, 0), (0, LANES - IDX_DIM - N_IDX_HEADS)))
    conv = w_in[:, o:]
    return jnp.concatenate([q, kv, iq, ikw, conv], axis=1).astype(BF16)


def _layer(x, pe, mixer, tm, g_ffn1, w1g, w1u, w1d, g_mix, w_in, conv_w, wo_a, wo_c,
           g_ffn2, w2g, w2u, w2d, g_ple, w_pg, w_pp, g_final):
    x1 = _ffn(x, g_ffn1, w1g, w1u, w1d, tm)
    attn, conv, state = mixer(x1, g_mix, w_in, conv_w)
    x2 = _out_proj(x1, attn, conv, wo_a, wo_c, tm)
    x3 = _ffn(x2, g_ffn2, w2g, w2u, w2d, tm)
    return _ple_final(x3, pe, g_ple, w_pg, w_pp, g_final, tm), state


def kernel(x_prompt, x_sample, cache_k, cache_v, cache_idx_k, state_conv, page_table, p_prompt, p_sample,
           g_ffn1, w1_gate, w1_up, w1_down, g_mix, w_in, conv_w, w_out,
           g_ffn2, w2_gate, w2_up, w2_down, g_ple, w_ple_gate, w_ple_proj, g_final):
    depth = w_in.shape[0]
    assert depth == 1, "single-layer step"
    batch, seq, _ = x_prompt.shape
    db, t, _ = x_sample.shape
    n_pages = page_table.shape[1]
    past = n_pages * PAGE_SIZE
    assert t == SUBLANES and seq % 512 == 0

    row = lambda gvec: gvec.reshape(1, -1)
    perm = jnp.array(_HEAD_PERM)
    wo_a = w_out[0, :ATTN_DIM].reshape(N_HEADS, HEAD_DIM, D_MODEL)[perm].reshape(ATTN_DIM, D_MODEL).astype(BF16)
    wo_c = w_out[0, ATTN_DIM:].astype(BF16)
    shared = (row(g_ffn1[0]), w1_gate[0].astype(BF16), w1_up[0].astype(BF16), w1_down[0].astype(BF16),
              row(g_mix[0]), _prep_w_in(w_in[0]), conv_w[0], wo_a, wo_c,
              row(g_ffn2[0]), w2_gate[0].astype(BF16), w2_up[0].astype(BF16), w2_down[0].astype(BF16),
              row(g_ple[0]), w_ple_gate[0].astype(BF16), w_ple_proj[0].astype(BF16), row(g_final))

    tm_p = 512
    tabs_p = _rope_tables(jnp.arange(seq, dtype=I32))

    def prompt_mixer(x1, g, w, cw):
        q, iq, k, v, ikw, kb, vb, ik2, cy, tail = _proj(x1, g, w, *tabs_p, cw, tm_p, seq // tm_p)
        attn = _attn_prompt(q, iq, ikw, ik2, kb, vb, batch, seq, qb=256, kc=512)
        return attn, cy, (k, v, ikw, tail)

    yp, (kp, vp, ikwp, tailp) = _layer(x_prompt.reshape(batch * seq, D_MODEL),
                                       p_prompt[0].reshape(batch * seq, PLE_DIM), prompt_mixer, tm_p, *shared)

    n_s = db * t
    tabs_s = tuple(jnp.tile(a, (db, 1)) for a in _rope_tables(past + jnp.arange(t, dtype=I32)))
    buf = state_conv[0]
    zero = jnp.zeros((db, t - 2, CONV_DIM), F32)
    halo1 = jnp.concatenate([buf[:, 1:2], jnp.zeros((db, t - 1, CONV_DIM), F32)], axis=1).reshape(n_s, CONV_DIM)
    halo2 = jnp.concatenate([buf, zero], axis=1).reshape(n_s, CONV_DIM)
    idx_t = jnp.transpose(cache_idx_k[0], (0, 2, 1))
    k_t = jnp.transpose(cache_k[0], (0, 2, 3, 1)).reshape(-1, KV_DIM, PAGE_SIZE)
    v_t = jnp.transpose(cache_v[0], (0, 2, 3, 1)).reshape(-1, KV_DIM, PAGE_SIZE)

    def sample_mixer(x1, g, w, cw):
        q, iq, k, v, ikw, kb, vb, ik2, cy, u = _proj(x1, g, w, *tabs_s, cw, n_s, 1, halos=(halo1, halo2))
        keys, nkeys = _sample_scores(page_table, iq, ikw, idx_t, db, t, pg=min(32, n_pages))
        thr, cut = _sample_thr(keys, nkeys, t, rows=min(64, n_s))
        attn = _sample_attn(page_table, keys, nkeys, thr, cut, q, k, v, k_t, v_t, db, t, pg=min(16, n_pages))
        return attn, cy, (k, v, ikw, u)

    ys, (ks, vs, ikws, us) = _layer(x_sample.reshape(n_s, D_MODEL), p_sample[0].reshape(n_s, PLE_DIM),
                                    sample_mixer, n_s, *shared)

    return (yp.reshape(batch, seq, D_MODEL),
            ys.reshape(db, t, D_MODEL),
            kp.reshape(1, batch, seq, N_KV_HEADS, HEAD_DIM),
            vp.reshape(1, batch, seq, N_KV_HEADS, HEAD_DIM),
            ikwp[:, :IDX_DIM].reshape(1, batch, seq, IDX_DIM),
            tailp[:, SUBLANES - (CONV_WIDTH - 1):][None],
            ks.reshape(1, db, t, N_KV_HEADS, HEAD_DIM),
            vs.reshape(1, db, t, N_KV_HEADS, HEAD_DIM),
            ikws[:, :IDX_DIM].reshape(1, db, t, IDX_DIM),
            us.reshape(db, t, CONV_DIM)[:, t - (CONV_WIDTH - 1):][None])
```

```python
import functools

import jax
import jax.numpy as jnp
from jax import lax
from jax.experimental import pallas as pl
from jax.experimental.pallas import tpu as pltpu

F32 = jnp.float32
BF16 = jnp.bfloat16
I32 = jnp.int32

D_MODEL = 1024
N_HEADS = 8
HEAD_DIM = 64
N_KV_HEADS = 2
ATTN_DIM = N_HEADS * HEAD_DIM
KV_DIM = N_KV_HEADS * HEAD_DIM
N_IDX_HEADS = 4
IDX_DIM = 64
TOPK_MAX = 256
CONV_DIM = D_MODEL - ATTN_DIM
CONV_WIDTH = 3
D_FF = 2816
PLE_DIM = 256
PAGE_SIZE = 128
ROPE_THETA = 10000.0
RMS_EPS = 1e-6

ROW_TILE = 64
LANES = 128
SUBLANES = 8
VMEM_LIMIT = 56 * 1024 * 1024

INT_MIN = -(2 ** 31)
MASKED_SCORE = -1e38
MIN_THRESHOLD = -5e37
NEG_BIG = -1e30
LOG2E = 1.4426950408889634

_Q0, _K0, _V0, _IQ0, _IK0, _CB0, _CC0, _CH0, _PROJ_W = 0, 512, 640, 768, 1024, 1152, 1664, 2176, 2688
_HEAD_PERM = (0, 4, 1, 5, 2, 6, 3, 7)


def _cparams(sem):
    return pltpu.CompilerParams(dimension_semantics=sem, vmem_limit_bytes=VMEM_LIMIT)


def _dot(a, b):
    return jnp.dot(a, b, preferred_element_type=F32)


def _dot_t(a, b):
    return lax.dot_general(a, b, (((1,), (1,)), ((), ())), preferred_element_type=F32)


def _rms(x, g):
    ms = jnp.mean(x * x, axis=-1, keepdims=True)
    return x * lax.rsqrt(ms + RMS_EPS) * g


def _ffn_body(x_ref, g_ref, wg_ref, wu_ref, wd_ref, o_ref, h_ref, acc_ref):
    j = pl.program_id(1)

    @pl.when(j == 0)
    def _():
        h_ref[...] = _rms(x_ref[...], g_ref[...]).astype(BF16)

    h = h_ref[...]
    gate = _dot(h, wg_ref[...])
    up = _dot(h, wu_ref[...])
    act = (gate * jax.nn.sigmoid(gate) * up).astype(BF16)
    part = _dot(act, wd_ref[...])

    @pl.when(j == 0)
    def _():
        acc_ref[...] = part

    @pl.when(j > 0)
    def _():
        acc_ref[...] += part

    @pl.when(j == pl.num_programs(1) - 1)
    def _():
        o_ref[...] = x_ref[...] + 0.5 * acc_ref[...]


def _ffn(x, g, wg, wu, wd, tm):
    n = x.shape[0]
    ff_chunk = D_FF // 2
    return pl.pallas_call(
        _ffn_body,
        grid=(n // tm, D_FF // ff_chunk),
        in_specs=[
            pl.BlockSpec((tm, D_MODEL), lambda i, j: (i, 0)),
            pl.BlockSpec((1, D_MODEL), lambda i, j: (0, 0)),
            pl.BlockSpec((D_MODEL, ff_chunk), lambda i, j: (0, j)),
            pl.BlockSpec((D_MODEL, ff_chunk), lambda i, j: (0, j)),
            pl.BlockSpec((ff_chunk, D_MODEL), lambda i, j: (j, 0)),
        ],
        out_specs=pl.BlockSpec((tm, D_MODEL), lambda i, j: (i, 0)),
        out_shape=jax.ShapeDtypeStruct((n, D_MODEL), F32),
        scratch_shapes=[pltpu.VMEM((tm, D_MODEL), BF16), pltpu.VMEM((tm, D_MODEL), F32)],
        compiler_params=_cparams(("arbitrary", "arbitrary")),
        name="ffn",
    )(x, g, wg, wu, wd)


def _swap32(x):
    w = x.shape[-1]
    lane = lax.broadcasted_iota(I32, x.shape, 1)
    return jnp.where((lane & 63) < 32, pltpu.roll(x, w - 32, 1), pltpu.roll(x, 32, 1))


def _rope_cols(z, cos, sin):
    cols = []
    for c in range(z.shape[-1] // LANES):
        zc = z[:, c * LANES:(c + 1) * LANES]
        cols.append(zc * cos + _swap32(zc) * sin)
    return cols[0] if len(cols) == 1 else jnp.concatenate(cols, axis=1)


def _proj_body(*refs, tm, tiles_per_seq, halo):
    if halo:
        (x_ref, g_ref, w_ref, cos_ref, sin_ref, cosk_ref, sink_ref, cw_ref, h1_ref, h2_ref,
         q_ref, iq_ref, k_ref, v_ref, ikw_ref, kb_ref, vb_ref, ik2_ref, cy_ref, u_ref) = refs
    else:
        (x_ref, g_ref, w_ref, cos_ref, sin_ref, cosk_ref, sink_ref, cw_ref,
         q_ref, iq_ref, k_ref, v_ref, ikw_ref, kb_ref, vb_ref, ik2_ref, cy_ref, u_ref, carry_ref) = refs

    h = _rms(x_ref[...], g_ref[...]).astype(BF16)
    cos, sin = cos_ref[...], sin_ref[...]

    zq = _dot(h, w_ref[:, _Q0:_K0])
    q_ref[...] = (_rope_cols(zq, cos, sin) * (HEAD_DIM ** -0.5 * LOG2E)).astype(BF16)
    kr = _rope_cols(_dot(h, w_ref[:, _K0:_V0]), cos, sin)
    k_ref[...] = kr if halo else kr.T
    kb_ref[...] = kr.astype(BF16)
    zv = _dot(h, w_ref[:, _V0:_IQ0])
    v_ref[...] = zv if halo else zv.T
    vb_ref[...] = jnp.concatenate([zv, jnp.ones_like(zv)], axis=1).astype(BF16)
    ziq = _dot(h, w_ref[:, _IQ0:_IK0])
    iq_ref[...] = (_rope_cols(ziq, cos, sin) * (IDX_DIM ** -0.5)).astype(BF16)
    ikr = _rope_cols(_dot(h, w_ref[:, _IK0:_CB0]), cosk_ref[...], sink_ref[...])
    ikw_ref[...] = ikr
    lane = lax.broadcasted_iota(I32, ikr.shape, 1)
    ik2_ref[...] = jnp.where(lane < IDX_DIM, ikr, pltpu.roll(ikr, IDX_DIM, 1)).astype(BF16)

    cb = _dot(h, w_ref[:, _CB0:_CC0])
    u = _dot(h, w_ref[:, _CC0:_CH0]) * _dot(h, w_ref[:, _CH0:_PROJ_W])
    row = lax.broadcasted_iota(I32, u.shape, 0)
    r1 = pltpu.roll(u, 1, 0)
    r2 = pltpu.roll(u, 2, 0)
    if halo:
        t = row & (SUBLANES - 1)
        us1 = jnp.where(t == 0, h1_ref[...], r1)
        us2 = jnp.where(t < 2, h2_ref[...], r2)
        u_ref[...] = u
    else:
        @pl.when(pl.program_id(0) % tiles_per_seq == 0)
        def _():
            carry_ref[...] = jnp.zeros_like(carry_ref)

        c0 = carry_ref[SUBLANES - 2:SUBLANES - 1, :]
        c1 = carry_ref[SUBLANES - 1:SUBLANES, :]
        us1 = jnp.where(row == 0, c1, r1)
        us2 = jnp.where(row == 0, c0, jnp.where(row == 1, c1, r2))
        tail = u[tm - SUBLANES:tm, :]
        carry_ref[...] = tail
        u_ref[0] = tail
    cw = cw_ref[...]
    y = cb * (cw[0:1, :] * us2 + cw[1:2, :] * us1 + cw[2:3, :] * u)
    cy_ref[...] = y.astype(BF16)


def _proj(x, g, w, cos, sin, cosk, sink, cw, tm, tiles_per_seq, halos=None):
    n = x.shape[0]
    n_tiles = n // tm
    halo = halos is not None
    tok = lambda width: pl.BlockSpec((tm, width), lambda i: (i, 0))
    tab = pl.BlockSpec((tm, LANES), lambda i: (i % tiles_per_seq, 0))
    full = lambda a: pl.BlockSpec(a.shape, lambda i: (0,) * a.ndim)
    in_specs = [tok(D_MODEL), full(g), full(w), tab, tab, tab, tab, full(cw)]
    args = [x, g, w, cos, sin, cosk, sink, cw]
    out_shape = [
        jax.ShapeDtypeStruct((n, ATTN_DIM), BF16),
        jax.ShapeDtypeStruct((n, N_IDX_HEADS * IDX_DIM), BF16),
        jax.ShapeDtypeStruct((n, KV_DIM), F32),
        jax.ShapeDtypeStruct((n, KV_DIM), F32),
        jax.ShapeDtypeStruct((n, LANES), F32),
        jax.ShapeDtypeStruct((n, KV_DIM), BF16),
        jax.ShapeDtypeStruct((n, 2 * KV_DIM), BF16),
        jax.ShapeDtypeStruct((n, LANES), BF16),
        jax.ShapeDtypeStruct((n, CONV_DIM), BF16),
    ]
    out_specs = [tok(ATTN_DIM), tok(N_IDX_HEADS * IDX_DIM), tok(KV_DIM), tok(KV_DIM), tok(LANES),
                 tok(KV_DIM), tok(2 * KV_DIM), tok(LANES), tok(CONV_DIM)]
    scratch = []
    if halo:
        in_specs += [tok(CONV_DIM), tok(CONV_DIM)]
        args += list(halos)
        out_shape.append(jax.ShapeDtypeStruct((n, CONV_DIM), F32))
        out_specs.append(tok(CONV_DIM))
    else:
        n_seq = n_tiles // tiles_per_seq
        for slot in (2, 3):
            out_shape[slot] = jax.ShapeDtypeStruct((n_seq, KV_DIM, tiles_per_seq * tm), F32)
            out_specs[slot] = pl.BlockSpec((None, KV_DIM, tm), lambda i: (i // tiles_per_seq, 0, i % tiles_per_seq))
        out_shape.append(jax.ShapeDtypeStruct((n_seq, SUBLANES, CONV_DIM), F32))
        out_specs.append(pl.BlockSpec((1, SUBLANES, CONV_DIM), lambda i: (i // tiles_per_seq, 0, 0)))
        scratch.append(pltpu.VMEM((SUBLANES, CONV_DIM), F32))
    return pl.pallas_call(
        functools.partial(_proj_body, tm=tm, tiles_per_seq=tiles_per_seq, halo=halo),
        grid=(n_tiles,),
        in_specs=in_specs,
        out_specs=out_specs,
        out_shape=out_shape,
        scratch_shapes=scratch,
        compiler_params=_cparams(("arbitrary",)),
        name="proj_sample" if halo else "proj_prompt",
    )(*args)


def _lane_fold(ind):
    acc = ind[:, 0:LANES]
    for j in range(1, ind.shape[-1] // LANES):
        acc = acc + ind[:, j * LANES:(j + 1) * LANES]
    return acc


def _key_to_f32(key):
    return lax.bitcast_convert_type(key ^ ((key >> 31) & 0x7FFFFFFF), F32)


def _select_threshold(count_ge, rows, topk):
    def step(b, t):
        cand = t + lax.shift_left(jnp.int32(1), jnp.int32(31) - jnp.asarray(b, I32))
        return jnp.where(count_ge(_key_to_f32(cand)) >= topk, cand, t)

    return _key_to_f32(lax.fori_loop(0, 32, step, jnp.full((rows, 1), INT_MIN, I32)))


def _tie_cutoff(count_fn, t, need, idx_bits):
    def step(b, c):
        cand = c + lax.shift_left(jnp.int32(1), jnp.int32(idx_bits - 1) - jnp.asarray(b, I32))
        cnt = count_fn(lambda s, col: (s == t) & (col < cand))
        return jnp.where(cnt <= need, cand, c)

    return lax.fori_loop(0, idx_bits, step, jnp.zeros(t.shape, I32))


def _attn_prompt_body(q_ref, iq_ref, w_ref, ik2_ref, kb_ref, vb_ref, tri_ref, o_ref,
                      sc_ref, qs_ref, iqs_ref, s0_ref, s1_ref, p_ref, bias_ref,
                      m_ref, a_ref, acc_ref,
                      *, qb, kc, topk, n_chunks):
    i = (n_chunks - 1) * (kc // qb) + pl.program_id(1)
    lo = lax.broadcasted_iota(I32, (qb, LANES), 1) < HEAD_DIM

    for c in range(N_HEADS // 2):
        qc = q_ref[:, c * LANES:(c + 1) * LANES]
        qs_ref[c * qb:(c + 1) * qb, :] = jnp.where(lo, qc, jnp.zeros_like(qc))
        qs_ref[(c + 4) * qb:(c + 5) * qb, :] = jnp.where(lo, jnp.zeros_like(qc), qc)
    for c in range(N_IDX_HEADS // 2):
        ic = iq_ref[:, c * LANES:(c + 1) * LANES]
        iqs_ref[(2 * c) * qb:(2 * c + 1) * qb, :] = jnp.where(lo, ic, jnp.zeros_like(ic))
        iqs_ref[(2 * c + 1) * qb:(2 * c + 2) * qb, :] = jnp.where(lo, jnp.zeros_like(ic), ic)

    row_t = lax.broadcasted_iota(I32, (ROW_TILE, kc), 0)
    col_t = lax.broadcasted_iota(I32, (ROW_TILE, kc), 1)

    def score_chunk(c, carry):
        off = pl.multiple_of(c * kc, kc)
        s0_ref[0:N_IDX_HEADS * qb, :] = _dot_t(iqs_ref[...], ik2_ref[pl.ds(off, kc), :])
        for r0 in range(0, qb, ROW_TILE):
            r = slice(r0, r0 + ROW_TILE)
            wt = w_ref[r, :] * (N_IDX_HEADS ** -0.5)
            sc = wt[:, IDX_DIM:IDX_DIM + 1] * jnp.maximum(s0_ref[r, :], 0.0)
            for h in range(1, N_IDX_HEADS):
                sc = sc + (wt[:, IDX_DIM + h:IDX_DIM + h + 1]
                           * jnp.maximum(s0_ref[h * qb + r0:h * qb + r0 + ROW_TILE, :], 0.0))
            sc_ref[c, r, :] = jnp.where(off + col_t <= i * qb + r0 + row_t, sc, MASKED_SCORE)
        return carry

    lax.fori_loop(0, n_chunks, score_chunk, 0)

    def kth_largest_score(n):
        def count_ge(cand):
            cw = jnp.concatenate([jnp.broadcast_to(cand, (qb, LANES))] * (kc // LANES), axis=1)
            acc = jnp.zeros((qb, LANES), F32)
            for c in range(n):
                acc = acc + _lane_fold(jnp.where(sc_ref[c] >= cw, 1.0, 0.0))
            return jnp.sum(acc, axis=1, keepdims=True)

        return _select_threshold(count_ge, qb, float(topk))

    def count_scores(pred):
        def body(c, acc):
            return acc + _lane_fold(jnp.where(pred(sc_ref[c]), 1.0, 0.0))
        acc = lax.fori_loop(0, n_chunks, body, jnp.zeros((qb, LANES), F32))
        return jnp.sum(acc, axis=1, keepdims=True)

    thr = kth_largest_score(n_chunks)
    thr = jnp.maximum(thr, MIN_THRESHOLD)
    over = count_scores(lambda s: s >= thr) > float(topk)

    @pl.when(jnp.max(jnp.where(over, 1.0, 0.0)) > 0.0)
    def _():
        need = float(topk) - count_scores(lambda s: s > thr)

        def drop(c, seen):
            s = sc_ref[c]
            tie = s == thr
            ind = jnp.where(tie, 1.0, 0.0)
            before = seen + _dot(ind.astype(BF16), tri_ref[...])
            sc_ref[c] = jnp.where(tie & (before >= need), MASKED_SCORE, s)
            return seen + jnp.sum(ind, axis=1, keepdims=True)

        lax.fori_loop(0, n_chunks, drop, jnp.zeros((qb, 1), F32))

    m_ref[...] = jnp.full(m_ref.shape, NEG_BIG, F32)
    acc_ref[...] = jnp.zeros(acc_ref.shape, F32)

    def key_offset(c):
        return c * kc if isinstance(c, int) else pl.multiple_of(c * kc, kc)

    def logits(c):
        return _dot_t(qs_ref[...], kb_ref[pl.ds(key_offset(c), kc), :])

    def softmax_pv(c, s_ref):
        off = key_offset(c)
        bias_ref[...] = jnp.where(sc_ref[c] >= thr, 0.0, NEG_BIG)
        for r0 in range(0, N_HEADS * qb, ROW_TILE):
            r = slice(r0, r0 + ROW_TILE)
            s = s_ref[r, :] + bias_ref[r0 % qb:r0 % qb + ROW_TILE, :]
            m_old = m_ref[r, :]
            m_new = jnp.maximum(m_old, jnp.max(s, axis=1, keepdims=True))
            p_ref[r, :] = jnp.exp2(s - jnp.concatenate([m_new] * (kc // LANES), axis=1)).astype(BF16)
            a_ref[r, :] = jnp.exp2(m_old - m_new)
            m_ref[r, :] = m_new
        alpha = a_ref[...]
        acc_ref[...] = (jnp.concatenate([alpha, alpha], axis=1) * acc_ref[...]
                        + _dot(p_ref[...], vb_ref[pl.ds(off, kc), :]))

    s0_ref[...] = logits(0)

    def chunk_pair(j, carry):
        c0 = 2 * j
        s1_ref[...] = logits(c0 + 1)
        softmax_pv(c0, s0_ref)
        s0_ref[...] = logits(c0 + 2)
        softmax_pv(c0 + 1, s1_ref)
        return carry

    full_pairs = (n_chunks - 1) // 2
    lax.fori_loop(0, full_pairs, chunk_pair, 0)
    if n_chunks % 2 == 0:
        s1_ref[...] = logits(n_chunks - 1)
        softmax_pv(n_chunks - 2, s0_ref)
        softmax_pv(n_chunks - 1, s1_ref)
    else:
        softmax_pv(n_chunks - 1, s0_ref)

    out = acc_ref[:, 0:LANES] / acc_ref[:, LANES:2 * LANES]
    for c in range(N_HEADS // 2):
        oc = jnp.where(lo, out[c * qb:(c + 1) * qb], out[(c + 4) * qb:(c + 5) * qb])
        o_ref[:, c * LANES:(c + 1) * LANES] = oc.astype(BF16)


def _attn_prompt(q, iq, ikw, ik2, kb, vb, batch, seq, qb, kc):
    nqb = seq // qb
    per_call = kc // qb
    topk = min(TOPK_MAX, seq // 4)
    as_seq = lambda a: a.reshape(batch, seq, a.shape[-1])
    ik2, kb, vb = as_seq(ik2), as_seq(kb), as_seq(vb)
    tri = jnp.triu(jnp.ones((kc, kc), BF16), k=1)
    outs = []
    for n in range(1, seq // kc + 1):
        outs.append(_attn_prompt_call(q, iq, ikw, ik2, kb, vb, tri, batch, qb, kc, topk, n, nqb, per_call))
    out = jnp.concatenate([o.reshape(batch, per_call * qb, ATTN_DIM) for o in outs], axis=1)
    return out.reshape(batch * seq, ATTN_DIM)


def _attn_prompt_call(q, iq, ikw, ik2, kb, vb, tri, batch, qb, kc, topk, n, nqb, per_call):
    i0 = (n - 1) * per_call
    tokq = lambda width: pl.BlockSpec((qb, width), lambda b, i: (b * nqb + i0 + i, 0))
    seqk = lambda width: pl.BlockSpec((None, n * kc, width), lambda b, i: (b, 0, 0))
    return pl.pallas_call(
        functools.partial(_attn_prompt_body, qb=qb, kc=kc, topk=topk, n_chunks=n),
        grid=(batch, per_call),
        in_specs=[tokq(ATTN_DIM), tokq(N_IDX_HEADS * IDX_DIM), tokq(LANES),
                  seqk(LANES), seqk(KV_DIM), seqk(2 * KV_DIM),
                  pl.BlockSpec((kc, kc), lambda b, i: (0, 0))],
        out_specs=pl.BlockSpec((qb, ATTN_DIM), lambda b, i: (b * per_call + i, 0)),
        out_shape=jax.ShapeDtypeStruct((batch * per_call * qb, ATTN_DIM), BF16),
        scratch_shapes=[
            pltpu.VMEM((n, qb, kc), F32),
            pltpu.VMEM((N_HEADS * qb, LANES), BF16),
            pltpu.VMEM((N_IDX_HEADS * qb, LANES), BF16),
            pltpu.VMEM((N_HEADS * qb, kc), F32),
            pltpu.VMEM((N_HEADS * qb, kc), F32),
            pltpu.VMEM((N_HEADS * qb, kc), BF16),
            pltpu.VMEM((qb, kc), F32),
            pltpu.VMEM((N_HEADS * qb, LANES), F32),
            pltpu.VMEM((N_HEADS * qb, LANES), F32),
            pltpu.VMEM((N_HEADS * qb, 2 * KV_DIM), F32),
        ],
        compiler_params=_cparams(("arbitrary", "arbitrary")),
        name=f"attn_prompt_{n}",
    )(q, iq, ikw, ik2, kb, vb, tri)


def _stack_heads(x_bf, n_heads):
    x = x_bf.astype(F32)
    return jnp.concatenate([x[:, h * HEAD_DIM:(h + 1) * HEAD_DIM] for h in range(n_heads)], axis=0).astype(BF16)


def _weighted_relu_sum(logits, wcol, t):
    sc = wcol[0] * jnp.maximum(logits[0:t], 0.0)
    for h in range(1, N_IDX_HEADS):
        sc = sc + wcol[h] * jnp.maximum(logits[h * t:(h + 1) * t], 0.0)
    return sc


def _head_weights(w_ref):
    wq = w_ref[...]
    return [wq[:, IDX_DIM + h:IDX_DIM + h + 1] * (N_IDX_HEADS ** -0.5) for h in range(N_IDX_HEADS)]


def _sample_scores_body(pt_ref, iq_ref, w_ref, *refs, pg, t):
    del pt_ref
    pages = refs[:pg]
    keys_ref, nkeys_ref, ikt_ref = refs[pg:]
    iqs = _stack_heads(iq_ref[...], N_IDX_HEADS)
    wcol = _head_weights(w_ref)
    for j in range(pg):
        ikt_ref[:, j * PAGE_SIZE:(j + 1) * PAGE_SIZE] = pages[j][...].astype(BF16)
    logits = _dot(iqs, ikt_ref[...])
    keys_ref[...] = _weighted_relu_sum(logits, wcol, t)

    @pl.when(pl.program_id(1) == 0)
    def _():
        ikn = jnp.concatenate([w_ref[...][:, 0:IDX_DIM], jnp.zeros((LANES - t, IDX_DIM), F32)], axis=0)
        sc = _weighted_relu_sum(_dot_t(iqs, ikn.astype(BF16)), wcol, t)
        col_n = lax.broadcasted_iota(I32, (t, LANES), 1)
        row_n = lax.broadcasted_iota(I32, (t, LANES), 0)
        nkeys_ref[...] = jnp.where(col_n <= row_n, sc, MASKED_SCORE)


def _sample_scores(page_table, iq, ikw, cache_idx_t, db, t, pg):
    n_pages = page_table.shape[1]
    page_spec = lambda j: pl.BlockSpec((None, IDX_DIM, PAGE_SIZE),
                                       lambda b, g, pt: (pt[b, g * pg + j], 0, 0))
    grid_spec = pltpu.PrefetchScalarGridSpec(
        num_scalar_prefetch=1,
        grid=(db, n_pages // pg),
        in_specs=[pl.BlockSpec((t, N_IDX_HEADS * IDX_DIM), lambda b, g, pt: (b, 0)),
                  pl.BlockSpec((t, LANES), lambda b, g, pt: (b, 0))] + [page_spec(j) for j in range(pg)],
        out_specs=[pl.BlockSpec((t, pg * PAGE_SIZE), lambda b, g, pt: (b, g)),
                   pl.BlockSpec((t, LANES), lambda b, g, pt: (b, 0))],
        scratch_shapes=[pltpu.VMEM((IDX_DIM, pg * PAGE_SIZE), BF16)],
    )
    return pl.pallas_call(
        functools.partial(_sample_scores_body, pg=pg, t=t),
        grid_spec=grid_spec,
        out_shape=[jax.ShapeDtypeStruct((db * t, n_pages * PAGE_SIZE), F32),
                   jax.ShapeDtypeStruct((db * t, LANES), F32)],
        compiler_params=_cparams(("arbitrary", "arbitrary")),
        name="sample_scores",
    )(page_table, iq, ikw, *([cache_idx_t] * pg))


def _sample_thr_body(keys_ref, nkeys_ref, thr_ref, cut_ref, *, rows, topk, idx_bits, cw):
    past = keys_ref.shape[-1]
    col_c = lax.broadcasted_iota(I32, (rows, cw), 1)
    col_n = lax.broadcasted_iota(I32, (rows, LANES), 1)

    def count_keys(pred):
        acc = jnp.where(pred(nkeys_ref[...]), 1.0, 0.0)
        for c in range(past // cw):
            acc = acc + _lane_fold(jnp.where(pred(keys_ref[:, c * cw:(c + 1) * cw]), 1.0, 0.0))
        return jnp.sum(acc, axis=1, keepdims=True)

    def count_keys_cols(pred):
        acc = jnp.where(pred(nkeys_ref[...], past + col_n), 1.0, 0.0)
        for c in range(past // cw):
            acc = acc + _lane_fold(jnp.where(pred(keys_ref[:, c * cw:(c + 1) * cw], c * cw + col_c), 1.0, 0.0))
        return jnp.sum(acc, axis=1, keepdims=True)

    thr = _select_threshold(lambda cand: count_keys(lambda kk: kk >= cand), rows, float(topk))
    thr = jnp.maximum(thr, MIN_THRESHOLD)
    over = count_keys(lambda kk: kk >= thr) > float(topk)
    thr_ref[...] = jnp.broadcast_to(thr, (rows, LANES))
    cut_ref[...] = jnp.full((rows, LANES), 1 << idx_bits, I32)

    @pl.when(jnp.max(jnp.where(over, 1.0, 0.0)) > 0.0)
    def _():
        need = float(topk) - count_keys(lambda kk: kk > thr)
        cut = _tie_cutoff(count_keys_cols, thr, need, idx_bits)
        cut_ref[...] = jnp.broadcast_to(jnp.where(over, cut, jnp.int32(1 << idx_bits)), (rows, LANES))


def _sample_thr(keys, nkeys, t, rows):
    n, past = keys.shape
    topk = min(TOPK_MAX, (past + t) // 4)
    blk = lambda width: pl.BlockSpec((rows, width), lambda i: (i, 0))
    return pl.pallas_call(
        functools.partial(_sample_thr_body, rows=rows, topk=topk,
                          idx_bits=(past + LANES - 1).bit_length(), cw=2048),
        grid=(n // rows,),
        in_specs=[blk(past), blk(LANES)],
        out_specs=[blk(LANES), blk(LANES)],
        out_shape=[jax.ShapeDtypeStruct((n, LANES), F32), jax.ShapeDtypeStruct((n, LANES), I32)],
        compiler_params=_cparams(("arbitrary",)),
        name="sample_thr",
    )(keys, nkeys)


def _sample_attn_body(pt_ref, keys_ref, nkeys_ref, thr_ref, cut_ref, q_ref, kn_ref, vn_ref, *refs,
                      pg, t, past):
    del pt_ref
    kpages, vpages = refs[:pg], refs[pg:2 * pg]
    o_ref = refs[2 * pg]
    qs_ref, kt_ref, vt_ref, m_ref, l_ref, acc_ref = refs[2 * pg + 1:]
    g = pl.program_id(1)
    rows = N_HEADS * t

    def flash_update(s, sel, pv):
        n = s.shape[-1]
        s = jnp.where(sel[None], s.reshape(N_HEADS, t, n), NEG_BIG).reshape(rows, n)
        m_old = m_ref[...]
        m_new = jnp.maximum(m_old, jnp.max(s, axis=1, keepdims=True))
        alpha = jnp.exp2(m_old - m_new)
        p = jnp.exp2(s - m_new)
        l_ref[...] = alpha * l_ref[...] + jnp.sum(p, axis=1, keepdims=True)
        acc_ref[...] = alpha * acc_ref[...] + pv(p.astype(BF16))
        m_ref[...] = m_new

    @pl.when(g == 0)
    def _():
        lo = lax.broadcasted_iota(I32, (t, LANES), 1) < HEAD_DIM
        for c in range(N_HEADS // 2):
            qc = q_ref[:, c * LANES:(c + 1) * LANES]
            qs_ref[c * t:(c + 1) * t, :] = jnp.where(lo, qc, jnp.zeros_like(qc))
            qs_ref[(c + 4) * t:(c + 5) * t, :] = jnp.where(lo, jnp.zeros_like(qc), qc)
        m_ref[...] = jnp.full(m_ref.shape, NEG_BIG, F32)
        l_ref[...] = jnp.zeros(l_ref.shape, F32)
        acc_ref[...] = jnp.zeros(acc_ref.shape, F32)

    thr = thr_ref[:, 0:1]
    cut = cut_ref[:, 0:1]

    def selected(kk, col):
        return (kk > thr) | ((kk == thr) & (col < cut))

    qs = qs_ref[...]
    width = pg * PAGE_SIZE
    for j in range(pg):
        kt_ref[:, j * PAGE_SIZE:(j + 1) * PAGE_SIZE] = kpages[j][...].astype(BF16)
        vt_ref[:, j * PAGE_SIZE:(j + 1) * PAGE_SIZE] = vpages[j][...].astype(BF16)
    col_w = g * width + lax.broadcasted_iota(I32, (t, width), 1)
    flash_update(_dot(qs, kt_ref[...]), selected(keys_ref[...], col_w),
                 lambda p: _dot_t(p, vt_ref[...]))

    @pl.when(g == pl.num_programs(1) - 1)
    def _():
        pad = jnp.zeros((LANES - t, LANES), F32)
        kn = jnp.concatenate([kn_ref[...], pad], axis=0).astype(BF16)
        vn = jnp.concatenate([vn_ref[...], pad], axis=0).astype(BF16)
        col_n = past + lax.broadcasted_iota(I32, (t, LANES), 1)
        flash_update(_dot_t(qs, kn), selected(nkeys_ref[...], col_n), lambda p: _dot(p, vn))
        lo = lax.broadcasted_iota(I32, (t, LANES), 1) < HEAD_DIM
        out = acc_ref[...] / l_ref[...]
        for c in range(N_HEADS // 2):
            oc = jnp.where(lo, out[c * t:(c + 1) * t], out[(c + 4) * t:(c + 5) * t])
            o_ref[:, c * LANES:(c + 1) * LANES] = oc.astype(BF16)


def _sample_attn(page_table, keys, nkeys, thr, cut, q, k_new, v_new, cache_kt, cache_vt, db, t, pg):
    n_pages = page_table.shape[1]
    past = n_pages * PAGE_SIZE
    page_spec = lambda j: pl.BlockSpec((None, KV_DIM, PAGE_SIZE),
                                       lambda b, g, pt: (pt[b, g * pg + j], 0, 0))
    tok = lambda width: pl.BlockSpec((t, width), lambda b, g, pt: (b, 0))
    grid_spec = pltpu.PrefetchScalarGridSpec(
        num_scalar_prefetch=1,
        grid=(db, n_pages // pg),
        in_specs=[pl.BlockSpec((t, pg * PAGE_SIZE), lambda b, g, pt: (b, g)),
                  tok(LANES), tok(LANES), tok(LANES), tok(ATTN_DIM), tok(KV_DIM), tok(KV_DIM)]
                 + [page_spec(j) for j in range(pg)] + [page_spec(j) for j in range(pg)],
        out_specs=tok(ATTN_DIM),
        scratch_shapes=[
            pltpu.VMEM((N_HEADS * t, LANES), BF16),
            pltpu.VMEM((KV_DIM, pg * PAGE_SIZE), BF16),
            pltpu.VMEM((KV_DIM, pg * PAGE_SIZE), BF16),
            pltpu.VMEM((N_HEADS * t, 1), F32),
            pltpu.VMEM((N_HEADS * t, 1), F32),
            pltpu.VMEM((N_HEADS * t, LANES), F32),
        ],
    )
    return pl.pallas_call(
        functools.partial(_sample_attn_body, pg=pg, t=t, past=past),
        grid_spec=grid_spec,
        out_shape=jax.ShapeDtypeStruct((db * t, ATTN_DIM), BF16),
        compiler_params=_cparams(("arbitrary", "arbitrary")),
        name="sample_attn",
    )(page_table, keys, nkeys, thr, cut, q, k_new, v_new, *([cache_kt] * pg), *([cache_vt] * pg))


def _out_body(x_ref, a_ref, c_ref, wa_ref, wc_ref, o_ref):
    o_ref[...] = x_ref[...] + _dot(a_ref[...], wa_ref[...]) + _dot(c_ref[...], wc_ref[...])


def _out_proj(x, attn, conv, wa, wc, tm):
    n = x.shape[0]
    tok = lambda width: pl.BlockSpec((tm, width), lambda i: (i, 0))
    full = lambda a: pl.BlockSpec(a.shape, lambda i: (0, 0))
    return pl.pallas_call(
        _out_body,
        grid=(n // tm,),
        in_specs=[tok(D_MODEL), tok(ATTN_DIM), tok(CONV_DIM), full(wa), full(wc)],
        out_specs=tok(D_MODEL),
        out_shape=jax.ShapeDtypeStruct((n, D_MODEL), F32),
        compiler_params=_cparams(("arbitrary",)),
        name="out_proj",
    )(x, attn, conv, wa, wc)


def _ple_body(x_ref, pe_ref, g_ref, wg_ref, wp_ref, gf_ref, o_ref):
    x = x_ref[...]
    gate = jax.nn.sigmoid(_dot(_rms(x, g_ref[...]).astype(BF16), wg_ref[...]))
    x = x + gate * _dot(pe_ref[...].astype(BF16), wp_ref[...])
    o_ref[...] = _rms(x, gf_ref[...])


def _ple_final(x, pe, g, wg, wp, gf, tm):
    n = x.shape[0]
    tok = lambda width: pl.BlockSpec((tm, width), lambda i: (i, 0))
    full = lambda a: pl.BlockSpec(a.shape, lambda i: (0, 0))
    return pl.pallas_call(
        _ple_body,
        grid=(n // tm,),
        in_specs=[tok(D_MODEL), tok(PLE_DIM), full(g), full(wg), full(wp), full(gf)],
        out_specs=tok(D_MODEL),
        out_shape=jax.ShapeDtypeStruct((n, D_MODEL), F32),
        compiler_params=_cparams(("arbitrary",)),
        name="ple_final",
    )(x, pe, g, wg, wp, gf)


def _rope_tables(pos):
    half = HEAD_DIM // 2
    inv = ROPE_THETA ** (-jnp.arange(half, dtype=F32) / half)
    ang = pos.astype(F32)[:, None] * inv[None, :]
    cos, sin = jnp.cos(ang), jnp.sin(ang)
    cos2 = jnp.concatenate([cos, cos], axis=1)
    sin2 = jnp.concatenate([-sin, sin], axis=1)
    cos128 = jnp.concatenate([cos2, cos2], axis=1)
    sin128 = jnp.concatenate([sin2, sin2], axis=1)
    cosk = jnp.concatenate([cos2, jnp.ones_like(cos2)], axis=1)
    sink = jnp.concatenate([sin2, jnp.zeros_like(sin2)], axis=1)
    return cos128, sin128, cosk, sink


def _prep_w_in(w_in):
    d = w_in.shape[0]
    q = w_in[:, :ATTN_DIM].reshape(d, N_HEADS, HEAD_DIM)[:, jnp.array(_HEAD_PERM)].reshape(d, ATTN_DIM)
    o = ATTN_DIM
    kv = w_in[:, o:o + 2 * KV_DIM]
    o += 2 * KV_DIM
    iq = w_in[:, o:o + N_IDX_HEADS * IDX_DIM]
    o += N_IDX_HEADS * IDX_DIM
    ikw = w_in[:, o:o + IDX_DIM + N_IDX_HEADS]
    o += IDX_DIM + N_IDX_HEADS
    ikw = jnp.pad(ikw, ((0, 0), (0, LANES - IDX_DIM - N_IDX_HEADS)))
    conv = w_in[:, o:]
    return jnp.concatenate([q, kv, iq, ikw, conv], axis=1).astype(BF16)


def _layer(x, pe, mixer, tm, g_ffn1, w1g, w1u, w1d, g_mix, w_in, conv_w, wo_a, wo_c,
           g_ffn2, w2g, w2u, w2d, g_ple, w_pg, w_pp, g_final):
    x1 = _ffn(x, g_ffn1, w1g, w1u, w1d, tm)
    attn, conv, state = mixer(x1, g_mix, w_in, conv_w)
    x2 = _out_proj(x1, attn, conv, wo_a, wo_c, tm)
    x3 = _ffn(x2, g_ffn2, w2g, w2u, w2d, tm)
    return _ple_final(x3, pe, g_ple, w_pg, w_pp, g_final, tm), state


def kernel(x_prompt, x_sample, cache_k, cache_v, cache_idx_k, state_conv, page_table, p_prompt, p_sample,
           g_ffn1, w1_gate, w1_up, w1_down, g_mix, w_in, conv_w, w_out,
           g_ffn2, w2_gate, w2_up, w2_down, g_ple, w_ple_gate, w_ple_proj, g_final):
    depth = w_in.shape[0]
    assert depth == 1, "single-layer step"
    batch, seq, _ = x_prompt.shape
    db, t, _ = x_sample.shape
    n_pages = page_table.shape[1]
    past = n_pages * PAGE_SIZE
    assert t == SUBLANES and seq % 512 == 0

    row = lambda gvec: gvec.reshape(1, -1)
    perm = jnp.array(_HEAD_PERM)
    wo_a = w_out[0, :ATTN_DIM].reshape(N_HEADS, HEAD_DIM, D_MODEL)[perm].reshape(ATTN_DIM, D_MODEL).astype(BF16)
    wo_c = w_out[0, ATTN_DIM:].astype(BF16)
    shared = (row(g_ffn1[0]), w1_gate[0].astype(BF16), w1_up[0].astype(BF16), w1_down[0].astype(BF16),
              row(g_mix[0]), _prep_w_in(w_in[0]), conv_w[0], wo_a, wo_c,
              row(g_ffn2[0]), w2_gate[0].astype(BF16), w2_up[0].astype(BF16), w2_down[0].astype(BF16),
              row(g_ple[0]), w_ple_gate[0].astype(BF16), w_ple_proj[0].astype(BF16), row(g_final))

    tm_p = 512
    tabs_p = _rope_tables(jnp.arange(seq, dtype=I32))

    def prompt_mixer(x1, g, w, cw):
        q, iq, k, v, ikw, kb, vb, ik2, cy, tail = _proj(x1, g, w, *tabs_p, cw, tm_p, seq // tm_p)
        attn = _attn_prompt(q, iq, ikw, ik2, kb, vb, batch, seq, qb=256, kc=512)
        return attn, cy, (k, v, ikw, tail)

    yp, (kp, vp, ikwp, tailp) = _layer(x_prompt.reshape(batch * seq, D_MODEL),
                                       p_prompt[0].reshape(batch * seq, PLE_DIM), prompt_mixer, tm_p, *shared)

    n_s = db * t
    tabs_s = tuple(jnp.tile(a, (db, 1)) for a in _rope_tables(past + jnp.arange(t, dtype=I32)))
    buf = state_conv[0]
    zero = jnp.zeros((db, t - 2, CONV_DIM), F32)
    halo1 = jnp.concatenate([buf[:, 1:2], jnp.zeros((db, t - 1, CONV_DIM), F32)], axis=1).reshape(n_s, CONV_DIM)
    halo2 = jnp.concatenate([buf, zero], axis=1).reshape(n_s, CONV_DIM)
    idx_t = jnp.transpose(cache_idx_k[0], (0, 2, 1))
    k_t = jnp.transpose(cache_k[0], (0, 2, 3, 1)).reshape(-1, KV_DIM, PAGE_SIZE)
    v_t = jnp.transpose(cache_v[0], (0, 2, 3, 1)).reshape(-1, KV_DIM, PAGE_SIZE)

    def sample_mixer(x1, g, w, cw):
        q, iq, k, v, ikw, kb, vb, ik2, cy, u = _proj(x1, g, w, *tabs_s, cw, n_s, 1, halos=(halo1, halo2))
        keys, nkeys = _sample_scores(page_table, iq, ikw, idx_t, db, t, pg=min(32, n_pages))
        thr, cut = _sample_thr(keys, nkeys, t, rows=min(128, n_s))
        attn = _sample_attn(page_table, keys, nkeys, thr, cut, q, k, v, k_t, v_t, db, t, pg=min(16, n_pages))
        return attn, cy, (k, v, ikw, u)

    ys, (ks, vs, ikws, us) = _layer(x_sample.reshape(n_s, D_MODEL), p_sample[0].reshape(n_s, PLE_DIM),
                                    sample_mixer, n_s, *shared)

    return (yp.reshape(batch, seq, D_MODEL),
            ys.reshape(db, t, D_MODEL),
            kp.reshape(batch, N_KV_HEADS, HEAD_DIM, seq).transpose(0, 3, 1, 2)[None],
            vp.reshape(batch, N_KV_HEADS, HEAD_DIM, seq).transpose(0, 3, 1, 2)[None],
            ikwp[:, :IDX_DIM].reshape(1, batch, seq, IDX_DIM),
            tailp[:, SUBLANES - (CONV_WIDTH - 1):][None],
            ks.reshape(1, db, t, N_KV_HEADS, HEAD_DIM),
            vs.reshape(1, db, t, N_KV_HEADS, HEAD_DIM),
            ikws[:, :IDX_DIM].reshape(1, db, t, IDX_DIM),
            us.reshape(db, t, CONV_DIM)[:, t - (CONV_WIDTH - 1):][None])
```

```python
import functools

import jax
import jax.numpy as jnp
from jax import lax
from jax.experimental import pallas as pl
from jax.experimental.pallas import tpu as pltpu

F32 = jnp.float32
BF16 = jnp.bfloat16
I32 = jnp.int32

D_MODEL = 1024
N_HEADS = 8
HEAD_DIM = 64
N_KV_HEADS = 2
ATTN_DIM = N_HEADS * HEAD_DIM
KV_DIM = N_KV_HEADS * HEAD_DIM
N_IDX_HEADS = 4
IDX_DIM = 64
TOPK_MAX = 256
CONV_DIM = D_MODEL - ATTN_DIM
CONV_WIDTH = 3
D_FF = 2816
PLE_DIM = 256
PAGE_SIZE = 128
ROPE_THETA = 10000.0
RMS_EPS = 1e-6

ROW_TILE = 64
LANES = 128
SUBLANES = 8
VMEM_LIMIT = 56 * 1024 * 1024

INT_MIN = -(2 ** 31)
MASKED_SCORE = -1e38
MIN_THRESHOLD = -5e37
SEARCH_MARGIN = 1 << 16
NEG_BIG = -1e30
LOG2E = 1.4426950408889634

_Q0, _K0, _V0, _IQ0, _IK0, _CB0, _CC0, _CH0, _PROJ_W = 0, 512, 640, 768, 1024, 1152, 1664, 2176, 2688
_HEAD_PERM = (0, 4, 1, 5, 2, 6, 3, 7)


def _cparams(sem):
    return pltpu.CompilerParams(dimension_semantics=sem, vmem_limit_bytes=VMEM_LIMIT)


def _dot(a, b):
    return jnp.dot(a, b, preferred_element_type=F32)


def _dot_t(a, b):
    return lax.dot_general(a, b, (((1,), (1,)), ((), ())), preferred_element_type=F32)


def _rms(x, g):
    ms = jnp.mean(x * x, axis=-1, keepdims=True)
    return x * lax.rsqrt(ms + RMS_EPS) * g


def _ffn_body(x_ref, g_ref, wg_ref, wu_ref, wd_ref, o_ref, h_ref, acc_ref):
    j = pl.program_id(1)

    @pl.when(j == 0)
    def _():
        h_ref[...] = _rms(x_ref[...], g_ref[...]).astype(BF16)

    h = h_ref[...]
    gate = _dot(h, wg_ref[...])
    up = _dot(h, wu_ref[...])
    act = (gate * jax.nn.sigmoid(gate) * up).astype(BF16)
    part = _dot(act, wd_ref[...])

    @pl.when(j == 0)
    def _():
        acc_ref[...] = part

    @pl.when(j > 0)
    def _():
        acc_ref[...] += part

    @pl.when(j == pl.num_programs(1) - 1)
    def _():
        o_ref[...] = x_ref[...] + 0.5 * acc_ref[...]


def _ffn(x, g, wg, wu, wd, tm):
    n = x.shape[0]
    ff_chunk = D_FF // 2
    return pl.pallas_call(
        _ffn_body,
        grid=(n // tm, D_FF // ff_chunk),
        in_specs=[
            pl.BlockSpec((tm, D_MODEL), lambda i, j: (i, 0)),
            pl.BlockSpec((1, D_MODEL), lambda i, j: (0, 0)),
            pl.BlockSpec((D_MODEL, ff_chunk), lambda i, j: (0, j)),
            pl.BlockSpec((D_MODEL, ff_chunk), lambda i, j: (0, j)),
            pl.BlockSpec((ff_chunk, D_MODEL), lambda i, j: (j, 0)),
        ],
        out_specs=pl.BlockSpec((tm, D_MODEL), lambda i, j: (i, 0)),
        out_shape=jax.ShapeDtypeStruct((n, D_MODEL), F32),
        scratch_shapes=[pltpu.VMEM((tm, D_MODEL), BF16), pltpu.VMEM((tm, D_MODEL), F32)],
        compiler_params=_cparams(("arbitrary", "arbitrary")),
        name="ffn",
    )(x, g, wg, wu, wd)


def _swap32(x):
    w = x.shape[-1]
    lane = lax.broadcasted_iota(I32, x.shape, 1)
    return jnp.where((lane & 63) < 32, pltpu.roll(x, w - 32, 1), pltpu.roll(x, 32, 1))


def _rope_cols(z, cos, sin):
    cols = []
    for c in range(z.shape[-1] // LANES):
        zc = z[:, c * LANES:(c + 1) * LANES]
        cols.append(zc * cos + _swap32(zc) * sin)
    return cols[0] if len(cols) == 1 else jnp.concatenate(cols, axis=1)


def _proj_body(*refs, tm, tiles_per_seq, halo):
    if halo:
        (x_ref, g_ref, w_ref, cos_ref, sin_ref, cosk_ref, sink_ref, cw_ref, h1_ref, h2_ref,
         q_ref, iq_ref, k_ref, v_ref, ikw_ref, kb_ref, vb_ref, ik2_ref, cy_ref, u_ref) = refs
    else:
        (x_ref, g_ref, w_ref, cos_ref, sin_ref, cosk_ref, sink_ref, cw_ref,
         q_ref, iq_ref, k_ref, v_ref, ikw_ref, kb_ref, vb_ref, ik2_ref, cy_ref, u_ref, carry_ref) = refs

    h = _rms(x_ref[...], g_ref[...]).astype(BF16)
    cos, sin = cos_ref[...], sin_ref[...]

    zq = _dot(h, w_ref[:, _Q0:_K0])
    q_ref[...] = (_rope_cols(zq, cos, sin) * (HEAD_DIM ** -0.5 * LOG2E)).astype(BF16)
    kr = _rope_cols(_dot(h, w_ref[:, _K0:_V0]), cos, sin)
    k_ref[...] = kr if halo else kr.T
    kb_ref[...] = kr.astype(BF16)
    zv = _dot(h, w_ref[:, _V0:_IQ0])
    v_ref[...] = zv if halo else zv.T
    vb_ref[...] = jnp.concatenate([zv, jnp.ones_like(zv)], axis=1).astype(BF16)
    ziq = _dot(h, w_ref[:, _IQ0:_IK0])
    iq_ref[...] = (_rope_cols(ziq, cos, sin) * (IDX_DIM ** -0.5)).astype(BF16)
    ikr = _rope_cols(_dot(h, w_ref[:, _IK0:_CB0]), cosk_ref[...], sink_ref[...])
    ikw_ref[...] = ikr
    lane = lax.broadcasted_iota(I32, ikr.shape, 1)
    ik2_ref[...] = jnp.where(lane < IDX_DIM, ikr, pltpu.roll(ikr, IDX_DIM, 1)).astype(BF16)

    cb = _dot(h, w_ref[:, _CB0:_CC0])
    u = _dot(h, w_ref[:, _CC0:_CH0]) * _dot(h, w_ref[:, _CH0:_PROJ_W])
    row = lax.broadcasted_iota(I32, u.shape, 0)
    r1 = pltpu.roll(u, 1, 0)
    r2 = pltpu.roll(u, 2, 0)
    if halo:
        t = row & (SUBLANES - 1)
        us1 = jnp.where(t == 0, h1_ref[...], r1)
        us2 = jnp.where(t < 2, h2_ref[...], r2)
        u_ref[...] = u
    else:
        @pl.when(pl.program_id(0) % tiles_per_seq == 0)
        def _():
            carry_ref[...] = jnp.zeros_like(carry_ref)

        c0 = carry_ref[SUBLANES - 2:SUBLANES - 1, :]
        c1 = carry_ref[SUBLANES - 1:SUBLANES, :]
        us1 = jnp.where(row == 0, c1, r1)
        us2 = jnp.where(row == 0, c0, jnp.where(row == 1, c1, r2))
        tail = u[tm - SUBLANES:tm, :]
        carry_ref[...] = tail
        u_ref[0] = tail
    cw = cw_ref[...]
    y = cb * (cw[0:1, :] * us2 + cw[1:2, :] * us1 + cw[2:3, :] * u)
    cy_ref[...] = y.astype(BF16)


def _proj(x, g, w, cos, sin, cosk, sink, cw, tm, tiles_per_seq, halos=None):
    n = x.shape[0]
    n_tiles = n // tm
    halo = halos is not None
    tok = lambda width: pl.BlockSpec((tm, width), lambda i: (i, 0))
    tab = pl.BlockSpec((tm, LANES), lambda i: (i % tiles_per_seq, 0))
    full = lambda a: pl.BlockSpec(a.shape, lambda i: (0,) * a.ndim)
    in_specs = [tok(D_MODEL), full(g), full(w), tab, tab, tab, tab, full(cw)]
    args = [x, g, w, cos, sin, cosk, sink, cw]
    out_shape = [
        jax.ShapeDtypeStruct((n, ATTN_DIM), BF16),
        jax.ShapeDtypeStruct((n, N_IDX_HEADS * IDX_DIM), BF16),
        jax.ShapeDtypeStruct((n, KV_DIM), F32),
        jax.ShapeDtypeStruct((n, KV_DIM), F32),
        jax.ShapeDtypeStruct((n, LANES), F32),
        jax.ShapeDtypeStruct((n, KV_DIM), BF16),
        jax.ShapeDtypeStruct((n, 2 * KV_DIM), BF16),
        jax.ShapeDtypeStruct((n, LANES), BF16),
        jax.ShapeDtypeStruct((n, CONV_DIM), BF16),
    ]
    out_specs = [tok(ATTN_DIM), tok(N_IDX_HEADS * IDX_DIM), tok(KV_DIM), tok(KV_DIM), tok(LANES),
                 tok(KV_DIM), tok(2 * KV_DIM), tok(LANES), tok(CONV_DIM)]
    scratch = []
    if halo:
        in_specs += [tok(CONV_DIM), tok(CONV_DIM)]
        args += list(halos)
        out_shape.append(jax.ShapeDtypeStruct((n, CONV_DIM), F32))
        out_specs.append(tok(CONV_DIM))
    else:
        n_seq = n_tiles // tiles_per_seq
        for slot in (2, 3):
            out_shape[slot] = jax.ShapeDtypeStruct((n_seq, KV_DIM, tiles_per_seq * tm), F32)
            out_specs[slot] = pl.BlockSpec((None, KV_DIM, tm), lambda i: (i // tiles_per_seq, 0, i % tiles_per_seq))
        out_shape.append(jax.ShapeDtypeStruct((n_seq, SUBLANES, CONV_DIM), F32))
        out_specs.append(pl.BlockSpec((1, SUBLANES, CONV_DIM), lambda i: (i // tiles_per_seq, 0, 0)))
        scratch.append(pltpu.VMEM((SUBLANES, CONV_DIM), F32))
    return pl.pallas_call(
        functools.partial(_proj_body, tm=tm, tiles_per_seq=tiles_per_seq, halo=halo),
        grid=(n_tiles,),
        in_specs=in_specs,
        out_specs=out_specs,
        out_shape=out_shape,
        scratch_shapes=scratch,
        compiler_params=_cparams(("arbitrary",)),
        name="proj_sample" if halo else "proj_prompt",
    )(*args)


def _lane_fold(ind):
    acc = ind[:, 0:LANES]
    for j in range(1, ind.shape[-1] // LANES):
        acc = acc + ind[:, j * LANES:(j + 1) * LANES]
    return acc


def _key_to_f32(key):
    return lax.bitcast_convert_type(key ^ ((key >> 31) & 0x7FFFFFFF), F32)


def _select_threshold(count_ge, rows, topk):
    def step(b, t):
        cand = t + lax.shift_left(jnp.int32(1), jnp.int32(31) - jnp.asarray(b, I32))
        return jnp.where(count_ge(_key_to_f32(cand)) >= topk, cand, t)

    return _key_to_f32(lax.fori_loop(0, 32, step, jnp.full((rows, 1), INT_MIN, I32)))


def _f32_to_key(x):
    bits = lax.bitcast_convert_type(x, I32)
    return bits ^ ((bits >> 31) & 0x7FFFFFFF)


def _bisect_threshold(count_ge, lo, hi, topk):
    width = hi - lo
    steps = jnp.max((32 - lax.clz(width)).astype(F32)).astype(I32)

    def step(_, carry):
        lo, hi = carry
        mid = lo + lax.shift_right_logical(hi - lo + 1, jnp.int32(1))
        ok = count_ge(_key_to_f32(mid)) >= topk
        return jnp.where(ok, mid, lo), jnp.where(ok, hi, mid - 1)

    lo, _ = lax.fori_loop(0, steps, step, (lo, hi))
    return _key_to_f32(lo)


def _tie_cutoff(count_fn, t, need, idx_bits):
    def step(b, c):
        cand = c + lax.shift_left(jnp.int32(1), jnp.int32(idx_bits - 1) - jnp.asarray(b, I32))
        cnt = count_fn(lambda s, col: (s == t) & (col < cand))
        return jnp.where(cnt <= need, cand, c)

    return lax.fori_loop(0, idx_bits, step, jnp.zeros(t.shape, I32))


def _attn_prompt_body(q_ref, iq_ref, w_ref, ik2_ref, kb_ref, vb_ref, tri_ref, o_ref,
                      sc_ref, qs_ref, iqs_ref, s0_ref, s1_ref, p_ref, bias_ref,
                      m_ref, a_ref, acc_ref,
                      *, qb, kc, topk, n_chunks):
    i = (n_chunks - 1) * (kc // qb) + pl.program_id(1)
    lo = lax.broadcasted_iota(I32, (qb, LANES), 1) < HEAD_DIM

    for c in range(N_HEADS // 2):
        qc = q_ref[:, c * LANES:(c + 1) * LANES]
        qs_ref[c * qb:(c + 1) * qb, :] = jnp.where(lo, qc, jnp.zeros_like(qc))
        qs_ref[(c + 4) * qb:(c + 5) * qb, :] = jnp.where(lo, jnp.zeros_like(qc), qc)
    for c in range(N_IDX_HEADS // 2):
        ic = iq_ref[:, c * LANES:(c + 1) * LANES]
        iqs_ref[(2 * c) * qb:(2 * c + 1) * qb, :] = jnp.where(lo, ic, jnp.zeros_like(ic))
        iqs_ref[(2 * c + 1) * qb:(2 * c + 2) * qb, :] = jnp.where(lo, jnp.zeros_like(ic), ic)

    row_t = lax.broadcasted_iota(I32, (ROW_TILE, kc), 0)
    col_t = lax.broadcasted_iota(I32, (ROW_TILE, kc), 1)

    def score_chunk(c, carry):
        off = pl.multiple_of(c * kc, kc)
        s0_ref[0:N_IDX_HEADS * qb, :] = _dot_t(iqs_ref[...], ik2_ref[pl.ds(off, kc), :])
        for r0 in range(0, qb, ROW_TILE):
            r = slice(r0, r0 + ROW_TILE)
            wt = w_ref[r, :] * (N_IDX_HEADS ** -0.5)
            sc = wt[:, IDX_DIM:IDX_DIM + 1] * jnp.maximum(s0_ref[r, :], 0.0)
            for h in range(1, N_IDX_HEADS):
                sc = sc + (wt[:, IDX_DIM + h:IDX_DIM + h + 1]
                           * jnp.maximum(s0_ref[h * qb + r0:h * qb + r0 + ROW_TILE, :], 0.0))
            sc_ref[c, r, :] = jnp.where(off + col_t <= i * qb + r0 + row_t, sc, MASKED_SCORE)
        return carry

    lax.fori_loop(0, n_chunks, score_chunk, 0)

    def kth_largest_score(n):
        def count_ge(cand):
            cw = jnp.concatenate([jnp.broadcast_to(cand, (qb, LANES))] * (kc // LANES), axis=1)
            acc = jnp.zeros((qb, LANES), F32)
            for c in range(n):
                acc = acc + _lane_fold(jnp.where(sc_ref[c] >= cw, 1.0, 0.0))
            return jnp.sum(acc, axis=1, keepdims=True)

        if topk > 2 * LANES:
            return _select_threshold(count_ge, qb, float(topk))

        m1 = jnp.full((qb, LANES), MASKED_SCORE, F32)
        m2 = jnp.full((qb, LANES), MASKED_SCORE, F32)
        for c in range(n):
            for j in range(kc // LANES):
                x = sc_ref[c, :, j * LANES:(j + 1) * LANES]
                m1, m2 = jnp.maximum(m1, x), jnp.maximum(m2, jnp.minimum(m1, x))
        lo = _f32_to_key(jnp.min(m2, axis=1, keepdims=True)) - SEARCH_MARGIN
        hi = _f32_to_key(jnp.max(m1, axis=1, keepdims=True)) + SEARCH_MARGIN
        return _bisect_threshold(count_ge, lo, hi, float(topk))

    def count_scores(pred):
        def body(c, acc):
            return acc + _lane_fold(jnp.where(pred(sc_ref[c]), 1.0, 0.0))
        acc = lax.fori_loop(0, n_chunks, body, jnp.zeros((qb, LANES), F32))
        return jnp.sum(acc, axis=1, keepdims=True)

    thr = kth_largest_score(n_chunks)
    thr = jnp.maximum(thr, MIN_THRESHOLD)
    over = count_scores(lambda s: s >= thr) > float(topk)

    @pl.when(jnp.max(jnp.where(over, 1.0, 0.0)) > 0.0)
    def _():
        need = float(topk) - count_scores(lambda s: s > thr)

        def drop(c, seen):
            s = sc_ref[c]
            tie = s == thr
            ind = jnp.where(tie, 1.0, 0.0)
            before = seen + _dot(ind.astype(BF16), tri_ref[...])
            sc_ref[c] = jnp.where(tie & (before >= need), MASKED_SCORE, s)
            return seen + jnp.sum(ind, axis=1, keepdims=True)

        lax.fori_loop(0, n_chunks, drop, jnp.zeros((qb, 1), F32))

    m_ref[...] = jnp.full(m_ref.shape, NEG_BIG, F32)
    acc_ref[...] = jnp.zeros(acc_ref.shape, F32)

    def key_offset(c):
        return c * kc if isinstance(c, int) else pl.multiple_of(c * kc, kc)

    def logits(c):
        return _dot_t(qs_ref[...], kb_ref[pl.ds(key_offset(c), kc), :])

    def softmax_pv(c, s_ref):
        off = key_offset(c)
        bias_ref[...] = jnp.where(sc_ref[c] >= thr, 0.0, NEG_BIG)
        for r0 in range(0, N_HEADS * qb, ROW_TILE):
            r = slice(r0, r0 + ROW_TILE)
            s = s_ref[r, :] + bias_ref[r0 % qb:r0 % qb + ROW_TILE, :]
            m_old = m_ref[r, :]
            m_new = jnp.maximum(m_old, jnp.max(s, axis=1, keepdims=True))
            p_ref[r, :] = jnp.exp2(s - jnp.concatenate([m_new] * (kc // LANES), axis=1)).astype(BF16)
            a_ref[r, :] = jnp.exp2(m_old - m_new)
            m_ref[r, :] = m_new
        alpha = a_ref[...]
        acc_ref[...] = (jnp.concatenate([alpha, alpha], axis=1) * acc_ref[...]
                        + _dot(p_ref[...], vb_ref[pl.ds(off, kc), :]))

    s0_ref[...] = logits(0)

    def chunk_pair(j, carry):
        c0 = 2 * j
        s1_ref[...] = logits(c0 + 1)
        softmax_pv(c0, s0_ref)
        s0_ref[...] = logits(c0 + 2)
        softmax_pv(c0 + 1, s1_ref)
        return carry

    full_pairs = (n_chunks - 1) // 2
    lax.fori_loop(0, full_pairs, chunk_pair, 0)
    if n_chunks % 2 == 0:
        s1_ref[...] = logits(n_chunks - 1)
        softmax_pv(n_chunks - 2, s0_ref)
        softmax_pv(n_chunks - 1, s1_ref)
    else:
        softmax_pv(n_chunks - 1, s0_ref)

    out = acc_ref[:, 0:LANES] / acc_ref[:, LANES:2 * LANES]
    for c in range(N_HEADS // 2):
        oc = jnp.where(lo, out[c * qb:(c + 1) * qb], out[(c + 4) * qb:(c + 5) * qb])
        o_ref[:, c * LANES:(c + 1) * LANES] = oc.astype(BF16)


def _attn_prompt(q, iq, ikw, ik2, kb, vb, batch, seq, qb, kc):
    nqb = seq // qb
    per_call = kc // qb
    topk = min(TOPK_MAX, seq // 4)
    as_seq = lambda a: a.reshape(batch, seq, a.shape[-1])
    ik2, kb, vb = as_seq(ik2), as_seq(kb), as_seq(vb)
    tri = jnp.triu(jnp.ones((kc, kc), BF16), k=1)
    outs = []
    for n in range(1, seq // kc + 1):
        outs.append(_attn_prompt_call(q, iq, ikw, ik2, kb, vb, tri, batch, qb, kc, topk, n, nqb, per_call))
    out = jnp.concatenate([o.reshape(batch, per_call * qb, ATTN_DIM) for o in outs], axis=1)
    return out.reshape(batch * seq, ATTN_DIM)


def _attn_prompt_call(q, iq, ikw, ik2, kb, vb, tri, batch, qb, kc, topk, n, nqb, per_call):
    i0 = (n - 1) * per_call
    tokq = lambda width: pl.BlockSpec((qb, width), lambda b, i: (b * nqb + i0 + i, 0))
    seqk = lambda width: pl.BlockSpec((None, n * kc, width), lambda b, i: (b, 0, 0))
    return pl.pallas_call(
        functools.partial(_attn_prompt_body, qb=qb, kc=kc, topk=topk, n_chunks=n),
        grid=(batch, per_call),
        in_specs=[tokq(ATTN_DIM), tokq(N_IDX_HEADS * IDX_DIM), tokq(LANES),
                  seqk(LANES), seqk(KV_DIM), seqk(2 * KV_DIM),
                  pl.BlockSpec((kc, kc), lambda b, i: (0, 0))],
        out_specs=pl.BlockSpec((qb, ATTN_DIM), lambda b, i: (b * per_call + i, 0)),
        out_shape=jax.ShapeDtypeStruct((batch * per_call * qb, ATTN_DIM), BF16),
        scratch_shapes=[
            pltpu.VMEM((n, qb, kc), F32),
            pltpu.VMEM((N_HEADS * qb, LANES), BF16),
            pltpu.VMEM((N_IDX_HEADS * qb, LANES), BF16),
            pltpu.VMEM((N_HEADS * qb, kc), F32),
            pltpu.VMEM((N_HEADS * qb, kc), F32),
            pltpu.VMEM((N_HEADS * qb, kc), BF16),
            pltpu.VMEM((qb, kc), F32),
            pltpu.VMEM((N_HEADS * qb, LANES), F32),
            pltpu.VMEM((N_HEADS * qb, LANES), F32),
            pltpu.VMEM((N_HEADS * qb, 2 * KV_DIM), F32),
        ],
        compiler_params=_cparams(("arbitrary", "arbitrary")),
        name=f"attn_prompt_{n}",
    )(q, iq, ikw, ik2, kb, vb, tri)


def _stack_heads(x_bf, n_heads):
    x = x_bf.astype(F32)
    return jnp.concatenate([x[:, h * HEAD_DIM:(h + 1) * HEAD_DIM] for h in range(n_heads)], axis=0).astype(BF16)


def _weighted_relu_sum(logits, wcol, t):
    sc = wcol[0] * jnp.maximum(logits[0:t], 0.0)
    for h in range(1, N_IDX_HEADS):
        sc = sc + wcol[h] * jnp.maximum(logits[h * t:(h + 1) * t], 0.0)
    return sc


def _head_weights(w_ref):
    wq = w_ref[...]
    return [wq[:, IDX_DIM + h:IDX_DIM + h + 1] * (N_IDX_HEADS ** -0.5) for h in range(N_IDX_HEADS)]


def _sample_scores_body(pt_ref, iq_ref, w_ref, *refs, pg, t):
    del pt_ref
    pages = refs[:pg]
    keys_ref, nkeys_ref, ikt_ref = refs[pg:]
    iqs = _stack_heads(iq_ref[...], N_IDX_HEADS)
    wcol = _head_weights(w_ref)
    for j in range(pg):
        ikt_ref[:, j * PAGE_SIZE:(j + 1) * PAGE_SIZE] = pages[j][...].astype(BF16)
    logits = _dot(iqs, ikt_ref[...])
    keys_ref[...] = _weighted_relu_sum(logits, wcol, t)

    @pl.when(pl.program_id(1) == 0)
    def _():
        ikn = jnp.concatenate([w_ref[...][:, 0:IDX_DIM], jnp.zeros((LANES - t, IDX_DIM), F32)], axis=0)
        sc = _weighted_relu_sum(_dot_t(iqs, ikn.astype(BF16)), wcol, t)
        col_n = lax.broadcasted_iota(I32, (t, LANES), 1)
        row_n = lax.broadcasted_iota(I32, (t, LANES), 0)
        nkeys_ref[...] = jnp.where(col_n <= row_n, sc, MASKED_SCORE)


def _sample_scores(page_table, iq, ikw, cache_idx_t, db, t, pg):
    n_pages = page_table.shape[1]
    page_spec = lambda j: pl.BlockSpec((None, IDX_DIM, PAGE_SIZE),
                                       lambda b, g, pt: (pt[b, g * pg + j], 0, 0))
    grid_spec = pltpu.PrefetchScalarGridSpec(
        num_scalar_prefetch=1,
        grid=(db, n_pages // pg),
        in_specs=[pl.BlockSpec((t, N_IDX_HEADS * IDX_DIM), lambda b, g, pt: (b, 0)),
                  pl.BlockSpec((t, LANES), lambda b, g, pt: (b, 0))] + [page_spec(j) for j in range(pg)],
        out_specs=[pl.BlockSpec((t, pg * PAGE_SIZE), lambda b, g, pt: (b, g)),
                   pl.BlockSpec((t, LANES), lambda b, g, pt: (b, 0))],
        scratch_shapes=[pltpu.VMEM((IDX_DIM, pg * PAGE_SIZE), BF16)],
    )
    return pl.pallas_call(
        functools.partial(_sample_scores_body, pg=pg, t=t),
        grid_spec=grid_spec,
        out_shape=[jax.ShapeDtypeStruct((db * t, n_pages * PAGE_SIZE), F32),
                   jax.ShapeDtypeStruct((db * t, LANES), F32)],
        compiler_params=_cparams(("arbitrary", "arbitrary")),
        name="sample_scores",
    )(page_table, iq, ikw, *([cache_idx_t] * pg))


def _sample_thr_body(keys_ref, nkeys_ref, thr_ref, cut_ref, *, rows, topk, idx_bits, cw):
    past = keys_ref.shape[-1]
    col_c = lax.broadcasted_iota(I32, (rows, cw), 1)
    col_n = lax.broadcasted_iota(I32, (rows, LANES), 1)

    def count_keys(pred):
        acc = jnp.where(pred(nkeys_ref[...]), 1.0, 0.0)
        for c in range(past // cw):
            acc = acc + _lane_fold(jnp.where(pred(keys_ref[:, c * cw:(c + 1) * cw]), 1.0, 0.0))
        return jnp.sum(acc, axis=1, keepdims=True)

    def count_keys_cols(pred):
        acc = jnp.where(pred(nkeys_ref[...], past + col_n), 1.0, 0.0)
        for c in range(past // cw):
            acc = acc + _lane_fold(jnp.where(pred(keys_ref[:, c * cw:(c + 1) * cw], c * cw + col_c), 1.0, 0.0))
        return jnp.sum(acc, axis=1, keepdims=True)

    thr = _select_threshold(lambda cand: count_keys(lambda kk: kk >= cand), rows, float(topk))
    thr = jnp.maximum(thr, MIN_THRESHOLD)
    over = count_keys(lambda kk: kk >= thr) > float(topk)
    thr_ref[...] = jnp.broadcast_to(thr, (rows, LANES))
    cut_ref[...] = jnp.full((rows, LANES), 1 << idx_bits, I32)

    @pl.when(jnp.max(jnp.where(over, 1.0, 0.0)) > 0.0)
    def _():
        need = float(topk) - count_keys(lambda kk: kk > thr)
        cut = _tie_cutoff(count_keys_cols, thr, need, idx_bits)
        cut_ref[...] = jnp.broadcast_to(jnp.where(over, cut, jnp.int32(1 << idx_bits)), (rows, LANES))


def _sample_thr(keys, nkeys, t, rows):
    n, past = keys.shape
    topk = min(TOPK_MAX, (past + t) // 4)
    blk = lambda width: pl.BlockSpec((rows, width), lambda i: (i, 0))
    return pl.pallas_call(
        functools.partial(_sample_thr_body, rows=rows, topk=topk,
                          idx_bits=(past + LANES - 1).bit_length(), cw=2048),
        grid=(n // rows,),
        in_specs=[blk(past), blk(LANES)],
        out_specs=[blk(LANES), blk(LANES)],
        out_shape=[jax.ShapeDtypeStruct((n, LANES), F32), jax.ShapeDtypeStruct((n, LANES), I32)],
        compiler_params=_cparams(("arbitrary",)),
        name="sample_thr",
    )(keys, nkeys)


def _sample_attn_body(pt_ref, keys_ref, nkeys_ref, thr_ref, cut_ref, q_ref, kn_ref, vn_ref, *refs,
                      pg, t, past):
    del pt_ref
    kpages, vpages = refs[:pg], refs[pg:2 * pg]
    o_ref = refs[2 * pg]
    qs_ref, kt_ref, vt_ref, m_ref, l_ref, acc_ref = refs[2 * pg + 1:]
    g = pl.program_id(1)
    rows = N_HEADS * t

    def flash_update(s, sel, pv):
        n = s.shape[-1]
        s = jnp.where(sel[None], s.reshape(N_HEADS, t, n), NEG_BIG).reshape(rows, n)
        m_old = m_ref[...]
        m_new = jnp.maximum(m_old, jnp.max(s, axis=1, keepdims=True))
        alpha = jnp.exp2(m_old - m_new)
        p = jnp.exp2(s - m_new)
        l_ref[...] = alpha * l_ref[...] + jnp.sum(p, axis=1, keepdims=True)
        acc_ref[...] = alpha * acc_ref[...] + pv(p.astype(BF16))
        m_ref[...] = m_new

    @pl.when(g == 0)
    def _():
        lo = lax.broadcasted_iota(I32, (t, LANES), 1) < HEAD_DIM
        for c in range(N_HEADS // 2):
            qc = q_ref[:, c * LANES:(c + 1) * LANES]
            qs_ref[c * t:(c + 1) * t, :] = jnp.where(lo, qc, jnp.zeros_like(qc))
            qs_ref[(c + 4) * t:(c + 5) * t, :] = jnp.where(lo, jnp.zeros_like(qc), qc)
        m_ref[...] = jnp.full(m_ref.shape, NEG_BIG, F32)
        l_ref[...] = jnp.zeros(l_ref.shape, F32)
        acc_ref[...] = jnp.zeros(acc_ref.shape, F32)

    thr = thr_ref[:, 0:1]
    cut = cut_ref[:, 0:1]

    def selected(kk, col):
        return (kk > thr) | ((kk == thr) & (col < cut))

    qs = qs_ref[...]
    width = pg * PAGE_SIZE
    for j in range(pg):
        kt_ref[:, j * PAGE_SIZE:(j + 1) * PAGE_SIZE] = kpages[j][...].astype(BF16)
        vt_ref[:, j * PAGE_SIZE:(j + 1) * PAGE_SIZE] = vpages[j][...].astype(BF16)
    col_w = g * width + lax.broadcasted_iota(I32, (t, width), 1)
    flash_update(_dot(qs, kt_ref[...]), selected(keys_ref[...], col_w),
                 lambda p: _dot_t(p, vt_ref[...]))

    @pl.when(g == pl.num_programs(1) - 1)
    def _():
        pad = jnp.zeros((LANES - t, LANES), F32)
        kn = jnp.concatenate([kn_ref[...], pad], axis=0).astype(BF16)
        vn = jnp.concatenate([vn_ref[...], pad], axis=0).astype(BF16)
        col_n = past + lax.broadcasted_iota(I32, (t, LANES), 1)
        flash_update(_dot_t(qs, kn), selected(nkeys_ref[...], col_n), lambda p: _dot(p, vn))
        lo = lax.broadcasted_iota(I32, (t, LANES), 1) < HEAD_DIM
        out = acc_ref[...] / l_ref[...]
        for c in range(N_HEADS // 2):
            oc = jnp.where(lo, out[c * t:(c + 1) * t], out[(c + 4) * t:(c + 5) * t])
            o_ref[:, c * LANES:(c + 1) * LANES] = oc.astype(BF16)


def _sample_attn(page_table, keys, nkeys, thr, cut, q, k_new, v_new, cache_kt, cache_vt, db, t, pg):
    n_pages = page_table.shape[1]
    past = n_pages * PAGE_SIZE
    page_spec = lambda j: pl.BlockSpec((None, KV_DIM, PAGE_SIZE),
                                       lambda b, g, pt: (pt[b, g * pg + j], 0, 0))
    tok = lambda width: pl.BlockSpec((t, width), lambda b, g, pt: (b, 0))
    grid_spec = pltpu.PrefetchScalarGridSpec(
        num_scalar_prefetch=1,
        grid=(db, n_pages // pg),
        in_specs=[pl.BlockSpec((t, pg * PAGE_SIZE), lambda b, g, pt: (b, g)),
                  tok(LANES), tok(LANES), tok(LANES), tok(ATTN_DIM), tok(KV_DIM), tok(KV_DIM)]
                 + [page_spec(j) for j in range(pg)] + [page_spec(j) for j in range(pg)],
        out_specs=tok(ATTN_DIM),
        scratch_shapes=[
            pltpu.VMEM((N_HEADS * t, LANES), BF16),
            pltpu.VMEM((KV_DIM, pg * PAGE_SIZE), BF16),
            pltpu.VMEM((KV_DIM, pg * PAGE_SIZE), BF16),
            pltpu.VMEM((N_HEADS * t, 1), F32),
            pltpu.VMEM((N_HEADS * t, 1), F32),
            pltpu.VMEM((N_HEADS * t, LANES), F32),
        ],
    )
    return pl.pallas_call(
        functools.partial(_sample_attn_body, pg=pg, t=t, past=past),
        grid_spec=grid_spec,
        out_shape=jax.ShapeDtypeStruct((db * t, ATTN_DIM), BF16),
        compiler_params=_cparams(("arbitrary", "arbitrary")),
        name="sample_attn",
    )(page_table, keys, nkeys, thr, cut, q, k_new, v_new, *([cache_kt] * pg), *([cache_vt] * pg))


def _out_body(x_ref, a_ref, c_ref, wa_ref, wc_ref, o_ref):
    o_ref[...] = x_ref[...] + _dot(a_ref[...], wa_ref[...]) + _dot(c_ref[...], wc_ref[...])


def _out_proj(x, attn, conv, wa, wc, tm):
    n = x.shape[0]
    tok = lambda width: pl.BlockSpec((tm, width), lambda i: (i, 0))
    full = lambda a: pl.BlockSpec(a.shape, lambda i: (0, 0))
    return pl.pallas_call(
        _out_body,
        grid=(n // tm,),
        in_specs=[tok(D_MODEL), tok(ATTN_DIM), tok(CONV_DIM), full(wa), full(wc)],
        out_specs=tok(D_MODEL),
        out_shape=jax.ShapeDtypeStruct((n, D_MODEL), F32),
        compiler_params=_cparams(("arbitrary",)),
        name="out_proj",
    )(x, attn, conv, wa, wc)


def _ple_body(x_ref, pe_ref, g_ref, wg_ref, wp_ref, gf_ref, o_ref):
    x = x_ref[...]
    gate = jax.nn.sigmoid(_dot(_rms(x, g_ref[...]).astype(BF16), wg_ref[...]))
    x = x + gate * _dot(pe_ref[...].astype(BF16), wp_ref[...])
    o_ref[...] = _rms(x, gf_ref[...])


def _ple_final(x, pe, g, wg, wp, gf, tm):
    n = x.shape[0]
    tok = lambda width: pl.BlockSpec((tm, width), lambda i: (i, 0))
    full = lambda a: pl.BlockSpec(a.shape, lambda i: (0, 0))
    return pl.pallas_call(
        _ple_body,
        grid=(n // tm,),
        in_specs=[tok(D_MODEL), tok(PLE_DIM), full(g), full(wg), full(wp), full(gf)],
        out_specs=tok(D_MODEL),
        out_shape=jax.ShapeDtypeStruct((n, D_MODEL), F32),
        compiler_params=_cparams(("arbitrary",)),
        name="ple_final",
    )(x, pe, g, wg, wp, gf)


def _rope_tables(pos):
    half = HEAD_DIM // 2
    inv = ROPE_THETA ** (-jnp.arange(half, dtype=F32) / half)
    ang = pos.astype(F32)[:, None] * inv[None, :]
    cos, sin = jnp.cos(ang), jnp.sin(ang)
    cos2 = jnp.concatenate([cos, cos], axis=1)
    sin2 = jnp.concatenate([-sin, sin], axis=1)
    cos128 = jnp.concatenate([cos2, cos2], axis=1)
    sin128 = jnp.concatenate([sin2, sin2], axis=1)
    cosk = jnp.concatenate([cos2, jnp.ones_like(cos2)], axis=1)
    sink = jnp.concatenate([sin2, jnp.zeros_like(sin2)], axis=1)
    return cos128, sin128, cosk, sink


def _prep_w_in(w_in):
    d = w_in.shape[0]
    q = w_in[:, :ATTN_DIM].reshape(d, N_HEADS, HEAD_DIM)[:, jnp.array(_HEAD_PERM)].reshape(d, ATTN_DIM)
    o = ATTN_DIM
    kv = w_in[:, o:o + 2 * KV_DIM]
    o += 2 * KV_DIM
    iq = w_in[:, o:o + N_IDX_HEADS * IDX_DIM]
    o += N_IDX_HEADS * IDX_DIM
    ikw = w_in[:, o:o + IDX_DIM + N_IDX_HEADS]
    o += IDX_DIM + N_IDX_HEADS
    ikw = jnp.pad(ikw, ((0, 0), (0, LANES - IDX_DIM - N_IDX_HEADS)))
    conv = w_in[:, o:]
    return jnp.concatenate([q, kv, iq, ikw, conv], axis=1).astype(BF16)


def _layer(x, pe, mixer, tm, g_ffn1, w1g, w1u, w1d, g_mix, w_in, conv_w, wo_a, wo_c,
           g_ffn2, w2g, w2u, w2d, g_ple, w_pg, w_pp, g_final):
    x1 = _ffn(x, g_ffn1, w1g, w1u, w1d, tm)
    attn, conv, state = mixer(x1, g_mix, w_in, conv_w)
    x2 = _out_proj(x1, attn, conv, wo_a, wo_c, tm)
    x3 = _ffn(x2, g_ffn2, w2g, w2u, w2d, tm)
    return _ple_final(x3, pe, g_ple, w_pg, w_pp, g_final, tm), state


def kernel(x_prompt, x_sample, cache_k, cache_v, cache_idx_k, state_conv, page_table, p_prompt, p_sample,
           g_ffn1, w1_gate, w1_up, w1_down, g_mix, w_in, conv_w, w_out,
           g_ffn2, w2_gate, w2_up, w2_down, g_ple, w_ple_gate, w_ple_proj, g_final):
    depth = w_in.shape[0]
    assert depth == 1, "single-layer step"
    batch, seq, _ = x_prompt.shape
    db, t, _ = x_sample.shape
    n_pages = page_table.shape[1]
    past = n_pages * PAGE_SIZE
    assert t == SUBLANES and seq % 512 == 0

    row = lambda gvec: gvec.reshape(1, -1)
    perm = jnp.array(_HEAD_PERM)
    wo_a = w_out[0, :ATTN_DIM].reshape(N_HEADS, HEAD_DIM, D_MODEL)[perm].reshape(ATTN_DIM, D_MODEL).astype(BF16)
    wo_c = w_out[0, ATTN_DIM:].astype(BF16)
    shared = (row(g_ffn1[0]), w1_gate[0].astype(BF16), w1_up[0].astype(BF16), w1_down[0].astype(BF16),
              row(g_mix[0]), _prep_w_in(w_in[0]), conv_w[0], wo_a, wo_c,
              row(g_ffn2[0]), w2_gate[0].astype(BF16), w2_up[0].astype(BF16), w2_down[0].astype(BF16),
              row(g_ple[0]), w_ple_gate[0].astype(BF16), w_ple_proj[0].astype(BF16), row(g_final))

    tm_p = 512
    tabs_p = _rope_tables(jnp.arange(seq, dtype=I32))

    def prompt_mixer(x1, g, w, cw):
        q, iq, k, v, ikw, kb, vb, ik2, cy, tail = _proj(x1, g, w, *tabs_p, cw, tm_p, seq // tm_p)
        attn = _attn_prompt(q, iq, ikw, ik2, kb, vb, batch, seq, qb=256, kc=512)
        return attn, cy, (k, v, ikw, tail)

    yp, (kp, vp, ikwp, tailp) = _layer(x_prompt.reshape(batch * seq, D_MODEL),
                                       p_prompt[0].reshape(batch * seq, PLE_DIM), prompt_mixer, tm_p, *shared)

    n_s = db * t
    tabs_s = tuple(jnp.tile(a, (db, 1)) for a in _rope_tables(past + jnp.arange(t, dtype=I32)))
    buf = state_conv[0]
    zero = jnp.zeros((db, t - 2, CONV_DIM), F32)
    halo1 = jnp.concatenate([buf[:, 1:2], jnp.zeros((db, t - 1, CONV_DIM), F32)], axis=1).reshape(n_s, CONV_DIM)
    halo2 = jnp.concatenate([buf, zero], axis=1).reshape(n_s, CONV_DIM)
    idx_t = jnp.transpose(cache_idx_k[0], (0, 2, 1))
    k_t = jnp.transpose(cache_k[0], (0, 2, 3, 1)).reshape(-1, KV_DIM, PAGE_SIZE)
    v_t = jnp.transpose(cache_v[0], (0, 2, 3, 1)).reshape(-1, KV_DIM, PAGE_SIZE)

    def sample_mixer(x1, g, w, cw):
        q, iq, k, v, ikw, kb, vb, ik2, cy, u = _proj(x1, g, w, *tabs_s, cw, n_s, 1, halos=(halo1, halo2))
        keys, nkeys = _sample_scores(page_table, iq, ikw, idx_t, db, t, pg=min(32, n_pages))
        thr, cut = _sample_thr(keys, nkeys, t, rows=min(128, n_s))
        attn = _sample_attn(page_table, keys, nkeys, thr, cut, q, k, v, k_t, v_t, db, t, pg=min(16, n_pages))
        return attn, cy, (k, v, ikw, u)

    ys, (ks, vs, ikws, us) = _layer(x_sample.reshape(n_s, D_MODEL), p_sample[0].reshape(n_s, PLE_DIM),
                                    sample_mixer, n_s, *shared)

    return (yp.reshape(batch, seq, D_MODEL),
            ys.reshape(db, t, D_MODEL),
            kp.reshape(batch, N_KV_HEADS, HEAD_DIM, seq).transpose(0, 3, 1, 2)[None],
            vp.reshape(batch, N_KV_HEADS, HEAD_DIM, seq).transpose(0, 3, 1, 2)[None],
            ikwp[:, :IDX_DIM].reshape(1, batch, seq, IDX_DIM),
            tailp[:, SUBLANES - (CONV_WIDTH - 1):][None],
            ks.reshape(1, db, t, N_KV_HEADS, HEAD_DIM),
            vs.reshape(1, db, t, N_KV_HEADS, HEAD_DIM),
            ikws[:, :IDX_DIM].reshape(1, db, t, IDX_DIM),
            us.reshape(db, t, CONV_DIM)[:, t - (CONV_WIDTH - 1):][None])
```

```python
import functools

import jax
import jax.numpy as jnp
from jax import lax
from jax.experimental import pallas as pl
from jax.experimental.pallas import tpu as pltpu

F32 = jnp.float32
BF16 = jnp.bfloat16
I32 = jnp.int32

D_MODEL = 1024
N_HEADS = 8
HEAD_DIM = 64
N_KV_HEADS = 2
ATTN_DIM = N_HEADS * HEAD_DIM
KV_DIM = N_KV_HEADS * HEAD_DIM
N_IDX_HEADS = 4
IDX_DIM = 64
TOPK_MAX = 256
CONV_DIM = D_MODEL - ATTN_DIM
CONV_WIDTH = 3
D_FF = 2816
PLE_DIM = 256
PAGE_SIZE = 128
ROPE_THETA = 10000.0
RMS_EPS = 1e-6

ROW_TILE = 64
LANES = 128
SUBLANES = 8
VMEM_LIMIT = 56 * 1024 * 1024

INT_MIN = -(2 ** 31)
MASKED_SCORE = -1e38
MIN_THRESHOLD = -5e37
NEG_BIG = -1e30
LOG2E = 1.4426950408889634

_Q0, _K0, _V0, _IQ0, _IK0, _CB0, _CC0, _CH0, _PROJ_W = 0, 512, 640, 768, 1024, 1152, 1664, 2176, 2688
_HEAD_PERM = (0, 4, 1, 5, 2, 6, 3, 7)


def _cparams(sem):
    return pltpu.CompilerParams(dimension_semantics=sem, vmem_limit_bytes=VMEM_LIMIT)


def _dot(a, b):
    return jnp.dot(a, b, preferred_element_type=F32)


def _dot_t(a, b):
    return lax.dot_general(a, b, (((1,), (1,)), ((), ())), preferred_element_type=F32)


def _rms(x, g):
    ms = jnp.mean(x * x, axis=-1, keepdims=True)
    return x * lax.rsqrt(ms + RMS_EPS) * g


def _ffn_body(x_ref, g_ref, wg_ref, wu_ref, wd_ref, o_ref, h_ref, acc_ref):
    j = pl.program_id(1)

    @pl.when(j == 0)
    def _():
        h_ref[...] = _rms(x_ref[...], g_ref[...]).astype(BF16)

    h = h_ref[...]
    gate = _dot(h, wg_ref[...])
    up = _dot(h, wu_ref[...])
    act = (gate * jax.nn.sigmoid(gate) * up).astype(BF16)
    part = _dot(act, wd_ref[...])

    @pl.when(j == 0)
    def _():
        acc_ref[...] = part

    @pl.when(j > 0)
    def _():
        acc_ref[...] += part

    @pl.when(j == pl.num_programs(1) - 1)
    def _():
        o_ref[...] = x_ref[...] + 0.5 * acc_ref[...]


def _ffn(x, g, wg, wu, wd, tm):
    n = x.shape[0]
    ff_chunk = D_FF // 2
    return pl.pallas_call(
        _ffn_body,
        grid=(n // tm, D_FF // ff_chunk),
        in_specs=[
            pl.BlockSpec((tm, D_MODEL), lambda i, j: (i, 0)),
            pl.BlockSpec((1, D_MODEL), lambda i, j: (0, 0)),
            pl.BlockSpec((D_MODEL, ff_chunk), lambda i, j: (0, j)),
            pl.BlockSpec((D_MODEL, ff_chunk), lambda i, j: (0, j)),
            pl.BlockSpec((ff_chunk, D_MODEL), lambda i, j: (j, 0)),
        ],
        out_specs=pl.BlockSpec((tm, D_MODEL), lambda i, j: (i, 0)),
        out_shape=jax.ShapeDtypeStruct((n, D_MODEL), F32),
        scratch_shapes=[pltpu.VMEM((tm, D_MODEL), BF16), pltpu.VMEM((tm, D_MODEL), F32)],
        compiler_params=_cparams(("arbitrary", "arbitrary")),
        name="ffn",
    )(x, g, wg, wu, wd)


def _swap32(x):
    w = x.shape[-1]
    lane = lax.broadcasted_iota(I32, x.shape, 1)
    return jnp.where((lane & 63) < 32, pltpu.roll(x, w - 32, 1), pltpu.roll(x, 32, 1))


def _rope_cols(z, cos, sin):
    cols = []
    for c in range(z.shape[-1] // LANES):
        zc = z[:, c * LANES:(c + 1) * LANES]
        cols.append(zc * cos + _swap32(zc) * sin)
    return cols[0] if len(cols) == 1 else jnp.concatenate(cols, axis=1)


def _proj_body(*refs, tm, tiles_per_seq, halo):
    if halo:
        (x_ref, g_ref, w_ref, cos_ref, sin_ref, cosk_ref, sink_ref, cw_ref, h1_ref, h2_ref,
         q_ref, iq_ref, k_ref, v_ref, ikw_ref, kb_ref, vb_ref, ik2_ref, cy_ref, u_ref) = refs
    else:
        (x_ref, g_ref, w_ref, cos_ref, sin_ref, cosk_ref, sink_ref, cw_ref,
         q_ref, iq_ref, k_ref, v_ref, ikw_ref, kb_ref, vb_ref, ik2_ref, cy_ref, u_ref, carry_ref) = refs

    h = _rms(x_ref[...], g_ref[...]).astype(BF16)
    cos, sin = cos_ref[...], sin_ref[...]

    zq = _dot(h, w_ref[:, _Q0:_K0])
    q_ref[...] = (_rope_cols(zq, cos, sin) * (HEAD_DIM ** -0.5 * LOG2E)).astype(BF16)
    kr = _rope_cols(_dot(h, w_ref[:, _K0:_V0]), cos, sin)
    k_ref[...] = kr if halo else kr.T
    kb_ref[...] = kr.astype(BF16)
    zv = _dot(h, w_ref[:, _V0:_IQ0])
    v_ref[...] = zv if halo else zv.T
    vb_ref[...] = jnp.concatenate([zv, jnp.ones_like(zv)], axis=1).astype(BF16)
    ziq = _dot(h, w_ref[:, _IQ0:_IK0])
    iq_ref[...] = (_rope_cols(ziq, cos, sin) * (IDX_DIM ** -0.5)).astype(BF16)
    ikr = _rope_cols(_dot(h, w_ref[:, _IK0:_CB0]), cosk_ref[...], sink_ref[...])
    ikw_ref[...] = ikr
    lane = lax.broadcasted_iota(I32, ikr.shape, 1)
    ik2_ref[...] = jnp.where(lane < IDX_DIM, ikr, pltpu.roll(ikr, IDX_DIM, 1)).astype(BF16)

    cb = _dot(h, w_ref[:, _CB0:_CC0])
    u = _dot(h, w_ref[:, _CC0:_CH0]) * _dot(h, w_ref[:, _CH0:_PROJ_W])
    row = lax.broadcasted_iota(I32, u.shape, 0)
    r1 = pltpu.roll(u, 1, 0)
    r2 = pltpu.roll(u, 2, 0)
    if halo:
        t = row & (SUBLANES - 1)
        us1 = jnp.where(t == 0, h1_ref[...], r1)
        us2 = jnp.where(t < 2, h2_ref[...], r2)
        u_ref[...] = u
    else:
        @pl.when(pl.program_id(0) % tiles_per_seq == 0)
        def _():
            carry_ref[...] = jnp.zeros_like(carry_ref)

        c0 = carry_ref[SUBLANES - 2:SUBLANES - 1, :]
        c1 = carry_ref[SUBLANES - 1:SUBLANES, :]
        us1 = jnp.where(row == 0, c1, r1)
        us2 = jnp.where(row == 0, c0, jnp.where(row == 1, c1, r2))
        tail = u[tm - SUBLANES:tm, :]
        carry_ref[...] = tail
        u_ref[0] = tail
    cw = cw_ref[...]
    y = cb * (cw[0:1, :] * us2 + cw[1:2, :] * us1 + cw[2:3, :] * u)
    cy_ref[...] = y.astype(BF16)


def _proj(x, g, w, cos, sin, cosk, sink, cw, tm, tiles_per_seq, halos=None):
    n = x.shape[0]
    n_tiles = n // tm
    halo = halos is not None
    tok = lambda width: pl.BlockSpec((tm, width), lambda i: (i, 0))
    tab = pl.BlockSpec((tm, LANES), lambda i: (i % tiles_per_seq, 0))
    full = lambda a: pl.BlockSpec(a.shape, lambda i: (0,) * a.ndim)
    in_specs = [tok(D_MODEL), full(g), full(w), tab, tab, tab, tab, full(cw)]
    args = [x, g, w, cos, sin, cosk, sink, cw]
    out_shape = [
        jax.ShapeDtypeStruct((n, ATTN_DIM), BF16),
        jax.ShapeDtypeStruct((n, N_IDX_HEADS * IDX_DIM), BF16),
        jax.ShapeDtypeStruct((n, KV_DIM), F32),
        jax.ShapeDtypeStruct((n, KV_DIM), F32),
        jax.ShapeDtypeStruct((n, LANES), F32),
        jax.ShapeDtypeStruct((n, KV_DIM), BF16),
        jax.ShapeDtypeStruct((n, 2 * KV_DIM), BF16),
        jax.ShapeDtypeStruct((n, LANES), BF16),
        jax.ShapeDtypeStruct((n, CONV_DIM), BF16),
    ]
    out_specs = [tok(ATTN_DIM), tok(N_IDX_HEADS * IDX_DIM), tok(KV_DIM), tok(KV_DIM), tok(LANES),
                 tok(KV_DIM), tok(2 * KV_DIM), tok(LANES), tok(CONV_DIM)]
    scratch = []
    if halo:
        in_specs += [tok(CONV_DIM), tok(CONV_DIM)]
        args += list(halos)
        out_shape.append(jax.ShapeDtypeStruct((n, CONV_DIM), F32))
        out_specs.append(tok(CONV_DIM))
    else:
        n_seq = n_tiles // tiles_per_seq
        for slot in (2, 3):
            out_shape[slot] = jax.ShapeDtypeStruct((n_seq, KV_DIM, tiles_per_seq * tm), F32)
            out_specs[slot] = pl.BlockSpec((None, KV_DIM, tm), lambda i: (i // tiles_per_seq, 0, i % tiles_per_seq))
        out_shape.append(jax.ShapeDtypeStruct((n_seq, SUBLANES, CONV_DIM), F32))
        out_specs.append(pl.BlockSpec((1, SUBLANES, CONV_DIM), lambda i: (i // tiles_per_seq, 0, 0)))
        scratch.append(pltpu.VMEM((SUBLANES, CONV_DIM), F32))
    return pl.pallas_call(
        functools.partial(_proj_body, tm=tm, tiles_per_seq=tiles_per_seq, halo=halo),
        grid=(n_tiles,),
        in_specs=in_specs,
        out_specs=out_specs,
        out_shape=out_shape,
        scratch_shapes=scratch,
        compiler_params=_cparams(("arbitrary",)),
        name="proj_sample" if halo else "proj_prompt",
    )(*args)


def _lane_fold(ind):
    acc = ind[:, 0:LANES]
    for j in range(1, ind.shape[-1] // LANES):
        acc = acc + ind[:, j * LANES:(j + 1) * LANES]
    return acc


def _key_to_f32(key):
    return lax.bitcast_convert_type(key ^ ((key >> 31) & 0x7FFFFFFF), F32)


def _select_threshold(count_ge, rows, topk):
    def step(b, t):
        cand = t + lax.shift_left(jnp.int32(1), jnp.int32(31) - jnp.asarray(b, I32))
        return jnp.where(count_ge(_key_to_f32(cand)) >= topk, cand, t)

    return _key_to_f32(lax.fori_loop(0, 32, step, jnp.full((rows, 1), INT_MIN, I32)))


def _tie_cutoff(count_fn, t, need, idx_bits):
    def step(b, c):
        cand = c + lax.shift_left(jnp.int32(1), jnp.int32(idx_bits - 1) - jnp.asarray(b, I32))
        cnt = count_fn(lambda s, col: (s == t) & (col < cand))
        return jnp.where(cnt <= need, cand, c)

    return lax.fori_loop(0, idx_bits, step, jnp.zeros(t.shape, I32))


def _attn_prompt_body(q_ref, iq_ref, w_ref, ik2_ref, kb_ref, vb_ref, tri_ref, o_ref,
                      sc_ref, qs_ref, iqs_ref, s0_ref, s1_ref, p_ref, bias_ref,
                      m_ref, a_ref, acc_ref,
                      *, qb, kc, topk, n_chunks):
    i = (n_chunks - 1) * (kc // qb) + pl.program_id(1)
    lo = lax.broadcasted_iota(I32, (qb, LANES), 1) < HEAD_DIM

    for c in range(N_HEADS // 2):
        qc = q_ref[:, c * LANES:(c + 1) * LANES]
        qs_ref[c * qb:(c + 1) * qb, :] = jnp.where(lo, qc, jnp.zeros_like(qc))
        qs_ref[(c + 4) * qb:(c + 5) * qb, :] = jnp.where(lo, jnp.zeros_like(qc), qc)
    for c in range(N_IDX_HEADS // 2):
        ic = iq_ref[:, c * LANES:(c + 1) * LANES]
        iqs_ref[(2 * c) * qb:(2 * c + 1) * qb, :] = jnp.where(lo, ic, jnp.zeros_like(ic))
        iqs_ref[(2 * c + 1) * qb:(2 * c + 2) * qb, :] = jnp.where(lo, jnp.zeros_like(ic), ic)

    row_t = lax.broadcasted_iota(I32, (ROW_TILE, kc), 0)
    col_t = lax.broadcasted_iota(I32, (ROW_TILE, kc), 1)

    def score_chunk(c, carry):
        off = pl.multiple_of(c * kc, kc)
        s0_ref[0:N_IDX_HEADS * qb, :] = _dot_t(iqs_ref[...], ik2_ref[pl.ds(off, kc), :])
        for r0 in range(0, qb, ROW_TILE):
            r = slice(r0, r0 + ROW_TILE)
            wt = w_ref[r, :] * (N_IDX_HEADS ** -0.5)
            sc = wt[:, IDX_DIM:IDX_DIM + 1] * jnp.maximum(s0_ref[r, :], 0.0)
            for h in range(1, N_IDX_HEADS):
                sc = sc + (wt[:, IDX_DIM + h:IDX_DIM + h + 1]
                           * jnp.maximum(s0_ref[h * qb + r0:h * qb + r0 + ROW_TILE, :], 0.0))
            sc_ref[c, r, :] = jnp.where(off + col_t <= i * qb + r0 + row_t, sc, MASKED_SCORE)
        return carry

    lax.fori_loop(0, n_chunks, score_chunk, 0)

    def kth_largest_score(n):
        def count_ge(cand):
            cw = jnp.concatenate([jnp.broadcast_to(cand, (qb, LANES))] * (kc // LANES), axis=1)
            acc = jnp.zeros((qb, LANES), F32)
            for c in range(n):
                acc = acc + _lane_fold(jnp.where(sc_ref[c] >= cw, 1.0, 0.0))
            return jnp.sum(acc, axis=1, keepdims=True)

        return _select_threshold(count_ge, qb, float(topk))

    def count_scores(pred):
        def body(c, acc):
            return acc + _lane_fold(jnp.where(pred(sc_ref[c]), 1.0, 0.0))
        acc = lax.fori_loop(0, n_chunks, body, jnp.zeros((qb, LANES), F32))
        return jnp.sum(acc, axis=1, keepdims=True)

    thr = kth_largest_score(n_chunks)
    thr = jnp.maximum(thr, MIN_THRESHOLD)
    over = count_scores(lambda s: s >= thr) > float(topk)

    @pl.when(jnp.max(jnp.where(over, 1.0, 0.0)) > 0.0)
    def _():
        need = float(topk) - count_scores(lambda s: s > thr)

        def drop(c, seen):
            s = sc_ref[c]
            tie = s == thr
            ind = jnp.where(tie, 1.0, 0.0)
            before = seen + _dot(ind.astype(BF16), tri_ref[...])
            sc_ref[c] = jnp.where(tie & (before >= need), MASKED_SCORE, s)
            return seen + jnp.sum(ind, axis=1, keepdims=True)

        lax.fori_loop(0, n_chunks, drop, jnp.zeros((qb, 1), F32))

    m_ref[...] = jnp.full(m_ref.shape, NEG_BIG, F32)
    acc_ref[...] = jnp.zeros(acc_ref.shape, F32)

    def key_offset(c):
        return c * kc if isinstance(c, int) else pl.multiple_of(c * kc, kc)

    def logits(c):
        return _dot_t(qs_ref[...], kb_ref[pl.ds(key_offset(c), kc), :])

    def softmax_pv(c, s_ref):
        off = key_offset(c)
        bias_ref[...] = jnp.where(sc_ref[c] >= thr, 0.0, NEG_BIG)
        for r0 in range(0, N_HEADS * qb, ROW_TILE):
            r = slice(r0, r0 + ROW_TILE)
            s = s_ref[r, :] + bias_ref[r0 % qb:r0 % qb + ROW_TILE, :]
            m_old = m_ref[r, :]
            m_new = jnp.maximum(m_old, jnp.max(s, axis=1, keepdims=True))
            p_ref[r, :] = jnp.exp2(s - jnp.concatenate([m_new] * (kc // LANES), axis=1)).astype(BF16)
            a_ref[r, :] = jnp.exp2(m_old - m_new)
            m_ref[r, :] = m_new
        alpha = a_ref[...]
        acc_ref[...] = (jnp.concatenate([alpha, alpha], axis=1) * acc_ref[...]
                        + _dot(p_ref[...], vb_ref[pl.ds(off, kc), :]))

    s0_ref[...] = logits(0)

    def chunk_pair(j, carry):
        c0 = 2 * j
        s1_ref[...] = logits(c0 + 1)
        softmax_pv(c0, s0_ref)
        s0_ref[...] = logits(c0 + 2)
        softmax_pv(c0 + 1, s1_ref)
        return carry

    full_pairs = (n_chunks - 1) // 2
    lax.fori_loop(0, full_pairs, chunk_pair, 0)
    if n_chunks % 2 == 0:
        s1_ref[...] = logits(n_chunks - 1)
        softmax_pv(n_chunks - 2, s0_ref)
        softmax_pv(n_chunks - 1, s1_ref)
    else:
        softmax_pv(n_chunks - 1, s0_ref)

    out = acc_ref[:, 0:LANES] / acc_ref[:, LANES:2 * LANES]
    for c in range(N_HEADS // 2):
        oc = jnp.where(lo, out[c * qb:(c + 1) * qb], out[(c + 4) * qb:(c + 5) * qb])
        o_ref[:, c * LANES:(c + 1) * LANES] = oc.astype(BF16)


def _attn_prompt(q, iq, ikw, ik2, kb, vb, batch, seq, qb, kc):
    nqb = seq // qb
    per_call = kc // qb
    topk = min(TOPK_MAX, seq // 4)
    as_seq = lambda a: a.reshape(batch, seq, a.shape[-1])
    ik2, kb, vb = as_seq(ik2), as_seq(kb), as_seq(vb)
    tri = jnp.triu(jnp.ones((kc, kc), BF16), k=1)
    outs = []
    for n in range(1, seq // kc + 1):
        outs.append(_attn_prompt_call(q, iq, ikw, ik2, kb, vb, tri, batch, qb, kc, topk, n, nqb, per_call))
    out = jnp.concatenate([o.reshape(batch, per_call * qb, ATTN_DIM) for o in outs], axis=1)
    return out.reshape(batch * seq, ATTN_DIM)


def _attn_prompt_call(q, iq, ikw, ik2, kb, vb, tri, batch, qb, kc, topk, n, nqb, per_call):
    i0 = (n - 1) * per_call
    tokq = lambda width: pl.BlockSpec((qb, width), lambda b, i: (b * nqb + i0 + i, 0))
    seqk = lambda width: pl.BlockSpec((None, n * kc, width), lambda b, i: (b, 0, 0))
    return pl.pallas_call(
        functools.partial(_attn_prompt_body, qb=qb, kc=kc, topk=topk, n_chunks=n),
        grid=(batch, per_call),
        in_specs=[tokq(ATTN_DIM), tokq(N_IDX_HEADS * IDX_DIM), tokq(LANES),
                  seqk(LANES), seqk(KV_DIM), seqk(2 * KV_DIM),
                  pl.BlockSpec((kc, kc), lambda b, i: (0, 0))],
        out_specs=pl.BlockSpec((qb, ATTN_DIM), lambda b, i: (b * per_call + i, 0)),
        out_shape=jax.ShapeDtypeStruct((batch * per_call * qb, ATTN_DIM), BF16),
        scratch_shapes=[
            pltpu.VMEM((n, qb, kc), F32),
            pltpu.VMEM((N_HEADS * qb, LANES), BF16),
            pltpu.VMEM((N_IDX_HEADS * qb, LANES), BF16),
            pltpu.VMEM((N_HEADS * qb, kc), F32),
            pltpu.VMEM((N_HEADS * qb, kc), F32),
            pltpu.VMEM((N_HEADS * qb, kc), BF16),
            pltpu.VMEM((qb, kc), F32),
            pltpu.VMEM((N_HEADS * qb, LANES), F32),
            pltpu.VMEM((N_HEADS * qb, LANES), F32),
            pltpu.VMEM((N_HEADS * qb, 2 * KV_DIM), F32),
        ],
        compiler_params=_cparams(("arbitrary", "arbitrary")),
        name=f"attn_prompt_{n}",
    )(q, iq, ikw, ik2, kb, vb, tri)


def _stack_heads(x_bf, n_heads):
    x = x_bf.astype(F32)
    return jnp.concatenate([x[:, h * HEAD_DIM:(h + 1) * HEAD_DIM] for h in range(n_heads)], axis=0).astype(BF16)


def _weighted_relu_sum(logits, wcol, t):
    sc = wcol[0] * jnp.maximum(logits[0:t], 0.0)
    for h in range(1, N_IDX_HEADS):
        sc = sc + wcol[h] * jnp.maximum(logits[h * t:(h + 1) * t], 0.0)
    return sc


def _head_weights(w_ref):
    wq = w_ref[...]
    return [wq[:, IDX_DIM + h:IDX_DIM + h + 1] * (N_IDX_HEADS ** -0.5) for h in range(N_IDX_HEADS)]


def _sample_scores_body(pt_ref, iq_ref, w_ref, *refs, pg, t):
    del pt_ref
    pages = refs[:pg]
    keys_ref, nkeys_ref, ikt_ref = refs[pg:]
    iqs = _stack_heads(iq_ref[...], N_IDX_HEADS)
    wcol = _head_weights(w_ref)
    for j in range(pg):
        ikt_ref[:, j * PAGE_SIZE:(j + 1) * PAGE_SIZE] = pages[j][...].astype(BF16)
    logits = _dot(iqs, ikt_ref[...])
    keys_ref[...] = _weighted_relu_sum(logits, wcol, t)

    @pl.when(pl.program_id(1) == 0)
    def _():
        ikn = jnp.concatenate([w_ref[...][:, 0:IDX_DIM], jnp.zeros((LANES - t, IDX_DIM), F32)], axis=0)
        sc = _weighted_relu_sum(_dot_t(iqs, ikn.astype(BF16)), wcol, t)
        col_n = lax.broadcasted_iota(I32, (t, LANES), 1)
        row_n = lax.broadcasted_iota(I32, (t, LANES), 0)
        nkeys_ref[...] = jnp.where(col_n <= row_n, sc, MASKED_SCORE)


def _sample_scores(page_table, iq, ikw, cache_idx_t, db, t, pg):
    n_pages = page_table.shape[1]
    page_spec = lambda j: pl.BlockSpec((None, IDX_DIM, PAGE_SIZE),
                                       lambda b, g, pt: (pt[b, g * pg + j], 0, 0))
    grid_spec = pltpu.PrefetchScalarGridSpec(
        num_scalar_prefetch=1,
        grid=(db, n_pages // pg),
        in_specs=[pl.BlockSpec((t, N_IDX_HEADS * IDX_DIM), lambda b, g, pt: (b, 0)),
                  pl.BlockSpec((t, LANES), lambda b, g, pt: (b, 0))] + [page_spec(j) for j in range(pg)],
        out_specs=[pl.BlockSpec((t, pg * PAGE_SIZE), lambda b, g, pt: (b, g)),
                   pl.BlockSpec((t, LANES), lambda b, g, pt: (b, 0))],
        scratch_shapes=[pltpu.VMEM((IDX_DIM, pg * PAGE_SIZE), BF16)],
    )
    return pl.pallas_call(
        functools.partial(_sample_scores_body, pg=pg, t=t),
        grid_spec=grid_spec,
        out_shape=[jax.ShapeDtypeStruct((db * t, n_pages * PAGE_SIZE), F32),
                   jax.ShapeDtypeStruct((db * t, LANES), F32)],
        compiler_params=_cparams(("arbitrary", "arbitrary")),
        name="sample_scores",
    )(page_table, iq, ikw, *([cache_idx_t] * pg))


def _sample_thr_body(keys_ref, nkeys_ref, thr_ref, cut_ref, *, rows, topk, idx_bits, cw):
    past = keys_ref.shape[-1]
    col_c = lax.broadcasted_iota(I32, (rows, cw), 1)
    col_n = lax.broadcasted_iota(I32, (rows, LANES), 1)

    def count_keys(pred):
        acc = jnp.where(pred(nkeys_ref[...]), 1.0, 0.0)
        for c in range(past // cw):
            acc = acc + _lane_fold(jnp.where(pred(keys_ref[:, c * cw:(c + 1) * cw]), 1.0, 0.0))
        return jnp.sum(acc, axis=1, keepdims=True)

    def count_keys_cols(pred):
        acc = jnp.where(pred(nkeys_ref[...], past + col_n), 1.0, 0.0)
        for c in range(past // cw):
            acc = acc + _lane_fold(jnp.where(pred(keys_ref[:, c * cw:(c + 1) * cw], c * cw + col_c), 1.0, 0.0))
        return jnp.sum(acc, axis=1, keepdims=True)

    thr = _select_threshold(lambda cand: count_keys(lambda kk: kk >= cand), rows, float(topk))
    thr = jnp.maximum(thr, MIN_THRESHOLD)
    over = count_keys(lambda kk: kk >= thr) > float(topk)
    thr_ref[...] = jnp.broadcast_to(thr, (rows, LANES))
    cut_ref[...] = jnp.full((rows, LANES), 1 << idx_bits, I32)

    @pl.when(jnp.max(jnp.where(over, 1.0, 0.0)) > 0.0)
    def _():
        need = float(topk) - count_keys(lambda kk: kk > thr)
        cut = _tie_cutoff(count_keys_cols, thr, need, idx_bits)
        cut_ref[...] = jnp.broadcast_to(jnp.where(over, cut, jnp.int32(1 << idx_bits)), (rows, LANES))


def _sample_thr(keys, nkeys, t, rows):
    n, past = keys.shape
    topk = min(TOPK_MAX, (past + t) // 4)
    blk = lambda width: pl.BlockSpec((rows, width), lambda i: (i, 0))
    return pl.pallas_call(
        functools.partial(_sample_thr_body, rows=rows, topk=topk,
                          idx_bits=(past + LANES - 1).bit_length(), cw=2048),
        grid=(n // rows,),
        in_specs=[blk(past), blk(LANES)],
        out_specs=[blk(LANES), blk(LANES)],
        out_shape=[jax.ShapeDtypeStruct((n, LANES), F32), jax.ShapeDtypeStruct((n, LANES), I32)],
        compiler_params=_cparams(("arbitrary",)),
        name="sample_thr",
    )(keys, nkeys)


def _sample_attn_body(pt_ref, keys_ref, nkeys_ref, thr_ref, cut_ref, q_ref, kn_ref, vn_ref, *refs,
                      pg, t, past):
    del pt_ref
    kpages, vpages = refs[:pg], refs[pg:2 * pg]
    o_ref = refs[2 * pg]
    qs_ref, kt_ref, vt_ref, m_ref, l_ref, acc_ref = refs[2 * pg + 1:]
    g = pl.program_id(1)
    rows = N_HEADS * t

    def flash_update(s, sel, pv):
        n = s.shape[-1]
        s = jnp.where(sel[None], s.reshape(N_HEADS, t, n), NEG_BIG).reshape(rows, n)
        m_old = m_ref[...]
        m_new = jnp.maximum(m_old, jnp.max(s, axis=1, keepdims=True))
        alpha = jnp.exp2(m_old - m_new)
        p = jnp.exp2(s - m_new)
        l_ref[...] = alpha * l_ref[...] + jnp.sum(p, axis=1, keepdims=True)
        acc_ref[...] = alpha * acc_ref[...] + pv(p.astype(BF16))
        m_ref[...] = m_new

    @pl.when(g == 0)
    def _():
        lo = lax.broadcasted_iota(I32, (t, LANES), 1) < HEAD_DIM
        for c in range(N_HEADS // 2):
            qc = q_ref[:, c * LANES:(c + 1) * LANES]
            qs_ref[c * t:(c + 1) * t, :] = jnp.where(lo, qc, jnp.zeros_like(qc))
            qs_ref[(c + 4) * t:(c + 5) * t, :] = jnp.where(lo, jnp.zeros_like(qc), qc)
        m_ref[...] = jnp.full(m_ref.shape, NEG_BIG, F32)
        l_ref[...] = jnp.zeros(l_ref.shape, F32)
        acc_ref[...] = jnp.zeros(acc_ref.shape, F32)

    thr = thr_ref[:, 0:1]
    cut = cut_ref[:, 0:1]

    def selected(kk, col):
        return (kk > thr) | ((kk == thr) & (col < cut))

    qs = qs_ref[...]
    width = pg * PAGE_SIZE
    for j in range(pg):
        kt_ref[:, j * PAGE_SIZE:(j + 1) * PAGE_SIZE] = kpages[j][...].astype(BF16)
        vt_ref[:, j * PAGE_SIZE:(j + 1) * PAGE_SIZE] = vpages[j][...].astype(BF16)
    col_w = g * width + lax.broadcasted_iota(I32, (t, width), 1)
    flash_update(_dot(qs, kt_ref[...]), selected(keys_ref[...], col_w),
                 lambda p: _dot_t(p, vt_ref[...]))

    @pl.when(g == pl.num_programs(1) - 1)
    def _():
        pad = jnp.zeros((LANES - t, LANES), F32)
        kn = jnp.concatenate([kn_ref[...], pad], axis=0).astype(BF16)
        vn = jnp.concatenate([vn_ref[...], pad], axis=0).astype(BF16)
        col_n = past + lax.broadcasted_iota(I32, (t, LANES), 1)
        flash_update(_dot_t(qs, kn), selected(nkeys_ref[...], col_n), lambda p: _dot(p, vn))
        lo = lax.broadcasted_iota(I32, (t, LANES), 1) < HEAD_DIM
        out = acc_ref[...] / l_ref[...]
        for c in range(N_HEADS // 2):
            oc = jnp.where(lo, out[c * t:(c + 1) * t], out[(c + 4) * t:(c + 5) * t])
            o_ref[:, c * LANES:(c + 1) * LANES] = oc.astype(BF16)


def _sample_attn(page_table, keys, nkeys, thr, cut, q, k_new, v_new, cache_kt, cache_vt, db, t, pg):
    n_pages = page_table.shape[1]
    past = n_pages * PAGE_SIZE
    page_spec = lambda j: pl.BlockSpec((None, KV_DIM, PAGE_SIZE),
                                       lambda b, g, pt: (pt[b, g * pg + j], 0, 0))
    tok = lambda width: pl.BlockSpec((t, width), lambda b, g, pt: (b, 0))
    grid_spec = pltpu.PrefetchScalarGridSpec(
        num_scalar_prefetch=1,
        grid=(db, n_pages // pg),
        in_specs=[pl.BlockSpec((t, pg * PAGE_SIZE), lambda b, g, pt: (b, g)),
                  tok(LANES), tok(LANES), tok(LANES), tok(ATTN_DIM), tok(KV_DIM), tok(KV_DIM)]
                 + [page_spec(j) for j in range(pg)] + [page_spec(j) for j in range(pg)],
        out_specs=tok(ATTN_DIM),
        scratch_shapes=[
            pltpu.VMEM((N_HEADS * t, LANES), BF16),
            pltpu.VMEM((KV_DIM, pg * PAGE_SIZE), BF16),
            pltpu.VMEM((KV_DIM, pg * PAGE_SIZE), BF16),
            pltpu.VMEM((N_HEADS * t, 1), F32),
            pltpu.VMEM((N_HEADS * t, 1), F32),
            pltpu.VMEM((N_HEADS * t, LANES), F32),
        ],
    )
    return pl.pallas_call(
        functools.partial(_sample_attn_body, pg=pg, t=t, past=past),
        grid_spec=grid_spec,
        out_shape=jax.ShapeDtypeStruct((db * t, ATTN_DIM), BF16),
        compiler_params=_cparams(("arbitrary", "arbitrary")),
        name="sample_attn",
    )(page_table, keys, nkeys, thr, cut, q, k_new, v_new, *([cache_kt] * pg), *([cache_vt] * pg))


def _out_body(x_ref, a_ref, c_ref, wa_ref, wc_ref, o_ref):
    o_ref[...] = x_ref[...] + _dot(a_ref[...], wa_ref[...]) + _dot(c_ref[...], wc_ref[...])


def _out_proj(x, attn, conv, wa, wc, tm):
    n = x.shape[0]
    tok = lambda width: pl.BlockSpec((tm, width), lambda i: (i, 0))
    full = lambda a: pl.BlockSpec(a.shape, lambda i: (0, 0))
    return pl.pallas_call(
        _out_body,
        grid=(n // tm,),
        in_specs=[tok(D_MODEL), tok(ATTN_DIM), tok(CONV_DIM), full(wa), full(wc)],
        out_specs=tok(D_MODEL),
        out_shape=jax.ShapeDtypeStruct((n, D_MODEL), F32),
        compiler_params=_cparams(("arbitrary",)),
        name="out_proj",
    )(x, attn, conv, wa, wc)


def _ple_body(x_ref, pe_ref, g_ref, wg_ref, wp_ref, gf_ref, o_ref):
    x = x_ref[...]
    gate = jax.nn.sigmoid(_dot(_rms(x, g_ref[...]).astype(BF16), wg_ref[...]))
    x = x + gate * _dot(pe_ref[...].astype(BF16), wp_ref[...])
    o_ref[...] = _rms(x, gf_ref[...])


def _ple_final(x, pe, g, wg, wp, gf, tm):
    n = x.shape[0]
    tok = lambda width: pl.BlockSpec((tm, width), lambda i: (i, 0))
    full = lambda a: pl.BlockSpec(a.shape, lambda i: (0, 0))
    return pl.pallas_call(
        _ple_body,
        grid=(n // tm,),
        in_specs=[tok(D_MODEL), tok(PLE_DIM), full(g), full(wg), full(wp), full(gf)],
        out_specs=tok(D_MODEL),
        out_shape=jax.ShapeDtypeStruct((n, D_MODEL), F32),
        compiler_params=_cparams(("arbitrary",)),
        name="ple_final",
    )(x, pe, g, wg, wp, gf)


def _rope_tables(pos):
    half = HEAD_DIM // 2
    inv = ROPE_THETA ** (-jnp.arange(half, dtype=F32) / half)
    ang = pos.astype(F32)[:, None] * inv[None, :]
    cos, sin = jnp.cos(ang), jnp.sin(ang)
    cos2 = jnp.concatenate([cos, cos], axis=1)
    sin2 = jnp.concatenate([-sin, sin], axis=1)
    cos128 = jnp.concatenate([cos2, cos2], axis=1)
    sin128 = jnp.concatenate([sin2, sin2], axis=1)
    cosk = jnp.concatenate([cos2, jnp.ones_like(cos2)], axis=1)
    sink = jnp.concatenate([sin2, jnp.zeros_like(sin2)], axis=1)
    return cos128, sin128, cosk, sink


def _prep_w_in(w_in):
    d = w_in.shape[0]
    q = w_in[:, :ATTN_DIM].reshape(d, N_HEADS, HEAD_DIM)[:, jnp.array(_HEAD_PERM)].reshape(d, ATTN_DIM)
    o = ATTN_DIM
    kv = w_in[:, o:o + 2 * KV_DIM]
    o += 2 * KV_DIM
    iq = w_in[:, o:o + N_IDX_HEADS * IDX_DIM]
    o += N_IDX_HEADS * IDX_DIM
    ikw = w_in[:, o:o + IDX_DIM + N_IDX_HEADS]
    o += IDX_DIM + N_IDX_HEADS
    ikw = jnp.pad(ikw, ((0, 0), (0, LANES - IDX_DIM - N_IDX_HEADS)))
    conv = w_in[:, o:]
    return jnp.concatenate([q, kv, iq, ikw, conv], axis=1).astype(BF16)


def _layer(x, pe, mixer, tm, g_ffn1, w1g, w1u, w1d, g_mix, w_in, conv_w, wo_a, wo_c,
           g_ffn2, w2g, w2u, w2d, g_ple, w_pg, w_pp, g_final):
    x1 = _ffn(x, g_ffn1, w1g, w1u, w1d, tm)
    attn, conv, state = mixer(x1, g_mix, w_in, conv_w)
    x2 = _out_proj(x1, attn, conv, wo_a, wo_c, tm)
    x3 = _ffn(x2, g_ffn2, w2g, w2u, w2d, tm)
    return _ple_final(x3, pe, g_ple, w_pg, w_pp, g_final, tm), state


def kernel(x_prompt, x_sample, cache_k, cache_v, cache_idx_k, state_conv, page_table, p_prompt, p_sample,
           g_ffn1, w1_gate, w1_up, w1_down, g_mix, w_in, conv_w, w_out,
           g_ffn2, w2_gate, w2_up, w2_down, g_ple, w_ple_gate, w_ple_proj, g_final):
    depth = w_in.shape[0]
    assert depth == 1, "single-layer step"
    batch, seq, _ = x_prompt.shape
    db, t, _ = x_sample.shape
    n_pages = page_table.shape[1]
    past = n_pages * PAGE_SIZE
    assert t == SUBLANES and seq % 512 == 0

    row = lambda gvec: gvec.reshape(1, -1)
    perm = jnp.array(_HEAD_PERM)
    wo_a = w_out[0, :ATTN_DIM].reshape(N_HEADS, HEAD_DIM, D_MODEL)[perm].reshape(ATTN_DIM, D_MODEL).astype(BF16)
    wo_c = w_out[0, ATTN_DIM:].astype(BF16)
    shared = (row(g_ffn1[0]), w1_gate[0].astype(BF16), w1_up[0].astype(BF16), w1_down[0].astype(BF16),
              row(g_mix[0]), _prep_w_in(w_in[0]), conv_w[0], wo_a, wo_c,
              row(g_ffn2[0]), w2_gate[0].astype(BF16), w2_up[0].astype(BF16), w2_down[0].astype(BF16),
              row(g_ple[0]), w_ple_gate[0].astype(BF16), w_ple_proj[0].astype(BF16), row(g_final))

    tm_p = 512
    tabs_p = _rope_tables(jnp.arange(seq, dtype=I32))

    def prompt_mixer(x1, g, w, cw):
        q, iq, k, v, ikw, kb, vb, ik2, cy, tail = _proj(x1, g, w, *tabs_p, cw, tm_p, seq // tm_p)
        attn = _attn_prompt(q, iq, ikw, ik2, kb, vb, batch, seq, qb=256, kc=512)
        return attn, cy, (k, v, ikw, tail)

    yp, (kp, vp, ikwp, tailp) = _layer(x_prompt.reshape(batch * seq, D_MODEL),
                                       p_prompt[0].reshape(batch * seq, PLE_DIM), prompt_mixer, tm_p, *shared)

    n_s = db * t
    tabs_s = tuple(jnp.tile(a, (db, 1)) for a in _rope_tables(past + jnp.arange(t, dtype=I32)))
    buf = state_conv[0]
    zero = jnp.zeros((db, t - 2, CONV_DIM), F32)
    halo1 = jnp.concatenate([buf[:, 1:2], jnp.zeros((db, t - 1, CONV_DIM), F32)], axis=1).reshape(n_s, CONV_DIM)
    halo2 = jnp.concatenate([buf, zero], axis=1).reshape(n_s, CONV_DIM)
    idx_t = jnp.transpose(cache_idx_k[0], (0, 2, 1))
    k_t = jnp.transpose(cache_k[0], (0, 2, 3, 1)).reshape(-1, KV_DIM, PAGE_SIZE)
    v_t = jnp.transpose(cache_v[0], (0, 2, 3, 1)).reshape(-1, KV_DIM, PAGE_SIZE)

    def sample_mixer(x1, g, w, cw):
        q, iq, k, v, ikw, kb, vb, ik2, cy, u = _proj(x1, g, w, *tabs_s, cw, n_s, 1, halos=(halo1, halo2))
        keys, nkeys = _sample_scores(page_table, iq, ikw, idx_t, db, t, pg=min(32, n_pages))
        thr, cut = _sample_thr(keys, nkeys, t, rows=min(128, n_s))
        attn = _sample_attn(page_table, keys, nkeys, thr, cut, q, k, v, k_t, v_t, db, t, pg=min(32, n_pages))
        return attn, cy, (k, v, ikw, u)

    ys, (ks, vs, ikws, us) = _layer(x_sample.reshape(n_s, D_MODEL), p_sample[0].reshape(n_s, PLE_DIM),
                                    sample_mixer, n_s, *shared)

    return (yp.reshape(batch, seq, D_MODEL),
            ys.reshape(db, t, D_MODEL),
            kp.reshape(batch, N_KV_HEADS, HEAD_DIM, seq).transpose(0, 3, 1, 2)[None],
            vp.reshape(batch, N_KV_HEADS, HEAD_DIM, seq).transpose(0, 3, 1, 2)[None],
            ikwp[:, :IDX_DIM].reshape(1, batch, seq, IDX_DIM),
            tailp[:, SUBLANES - (CONV_WIDTH - 1):][None],
            ks.reshape(1, db, t, N_KV_HEADS, HEAD_DIM),
            vs.reshape(1, db, t, N_KV_HEADS, HEAD_DIM),
            ikws[:, :IDX_DIM].reshape(1, db, t, IDX_DIM),
            us.reshape(db, t, CONV_DIM)[:, t - (CONV_WIDTH - 1):][None])
```

```python
import functools

import jax
import jax.numpy as jnp
from jax import lax
from jax.experimental import pallas as pl
from jax.experimental.pallas import tpu as pltpu

F32 = jnp.float32
BF16 = jnp.bfloat16
I32 = jnp.int32

D_MODEL = 1024
N_HEADS = 8
HEAD_DIM = 64
N_KV_HEADS = 2
ATTN_DIM = N_HEADS * HEAD_DIM
KV_DIM = N_KV_HEADS * HEAD_DIM
N_IDX_HEADS = 4
IDX_DIM = 64
TOPK_MAX = 256
CONV_DIM = D_MODEL - ATTN_DIM
CONV_WIDTH = 3
D_FF = 2816
PLE_DIM = 256
PAGE_SIZE = 128
ROPE_THETA = 10000.0
RMS_EPS = 1e-6

ROW_TILE = 64
LANES = 128
SUBLANES = 8
VMEM_LIMIT = 56 * 1024 * 1024

INT_MIN = -(2 ** 31)
MASKED_SCORE = -1e38
MIN_THRESHOLD = -5e37
NEG_BIG = -1e30
LOG2E = 1.4426950408889634

_Q0, _K0, _V0, _IQ0, _IK0, _CB0, _CC0, _CH0, _PROJ_W = 0, 512, 640, 768, 1024, 1152, 1664, 2176, 2688
_HEAD_PERM = (0, 4, 1, 5, 2, 6, 3, 7)


def _cparams(sem):
    return pltpu.CompilerParams(dimension_semantics=sem, vmem_limit_bytes=VMEM_LIMIT)


def _dot(a, b):
    return jnp.dot(a, b, preferred_element_type=F32)


def _dot_t(a, b):
    return lax.dot_general(a, b, (((1,), (1,)), ((), ())), preferred_element_type=F32)


def _rms(x, g):
    ms = jnp.mean(x * x, axis=-1, keepdims=True)
    return x * lax.rsqrt(ms + RMS_EPS) * g


def _ffn_body(x_ref, g_ref, wg_ref, wu_ref, wd_ref, o_ref, h_ref, acc_ref):
    j = pl.program_id(1)

    @pl.when(j == 0)
    def _():
        h_ref[...] = _rms(x_ref[...], g_ref[...]).astype(BF16)

    h = h_ref[...]
    gate = _dot(h, wg_ref[...])
    up = _dot(h, wu_ref[...])
    act = (gate * jax.nn.sigmoid(gate) * up).astype(BF16)
    part = _dot(act, wd_ref[...])

    @pl.when(j == 0)
    def _():
        acc_ref[...] = part

    @pl.when(j > 0)
    def _():
        acc_ref[...] += part

    @pl.when(j == pl.num_programs(1) - 1)
    def _():
        o_ref[...] = x_ref[...] + 0.5 * acc_ref[...]


def _ffn(x, g, wg, wu, wd, tm):
    n = x.shape[0]
    ff_chunk = D_FF // 2
    return pl.pallas_call(
        _ffn_body,
        grid=(n // tm, D_FF // ff_chunk),
        in_specs=[
            pl.BlockSpec((tm, D_MODEL), lambda i, j: (i, 0)),
            pl.BlockSpec((1, D_MODEL), lambda i, j: (0, 0)),
            pl.BlockSpec((D_MODEL, ff_chunk), lambda i, j: (0, j)),
            pl.BlockSpec((D_MODEL, ff_chunk), lambda i, j: (0, j)),
            pl.BlockSpec((ff_chunk, D_MODEL), lambda i, j: (j, 0)),
        ],
        out_specs=pl.BlockSpec((tm, D_MODEL), lambda i, j: (i, 0)),
        out_shape=jax.ShapeDtypeStruct((n, D_MODEL), F32),
        scratch_shapes=[pltpu.VMEM((tm, D_MODEL), BF16), pltpu.VMEM((tm, D_MODEL), F32)],
        compiler_params=_cparams(("arbitrary", "arbitrary")),
        name="ffn",
    )(x, g, wg, wu, wd)


def _mix_ffn_body(x_ref, a_ref, c_ref, wa_ref, wc_ref, g_ref, wg_ref, wu_ref, wd_ref, o_ref,
                  h_ref, acc_ref, xs_ref):
    j = pl.program_id(1)

    @pl.when(j == 0)
    def _():
        xs = x_ref[...] + _dot(a_ref[...], wa_ref[...]) + _dot(c_ref[...], wc_ref[...])
        xs_ref[...] = xs
        h_ref[...] = _rms(xs, g_ref[...]).astype(BF16)

    h = h_ref[...]
    gate = _dot(h, wg_ref[...])
    up = _dot(h, wu_ref[...])
    act = (gate * jax.nn.sigmoid(gate) * up).astype(BF16)
    part = _dot(act, wd_ref[...])

    @pl.when(j == 0)
    def _():
        acc_ref[...] = part

    @pl.when(j > 0)
    def _():
        acc_ref[...] += part

    @pl.when(j == pl.num_programs(1) - 1)
    def _():
        o_ref[...] = xs_ref[...] + 0.5 * acc_ref[...]


def _mix_ffn(x, attn, conv, wa, wc, g, wg, wu, wd, tm):
    n = x.shape[0]
    ff_chunk = D_FF // 2
    tok = lambda width: pl.BlockSpec((tm, width), lambda i, j: (i, 0))
    full = lambda a: pl.BlockSpec(a.shape, lambda i, j: (0, 0))
    return pl.pallas_call(
        _mix_ffn_body,
        grid=(n // tm, D_FF // ff_chunk),
        in_specs=[tok(D_MODEL), tok(ATTN_DIM), tok(CONV_DIM), full(wa), full(wc), full(g),
                  pl.BlockSpec((D_MODEL, ff_chunk), lambda i, j: (0, j)),
                  pl.BlockSpec((D_MODEL, ff_chunk), lambda i, j: (0, j)),
                  pl.BlockSpec((ff_chunk, D_MODEL), lambda i, j: (j, 0))],
        out_specs=tok(D_MODEL),
        out_shape=jax.ShapeDtypeStruct((n, D_MODEL), F32),
        scratch_shapes=[pltpu.VMEM((tm, D_MODEL), BF16), pltpu.VMEM((tm, D_MODEL), F32),
                        pltpu.VMEM((tm, D_MODEL), F32)],
        compiler_params=_cparams(("arbitrary", "arbitrary")),
        name="mix_ffn",
    )(x, attn, conv, wa, wc, g, wg, wu, wd)


def _swap32(x):
    w = x.shape[-1]
    lane = lax.broadcasted_iota(I32, x.shape, 1)
    return jnp.where((lane & 63) < 32, pltpu.roll(x, w - 32, 1), pltpu.roll(x, 32, 1))


def _rope_cols(z, cos, sin):
    cols = []
    for c in range(z.shape[-1] // LANES):
        zc = z[:, c * LANES:(c + 1) * LANES]
        cols.append(zc * cos + _swap32(zc) * sin)
    return cols[0] if len(cols) == 1 else jnp.concatenate(cols, axis=1)


def _proj_body(*refs, tm, tiles_per_seq, halo):
    if halo:
        (x_ref, g_ref, w_ref, cos_ref, sin_ref, cosk_ref, sink_ref, cw_ref, h1_ref, h2_ref,
         q_ref, iq_ref, k_ref, v_ref, ikw_ref, kb_ref, vb_ref, ik2_ref, cy_ref, u_ref) = refs
    else:
        (x_ref, g_ref, w_ref, cos_ref, sin_ref, cosk_ref, sink_ref, cw_ref,
         q_ref, iq_ref, k_ref, v_ref, ikw_ref, kb_ref, vb_ref, ik2_ref, cy_ref, u_ref, carry_ref) = refs

    h = _rms(x_ref[...], g_ref[...]).astype(BF16)
    cos, sin = cos_ref[...], sin_ref[...]

    zq = _dot(h, w_ref[:, _Q0:_K0])
    q_ref[...] = (_rope_cols(zq, cos, sin) * (HEAD_DIM ** -0.5 * LOG2E)).astype(BF16)
    kr = _rope_cols(_dot(h, w_ref[:, _K0:_V0]), cos, sin)
    k_ref[...] = kr if halo else kr.T
    kb_ref[...] = kr.astype(BF16)
    zv = _dot(h, w_ref[:, _V0:_IQ0])
    v_ref[...] = zv if halo else zv.T
    vb_ref[...] = jnp.concatenate([zv, jnp.ones_like(zv)], axis=1).astype(BF16)
    ziq = _dot(h, w_ref[:, _IQ0:_IK0])
    iq_ref[...] = (_rope_cols(ziq, cos, sin) * (IDX_DIM ** -0.5)).astype(BF16)
    ikr = _rope_cols(_dot(h, w_ref[:, _IK0:_CB0]), cosk_ref[...], sink_ref[...])
    ikw_ref[...] = ikr
    lane = lax.broadcasted_iota(I32, ikr.shape, 1)
    ik2_ref[...] = jnp.where(lane < IDX_DIM, ikr, pltpu.roll(ikr, IDX_DIM, 1)).astype(BF16)

    cb = _dot(h, w_ref[:, _CB0:_CC0])
    u = _dot(h, w_ref[:, _CC0:_CH0]) * _dot(h, w_ref[:, _CH0:_PROJ_W])
    row = lax.broadcasted_iota(I32, u.shape, 0)
    r1 = pltpu.roll(u, 1, 0)
    r2 = pltpu.roll(u, 2, 0)
    if halo:
        t = row & (SUBLANES - 1)
        us1 = jnp.where(t == 0, h1_ref[...], r1)
        us2 = jnp.where(t < 2, h2_ref[...], r2)
        u_ref[...] = u
    else:
        @pl.when(pl.program_id(0) % tiles_per_seq == 0)
        def _():
            carry_ref[...] = jnp.zeros_like(carry_ref)

        c0 = carry_ref[SUBLANES - 2:SUBLANES - 1, :]
        c1 = carry_ref[SUBLANES - 1:SUBLANES, :]
        us1 = jnp.where(row == 0, c1, r1)
        us2 = jnp.where(row == 0, c0, jnp.where(row == 1, c1, r2))
        tail = u[tm - SUBLANES:tm, :]
        carry_ref[...] = tail
        u_ref[0] = tail
    cw = cw_ref[...]
    y = cb * (cw[0:1, :] * us2 + cw[1:2, :] * us1 + cw[2:3, :] * u)
    cy_ref[...] = y.astype(BF16)


def _proj(x, g, w, cos, sin, cosk, sink, cw, tm, tiles_per_seq, halos=None):
    n = x.shape[0]
    n_tiles = n // tm
    halo = halos is not None
    tok = lambda width: pl.BlockSpec((tm, width), lambda i: (i, 0))
    tab = pl.BlockSpec((tm, LANES), lambda i: (i % tiles_per_seq, 0))
    full = lambda a: pl.BlockSpec(a.shape, lambda i: (0,) * a.ndim)
    in_specs = [tok(D_MODEL), full(g), full(w), tab, tab, tab, tab, full(cw)]
    args = [x, g, w, cos, sin, cosk, sink, cw]
    out_shape = [
        jax.ShapeDtypeStruct((n, ATTN_DIM), BF16),
        jax.ShapeDtypeStruct((n, N_IDX_HEADS * IDX_DIM), BF16),
        jax.ShapeDtypeStruct((n, KV_DIM), F32),
        jax.ShapeDtypeStruct((n, KV_DIM), F32),
        jax.ShapeDtypeStruct((n, LANES), F32),
        jax.ShapeDtypeStruct((n, KV_DIM), BF16),
        jax.ShapeDtypeStruct((n, 2 * KV_DIM), BF16),
        jax.ShapeDtypeStruct((n, LANES), BF16),
        jax.ShapeDtypeStruct((n, CONV_DIM), BF16),
    ]
    out_specs = [tok(ATTN_DIM), tok(N_IDX_HEADS * IDX_DIM), tok(KV_DIM), tok(KV_DIM), tok(LANES),
                 tok(KV_DIM), tok(2 * KV_DIM), tok(LANES), tok(CONV_DIM)]
    scratch = []
    if halo:
        in_specs += [tok(CONV_DIM), tok(CONV_DIM)]
        args += list(halos)
        out_shape.append(jax.ShapeDtypeStruct((n, CONV_DIM), F32))
        out_specs.append(tok(CONV_DIM))
    else:
        n_seq = n_tiles // tiles_per_seq
        for slot in (2, 3):
            out_shape[slot] = jax.ShapeDtypeStruct((n_seq, KV_DIM, tiles_per_seq * tm), F32)
            out_specs[slot] = pl.BlockSpec((None, KV_DIM, tm), lambda i: (i // tiles_per_seq, 0, i % tiles_per_seq))
        out_shape.append(jax.ShapeDtypeStruct((n_seq, SUBLANES, CONV_DIM), F32))
        out_specs.append(pl.BlockSpec((1, SUBLANES, CONV_DIM), lambda i: (i // tiles_per_seq, 0, 0)))
        scratch.append(pltpu.VMEM((SUBLANES, CONV_DIM), F32))
    return pl.pallas_call(
        functools.partial(_proj_body, tm=tm, tiles_per_seq=tiles_per_seq, halo=halo),
        grid=(n_tiles,),
        in_specs=in_specs,
        out_specs=out_specs,
        out_shape=out_shape,
        scratch_shapes=scratch,
        compiler_params=_cparams(("arbitrary",)),
        name="proj_sample" if halo else "proj_prompt",
    )(*args)


def _lane_fold(ind):
    acc = ind[:, 0:LANES]
    for j in range(1, ind.shape[-1] // LANES):
        acc = acc + ind[:, j * LANES:(j + 1) * LANES]
    return acc


def _key_to_f32(key):
    return lax.bitcast_convert_type(key ^ ((key >> 31) & 0x7FFFFFFF), F32)


def _select_threshold(count_ge, rows, topk):
    def step(b, t):
        cand = t + lax.shift_left(jnp.int32(1), jnp.int32(31) - jnp.asarray(b, I32))
        return jnp.where(count_ge(_key_to_f32(cand)) >= topk, cand, t)

    return _key_to_f32(lax.fori_loop(0, 32, step, jnp.full((rows, 1), INT_MIN, I32)))


def _tie_cutoff(count_fn, t, need, idx_bits):
    def step(b, c):
        cand = c + lax.shift_left(jnp.int32(1), jnp.int32(idx_bits - 1) - jnp.asarray(b, I32))
        cnt = count_fn(lambda s, col: (s == t) & (col < cand))
        return jnp.where(cnt <= need, cand, c)

    return lax.fori_loop(0, idx_bits, step, jnp.zeros(t.shape, I32))


def _attn_prompt_body(q_ref, iq_ref, w_ref, ik2_ref, kb_ref, vb_ref, tri_ref, o_ref,
                      sc_ref, qs_ref, iqs_ref, s0_ref, s1_ref, p_ref, bias_ref,
                      m_ref, a_ref, acc_ref,
                      *, qb, kc, topk, n_chunks):
    i = (n_chunks - 1) * (kc // qb) + pl.program_id(1)
    lo = lax.broadcasted_iota(I32, (qb, LANES), 1) < HEAD_DIM

    for c in range(N_HEADS // 2):
        qc = q_ref[:, c * LANES:(c + 1) * LANES]
        qs_ref[c * qb:(c + 1) * qb, :] = jnp.where(lo, qc, jnp.zeros_like(qc))
        qs_ref[(c + 4) * qb:(c + 5) * qb, :] = jnp.where(lo, jnp.zeros_like(qc), qc)
    for c in range(N_IDX_HEADS // 2):
        ic = iq_ref[:, c * LANES:(c + 1) * LANES]
        iqs_ref[(2 * c) * qb:(2 * c + 1) * qb, :] = jnp.where(lo, ic, jnp.zeros_like(ic))
        iqs_ref[(2 * c + 1) * qb:(2 * c + 2) * qb, :] = jnp.where(lo, jnp.zeros_like(ic), ic)

    row_t = lax.broadcasted_iota(I32, (ROW_TILE, kc), 0)
    col_t = lax.broadcasted_iota(I32, (ROW_TILE, kc), 1)

    def score_chunk(c, carry):
        off = pl.multiple_of(c * kc, kc)
        s0_ref[0:N_IDX_HEADS * qb, :] = _dot_t(iqs_ref[...], ik2_ref[pl.ds(off, kc), :])
        for r0 in range(0, qb, ROW_TILE):
            r = slice(r0, r0 + ROW_TILE)
            wt = w_ref[r, :] * (N_IDX_HEADS ** -0.5)
            sc = wt[:, IDX_DIM:IDX_DIM + 1] * jnp.maximum(s0_ref[r, :], 0.0)
            for h in range(1, N_IDX_HEADS):
                sc = sc + (wt[:, IDX_DIM + h:IDX_DIM + h + 1]
                           * jnp.maximum(s0_ref[h * qb + r0:h * qb + r0 + ROW_TILE, :], 0.0))
            sc_ref[c, r, :] = jnp.where(off + col_t <= i * qb + r0 + row_t, sc, MASKED_SCORE)
        return carry

    lax.fori_loop(0, n_chunks, score_chunk, 0)

    def kth_largest_score(n):
        def count_ge(cand):
            cw = jnp.concatenate([jnp.broadcast_to(cand, (qb, LANES))] * (kc // LANES), axis=1)
            acc = jnp.zeros((qb, LANES), F32)
            for c in range(n):
                acc = acc + _lane_fold(jnp.where(sc_ref[c] >= cw, 1.0, 0.0))
            return jnp.sum(acc, axis=1, keepdims=True)

        return _select_threshold(count_ge, qb, float(topk))

    def count_scores(pred):
        def body(c, acc):
            return acc + _lane_fold(jnp.where(pred(sc_ref[c]), 1.0, 0.0))
        acc = lax.fori_loop(0, n_chunks, body, jnp.zeros((qb, LANES), F32))
        return jnp.sum(acc, axis=1, keepdims=True)

    thr = kth_largest_score(n_chunks)
    thr = jnp.maximum(thr, MIN_THRESHOLD)
    over = count_scores(lambda s: s >= thr) > float(topk)

    @pl.when(jnp.max(jnp.where(over, 1.0, 0.0)) > 0.0)
    def _():
        need = float(topk) - count_scores(lambda s: s > thr)

        def drop(c, seen):
            s = sc_ref[c]
            tie = s == thr
            ind = jnp.where(tie, 1.0, 0.0)
            before = seen + _dot(ind.astype(BF16), tri_ref[...])
            sc_ref[c] = jnp.where(tie & (before >= need), MASKED_SCORE, s)
            return seen + jnp.sum(ind, axis=1, keepdims=True)

        lax.fori_loop(0, n_chunks, drop, jnp.zeros((qb, 1), F32))

    m_ref[...] = jnp.full(m_ref.shape, NEG_BIG, F32)
    acc_ref[...] = jnp.zeros(acc_ref.shape, F32)

    def key_offset(c):
        return c * kc if isinstance(c, int) else pl.multiple_of(c * kc, kc)

    def logits(c):
        return _dot_t(qs_ref[...], kb_ref[pl.ds(key_offset(c), kc), :])

    def softmax_pv(c, s_ref):
        off = key_offset(c)
        bias_ref[...] = jnp.where(sc_ref[c] >= thr, 0.0, NEG_BIG)
        for r0 in range(0, N_HEADS * qb, ROW_TILE):
            r = slice(r0, r0 + ROW_TILE)
            s = s_ref[r, :] + bias_ref[r0 % qb:r0 % qb + ROW_TILE, :]
            m_old = m_ref[r, :]
            m_new = jnp.maximum(m_old, jnp.max(s, axis=1, keepdims=True))
            p_ref[r, :] = jnp.exp2(s - jnp.concatenate([m_new] * (kc // LANES), axis=1)).astype(BF16)
            a_ref[r, :] = jnp.exp2(m_old - m_new)
            m_ref[r, :] = m_new
        alpha = a_ref[...]
        acc_ref[...] = (jnp.concatenate([alpha, alpha], axis=1) * acc_ref[...]
                        + _dot(p_ref[...], vb_ref[pl.ds(off, kc), :]))

    s0_ref[...] = logits(0)

    def chunk_pair(j, carry):
        c0 = 2 * j
        s1_ref[...] = logits(c0 + 1)
        softmax_pv(c0, s0_ref)
        s0_ref[...] = logits(c0 + 2)
        softmax_pv(c0 + 1, s1_ref)
        return carry

    full_pairs = (n_chunks - 1) // 2
    lax.fori_loop(0, full_pairs, chunk_pair, 0)
    if n_chunks % 2 == 0:
        s1_ref[...] = logits(n_chunks - 1)
        softmax_pv(n_chunks - 2, s0_ref)
        softmax_pv(n_chunks - 1, s1_ref)
    else:
        softmax_pv(n_chunks - 1, s0_ref)

    out = acc_ref[:, 0:LANES] / acc_ref[:, LANES:2 * LANES]
    for c in range(N_HEADS // 2):
        oc = jnp.where(lo, out[c * qb:(c + 1) * qb], out[(c + 4) * qb:(c + 5) * qb])
        o_ref[:, c * LANES:(c + 1) * LANES] = oc.astype(BF16)


def _attn_prompt(q, iq, ikw, ik2, kb, vb, batch, seq, qb, kc):
    nqb = seq // qb
    per_call = kc // qb
    topk = min(TOPK_MAX, seq // 4)
    as_seq = lambda a: a.reshape(batch, seq, a.shape[-1])
    ik2, kb, vb = as_seq(ik2), as_seq(kb), as_seq(vb)
    tri = jnp.triu(jnp.ones((kc, kc), BF16), k=1)
    outs = []
    for n in range(1, seq // kc + 1):
        outs.append(_attn_prompt_call(q, iq, ikw, ik2, kb, vb, tri, batch, qb, kc, topk, n, nqb, per_call))
    out = jnp.concatenate([o.reshape(batch, per_call * qb, ATTN_DIM) for o in outs], axis=1)
    return out.reshape(batch * seq, ATTN_DIM)


def _attn_prompt_call(q, iq, ikw, ik2, kb, vb, tri, batch, qb, kc, topk, n, nqb, per_call):
    i0 = (n - 1) * per_call
    tokq = lambda width: pl.BlockSpec((qb, width), lambda b, i: (b * nqb + i0 + i, 0))
    seqk = lambda width: pl.BlockSpec((None, n * kc, width), lambda b, i: (b, 0, 0))
    return pl.pallas_call(
        functools.partial(_attn_prompt_body, qb=qb, kc=kc, topk=topk, n_chunks=n),
        grid=(batch, per_call),
        in_specs=[tokq(ATTN_DIM), tokq(N_IDX_HEADS * IDX_DIM), tokq(LANES),
                  seqk(LANES), seqk(KV_DIM), seqk(2 * KV_DIM),
                  pl.BlockSpec((kc, kc), lambda b, i: (0, 0))],
        out_specs=pl.BlockSpec((qb, ATTN_DIM), lambda b, i: (b * per_call + i, 0)),
        out_shape=jax.ShapeDtypeStruct((batch * per_call * qb, ATTN_DIM), BF16),
        scratch_shapes=[
            pltpu.VMEM((n, qb, kc), F32),
            pltpu.VMEM((N_HEADS * qb, LANES), BF16),
            pltpu.VMEM((N_IDX_HEADS * qb, LANES), BF16),
            pltpu.VMEM((N_HEADS * qb, kc), F32),
            pltpu.VMEM((N_HEADS * qb, kc), F32),
            pltpu.VMEM((N_HEADS * qb, kc), BF16),
            pltpu.VMEM((qb, kc), F32),
            pltpu.VMEM((N_HEADS * qb, LANES), F32),
            pltpu.VMEM((N_HEADS * qb, LANES), F32),
            pltpu.VMEM((N_HEADS * qb, 2 * KV_DIM), F32),
        ],
        compiler_params=_cparams(("arbitrary", "arbitrary")),
        name=f"attn_prompt_{n}",
    )(q, iq, ikw, ik2, kb, vb, tri)


def _stack_heads(x_bf, n_heads):
    x = x_bf.astype(F32)
    return jnp.concatenate([x[:, h * HEAD_DIM:(h + 1) * HEAD_DIM] for h in range(n_heads)], axis=0).astype(BF16)


def _weighted_relu_sum(logits, wcol, t):
    sc = wcol[0] * jnp.maximum(logits[0:t], 0.0)
    for h in range(1, N_IDX_HEADS):
        sc = sc + wcol[h] * jnp.maximum(logits[h * t:(h + 1) * t], 0.0)
    return sc


def _head_weights(w_ref):
    wq = w_ref[...]
    return [wq[:, IDX_DIM + h:IDX_DIM + h + 1] * (N_IDX_HEADS ** -0.5) for h in range(N_IDX_HEADS)]


def _sample_scores_body(pt_ref, iq_ref, w_ref, *refs, pg, t):
    del pt_ref
    pages = refs[:pg]
    keys_ref, nkeys_ref, ikt_ref = refs[pg:]
    iqs = _stack_heads(iq_ref[...], N_IDX_HEADS)
    wcol = _head_weights(w_ref)
    for j in range(pg):
        ikt_ref[:, j * PAGE_SIZE:(j + 1) * PAGE_SIZE] = pages[j][...].astype(BF16)
    logits = _dot(iqs, ikt_ref[...])
    keys_ref[...] = _weighted_relu_sum(logits, wcol, t)

    @pl.when(pl.program_id(1) == 0)
    def _():
        ikn = jnp.concatenate([w_ref[...][:, 0:IDX_DIM], jnp.zeros((LANES - t, IDX_DIM), F32)], axis=0)
        sc = _weighted_relu_sum(_dot_t(iqs, ikn.astype(BF16)), wcol, t)
        col_n = lax.broadcasted_iota(I32, (t, LANES), 1)
        row_n = lax.broadcasted_iota(I32, (t, LANES), 0)
        nkeys_ref[...] = jnp.where(col_n <= row_n, sc, MASKED_SCORE)


def _sample_scores(page_table, iq, ikw, cache_idx_t, db, t, pg):
    n_pages = page_table.shape[1]
    page_spec = lambda j: pl.BlockSpec((None, IDX_DIM, PAGE_SIZE),
                                       lambda b, g, pt: (pt[b, g * pg + j], 0, 0))
    grid_spec = pltpu.PrefetchScalarGridSpec(
        num_scalar_prefetch=1,
        grid=(db, n_pages // pg),
        in_specs=[pl.BlockSpec((t, N_IDX_HEADS * IDX_DIM), lambda b, g, pt: (b, 0)),
                  pl.BlockSpec((t, LANES), lambda b, g, pt: (b, 0))] + [page_spec(j) for j in range(pg)],
        out_specs=[pl.BlockSpec((t, pg * PAGE_SIZE), lambda b, g, pt: (b, g)),
                   pl.BlockSpec((t, LANES), lambda b, g, pt: (b, 0))],
        scratch_shapes=[pltpu.VMEM((IDX_DIM, pg * PAGE_SIZE), BF16)],
    )
    return pl.pallas_call(
        functools.partial(_sample_scores_body, pg=pg, t=t),
        grid_spec=grid_spec,
        out_shape=[jax.ShapeDtypeStruct((db * t, n_pages * PAGE_SIZE), F32),
                   jax.ShapeDtypeStruct((db * t, LANES), F32)],
        compiler_params=_cparams(("arbitrary", "arbitrary")),
        name="sample_scores",
    )(page_table, iq, ikw, *([cache_idx_t] * pg))


def _sample_thr_body(keys_ref, nkeys_ref, thr_ref, cut_ref, *, rows, topk, idx_bits, cw):
    past = keys_ref.shape[-1]
    col_c = lax.broadcasted_iota(I32, (rows, cw), 1)
    col_n = lax.broadcasted_iota(I32, (rows, LANES), 1)

    def count_keys(pred):
        acc = jnp.where(pred(nkeys_ref[...]), 1.0, 0.0)
        for c in range(past // cw):
            acc = acc + _lane_fold(jnp.where(pred(keys_ref[:, c * cw:(c + 1) * cw]), 1.0, 0.0))
        return jnp.sum(acc, axis=1, keepdims=True)

    def count_keys_cols(pred):
        acc = jnp.where(pred(nkeys_ref[...], past + col_n), 1.0, 0.0)
        for c in range(past // cw):
            acc = acc + _lane_fold(jnp.where(pred(keys_ref[:, c * cw:(c + 1) * cw], c * cw + col_c), 1.0, 0.0))
        return jnp.sum(acc, axis=1, keepdims=True)

    thr = _select_threshold(lambda cand: count_keys(lambda kk: kk >= cand), rows, float(topk))
    thr = jnp.maximum(thr, MIN_THRESHOLD)
    over = count_keys(lambda kk: kk >= thr) > float(topk)
    thr_ref[...] = jnp.broadcast_to(thr, (rows, LANES))
    cut_ref[...] = jnp.full((rows, LANES), 1 << idx_bits, I32)

    @pl.when(jnp.max(jnp.where(over, 1.0, 0.0)) > 0.0)
    def _():
        need = float(topk) - count_keys(lambda kk: kk > thr)
        cut = _tie_cutoff(count_keys_cols, thr, need, idx_bits)
        cut_ref[...] = jnp.broadcast_to(jnp.where(over, cut, jnp.int32(1 << idx_bits)), (rows, LANES))


def _sample_thr(keys, nkeys, t, rows):
    n, past = keys.shape
    topk = min(TOPK_MAX, (past + t) // 4)
    blk = lambda width: pl.BlockSpec((rows, width), lambda i: (i, 0))
    return pl.pallas_call(
        functools.partial(_sample_thr_body, rows=rows, topk=topk,
                          idx_bits=(past + LANES - 1).bit_length(), cw=2048),
        grid=(n // rows,),
        in_specs=[blk(past), blk(LANES)],
        out_specs=[blk(LANES), blk(LANES)],
        out_shape=[jax.ShapeDtypeStruct((n, LANES), F32), jax.ShapeDtypeStruct((n, LANES), I32)],
        compiler_params=_cparams(("arbitrary",)),
        name="sample_thr",
    )(keys, nkeys)


def _sample_attn_body(pt_ref, keys_ref, nkeys_ref, thr_ref, cut_ref, q_ref, kn_ref, vn_ref, *refs,
                      pg, t, past):
    del pt_ref
    kpages, vpages = refs[:pg], refs[pg:2 * pg]
    o_ref = refs[2 * pg]
    qs_ref, kt_ref, vt_ref, m_ref, l_ref, acc_ref = refs[2 * pg + 1:]
    g = pl.program_id(1)
    rows = N_HEADS * t

    def flash_update(s, sel, pv):
        n = s.shape[-1]
        s = jnp.where(sel[None], s.reshape(N_HEADS, t, n), NEG_BIG).reshape(rows, n)
        m_old = m_ref[...]
        m_new = jnp.maximum(m_old, jnp.max(s, axis=1, keepdims=True))
        alpha = jnp.exp2(m_old - m_new)
        p = jnp.exp2(s - m_new)
        l_ref[...] = alpha * l_ref[...] + jnp.sum(p, axis=1, keepdims=True)
        acc_ref[...] = alpha * acc_ref[...] + pv(p.astype(BF16))
        m_ref[...] = m_new

    @pl.when(g == 0)
    def _():
        lo = lax.broadcasted_iota(I32, (t, LANES), 1) < HEAD_DIM
        for c in range(N_HEADS // 2):
            qc = q_ref[:, c * LANES:(c + 1) * LANES]
            qs_ref[c * t:(c + 1) * t, :] = jnp.where(lo, qc, jnp.zeros_like(qc))
            qs_ref[(c + 4) * t:(c + 5) * t, :] = jnp.where(lo, jnp.zeros_like(qc), qc)
        m_ref[...] = jnp.full(m_ref.shape, NEG_BIG, F32)
        l_ref[...] = jnp.zeros(l_ref.shape, F32)
        acc_ref[...] = jnp.zeros(acc_ref.shape, F32)

    thr = thr_ref[:, 0:1]
    cut = cut_ref[:, 0:1]

    def selected(kk, col):
        return (kk > thr) | ((kk == thr) & (col < cut))

    qs = qs_ref[...]
    width = pg * PAGE_SIZE
    for j in range(pg):
        kt_ref[:, j * PAGE_SIZE:(j + 1) * PAGE_SIZE] = kpages[j][...].astype(BF16)
        vt_ref[:, j * PAGE_SIZE:(j + 1) * PAGE_SIZE] = vpages[j][...].astype(BF16)
    col_w = g * width + lax.broadcasted_iota(I32, (t, width), 1)
    flash_update(_dot(qs, kt_ref[...]), selected(keys_ref[...], col_w),
                 lambda p: _dot_t(p, vt_ref[...]))

    @pl.when(g == pl.num_programs(1) - 1)
    def _():
        pad = jnp.zeros((LANES - t, LANES), F32)
        kn = jnp.concatenate([kn_ref[...], pad], axis=0).astype(BF16)
        vn = jnp.concatenate([vn_ref[...], pad], axis=0).astype(BF16)
        col_n = past + lax.broadcasted_iota(I32, (t, LANES), 1)
        flash_update(_dot_t(qs, kn), selected(nkeys_ref[...], col_n), lambda p: _dot(p, vn))
        lo = lax.broadcasted_iota(I32, (t, LANES), 1) < HEAD_DIM
        out = acc_ref[...] / l_ref[...]
        for c in range(N_HEADS // 2):
            oc = jnp.where(lo, out[c * t:(c + 1) * t], out[(c + 4) * t:(c + 5) * t])
            o_ref[:, c * LANES:(c + 1) * LANES] = oc.astype(BF16)


def _sample_attn(page_table, keys, nkeys, thr, cut, q, k_new, v_new, cache_kt, cache_vt, db, t, pg):
    n_pages = page_table.shape[1]
    past = n_pages * PAGE_SIZE
    page_spec = lambda j: pl.BlockSpec((None, KV_DIM, PAGE_SIZE),
                                       lambda b, g, pt: (pt[b, g * pg + j], 0, 0))
    tok = lambda width: pl.BlockSpec((t, width), lambda b, g, pt: (b, 0))
    grid_spec = pltpu.PrefetchScalarGridSpec(
        num_scalar_prefetch=1,
        grid=(db, n_pages // pg),
        in_specs=[pl.BlockSpec((t, pg * PAGE_SIZE), lambda b, g, pt: (b, g)),
                  tok(LANES), tok(LANES), tok(LANES), tok(ATTN_DIM), tok(KV_DIM), tok(KV_DIM)]
                 + [page_spec(j) for j in range(pg)] + [page_spec(j) for j in range(pg)],
        out_specs=tok(ATTN_DIM),
        scratch_shapes=[
            pltpu.VMEM((N_HEADS * t, LANES), BF16),
            pltpu.VMEM((KV_DIM, pg * PAGE_SIZE), BF16),
            pltpu.VMEM((KV_DIM, pg * PAGE_SIZE), BF16),
            pltpu.VMEM((N_HEADS * t, 1), F32),
            pltpu.VMEM((N_HEADS * t, 1), F32),
            pltpu.VMEM((N_HEADS * t, LANES), F32),
        ],
    )
    return pl.pallas_call(
        functools.partial(_sample_attn_body, pg=pg, t=t, past=past),
        grid_spec=grid_spec,
        out_shape=jax.ShapeDtypeStruct((db * t, ATTN_DIM), BF16),
        compiler_params=_cparams(("arbitrary", "arbitrary")),
        name="sample_attn",
    )(page_table, keys, nkeys, thr, cut, q, k_new, v_new, *([cache_kt] * pg), *([cache_vt] * pg))


def _out_body(x_ref, a_ref, c_ref, wa_ref, wc_ref, o_ref):
    o_ref[...] = x_ref[...] + _dot(a_ref[...], wa_ref[...]) + _dot(c_ref[...], wc_ref[...])


def _out_proj(x, attn, conv, wa, wc, tm):
    n = x.shape[0]
    tok = lambda width: pl.BlockSpec((tm, width), lambda i: (i, 0))
    full = lambda a: pl.BlockSpec(a.shape, lambda i: (0, 0))
    return pl.pallas_call(
        _out_body,
        grid=(n // tm,),
        in_specs=[tok(D_MODEL), tok(ATTN_DIM), tok(CONV_DIM), full(wa), full(wc)],
        out_specs=tok(D_MODEL),
        out_shape=jax.ShapeDtypeStruct((n, D_MODEL), F32),
        compiler_params=_cparams(("arbitrary",)),
        name="out_proj",
    )(x, attn, conv, wa, wc)


def _ple_body(x_ref, pe_ref, g_ref, wg_ref, wp_ref, gf_ref, o_ref):
    x = x_ref[...]
    gate = jax.nn.sigmoid(_dot(_rms(x, g_ref[...]).astype(BF16), wg_ref[...]))
    x = x + gate * _dot(pe_ref[...].astype(BF16), wp_ref[...])
    o_ref[...] = _rms(x, gf_ref[...])


def _ple_final(x, pe, g, wg, wp, gf, tm):
    n = x.shape[0]
    tok = lambda width: pl.BlockSpec((tm, width), lambda i: (i, 0))
    full = lambda a: pl.BlockSpec(a.shape, lambda i: (0, 0))
    return pl.pallas_call(
        _ple_body,
        grid=(n // tm,),
        in_specs=[tok(D_MODEL), tok(PLE_DIM), full(g), full(wg), full(wp), full(gf)],
        out_specs=tok(D_MODEL),
        out_shape=jax.ShapeDtypeStruct((n, D_MODEL), F32),
        compiler_params=_cparams(("arbitrary",)),
        name="ple_final",
    )(x, pe, g, wg, wp, gf)


def _rope_tables(pos):
    half = HEAD_DIM // 2
    inv = ROPE_THETA ** (-jnp.arange(half, dtype=F32) / half)
    ang = pos.astype(F32)[:, None] * inv[None, :]
    cos, sin = jnp.cos(ang), jnp.sin(ang)
    cos2 = jnp.concatenate([cos, cos], axis=1)
    sin2 = jnp.concatenate([-sin, sin], axis=1)
    cos128 = jnp.concatenate([cos2, cos2], axis=1)
    sin128 = jnp.concatenate([sin2, sin2], axis=1)
    cosk = jnp.concatenate([cos2, jnp.ones_like(cos2)], axis=1)
    sink = jnp.concatenate([sin2, jnp.zeros_like(sin2)], axis=1)
    return cos128, sin128, cosk, sink


def _prep_w_in(w_in):
    d = w_in.shape[0]
    q = w_in[:, :ATTN_DIM].reshape(d, N_HEADS, HEAD_DIM)[:, jnp.array(_HEAD_PERM)].reshape(d, ATTN_DIM)
    o = ATTN_DIM
    kv = w_in[:, o:o + 2 * KV_DIM]
    o += 2 * KV_DIM
    iq = w_in[:, o:o + N_IDX_HEADS * IDX_DIM]
    o += N_IDX_HEADS * IDX_DIM
    ikw = w_in[:, o:o + IDX_DIM + N_IDX_HEADS]
    o += IDX_DIM + N_IDX_HEADS
    ikw = jnp.pad(ikw, ((0, 0), (0, LANES - IDX_DIM - N_IDX_HEADS)))
    conv = w_in[:, o:]
    return jnp.concatenate([q, kv, iq, ikw, conv], axis=1).astype(BF16)


def _layer(x, pe, mixer, tm, g_ffn1, w1g, w1u, w1d, g_mix, w_in, conv_w, wo_a, wo_c,
           g_ffn2, w2g, w2u, w2d, g_ple, w_pg, w_pp, g_final):
    x1 = _ffn(x, g_ffn1, w1g, w1u, w1d, tm)
    attn, conv, state = mixer(x1, g_mix, w_in, conv_w)
    x3 = _mix_ffn(x1, attn, conv, wo_a, wo_c, g_ffn2, w2g, w2u, w2d, tm)
    return _ple_final(x3, pe, g_ple, w_pg, w_pp, g_final, tm), state


def kernel(x_prompt, x_sample, cache_k, cache_v, cache_idx_k, state_conv, page_table, p_prompt, p_sample,
           g_ffn1, w1_gate, w1_up, w1_down, g_mix, w_in, conv_w, w_out,
           g_ffn2, w2_gate, w2_up, w2_down, g_ple, w_ple_gate, w_ple_proj, g_final):
    depth = w_in.shape[0]
    assert depth == 1, "single-layer step"
    batch, seq, _ = x_prompt.shape
    db, t, _ = x_sample.shape
    n_pages = page_table.shape[1]
    past = n_pages * PAGE_SIZE
    assert t == SUBLANES and seq % 512 == 0

    row = lambda gvec: gvec.reshape(1, -1)
    perm = jnp.array(_HEAD_PERM)
    wo_a = w_out[0, :ATTN_DIM].reshape(N_HEADS, HEAD_DIM, D_MODEL)[perm].reshape(ATTN_DIM, D_MODEL).astype(BF16)
    wo_c = w_out[0, ATTN_DIM:].astype(BF16)
    shared = (row(g_ffn1[0]), w1_gate[0].astype(BF16), w1_up[0].astype(BF16), w1_down[0].astype(BF16),
              row(g_mix[0]), _prep_w_in(w_in[0]), conv_w[0], wo_a, wo_c,
              row(g_ffn2[0]), w2_gate[0].astype(BF16), w2_up[0].astype(BF16), w2_down[0].astype(BF16),
              row(g_ple[0]), w_ple_gate[0].astype(BF16), w_ple_proj[0].astype(BF16), row(g_final))

    tm_p = 512
    tabs_p = _rope_tables(jnp.arange(seq, dtype=I32))

    def prompt_mixer(x1, g, w, cw):
        q, iq, k, v, ikw, kb, vb, ik2, cy, tail = _proj(x1, g, w, *tabs_p, cw, tm_p, seq // tm_p)
        attn = _attn_prompt(q, iq, ikw, ik2, kb, vb, batch, seq, qb=256, kc=512)
        return attn, cy, (k, v, ikw, tail)

    yp, (kp, vp, ikwp, tailp) = _layer(x_prompt.reshape(batch * seq, D_MODEL),
                                       p_prompt[0].reshape(batch * seq, PLE_DIM), prompt_mixer, tm_p, *shared)

    n_s = db * t
    tabs_s = tuple(jnp.tile(a, (db, 1)) for a in _rope_tables(past + jnp.arange(t, dtype=I32)))
    buf = state_conv[0]
    zero = jnp.zeros((db, t - 2, CONV_DIM), F32)
    halo1 = jnp.concatenate([buf[:, 1:2], jnp.zeros((db, t - 1, CONV_DIM), F32)], axis=1).reshape(n_s, CONV_DIM)
    halo2 = jnp.concatenate([buf, zero], axis=1).reshape(n_s, CONV_DIM)
    idx_t = jnp.transpose(cache_idx_k[0], (0, 2, 1))
    k_t = jnp.transpose(cache_k[0], (0, 2, 3, 1)).reshape(-1, KV_DIM, PAGE_SIZE)
    v_t = jnp.transpose(cache_v[0], (0, 2, 3, 1)).reshape(-1, KV_DIM, PAGE_SIZE)

    def sample_mixer(x1, g, w, cw):
        q, iq, k, v, ikw, kb, vb, ik2, cy, u = _proj(x1, g, w, *tabs_s, cw, n_s, 1, halos=(halo1, halo2))
        keys, nkeys = _sample_scores(page_table, iq, ikw, idx_t, db, t, pg=min(32, n_pages))
        thr, cut = _sample_thr(keys, nkeys, t, rows=min(128, n_s))
        attn = _sample_attn(page_table, keys, nkeys, thr, cut, q, k, v, k_t, v_t, db, t, pg=min(32, n_pages))
        return attn, cy, (k, v, ikw, u)

    ys, (ks, vs, ikws, us) = _layer(x_sample.reshape(n_s, D_MODEL), p_sample[0].reshape(n_s, PLE_DIM),
                                    sample_mixer, n_s, *shared)

    return (yp.reshape(batch, seq, D_MODEL),
            ys.reshape(db, t, D_MODEL),
            kp.reshape(batch, N_KV_HEADS, HEAD_DIM, seq).transpose(0, 3, 1, 2)[None],
            vp.reshape(batch, N_KV_HEADS, HEAD_DIM, seq).transpose(0, 3, 1, 2)[None],
            ikwp[:, :IDX_DIM].reshape(1, batch, seq, IDX_DIM),
            tailp[:, SUBLANES - (CONV_WIDTH - 1):][None],
            ks.reshape(1, db, t, N_KV_HEADS, HEAD_DIM),
            vs.reshape(1, db, t, N_KV_HEADS, HEAD_DIM),
            ikws[:, :IDX_DIM].reshape(1, db, t, IDX_DIM),
            us.reshape(db, t, CONV_DIM)[:, t - (CONV_WIDTH - 1):][None])
```
